```python
import jax, jax.numpy as jnp
from jax import lax
import numpy as np

D_MODEL = 2048
BATCH = 4
SEQ = 2048
DEPTH = 4
DEC_BATCH = 8
DEC_SEQ = 4
PAST_LEN = 16384
PAGE_SIZE = 128

N_MIXERS = 3
N_A = len([i for i in range(DEPTH) if i % N_MIXERS == 0])
N_B = len([i for i in range(DEPTH) if i % N_MIXERS == 1])
N_C = len([i for i in range(DEPTH) if i % N_MIXERS == 2])
A_CHUNK = 128
A_WIDTH = D_MODEL
A_HEADS = 16
A_HD = A_WIDTH // A_HEADS
B_WINDOWS = (128, 512, 2048)
B_DILATIONS = (1, 4, 16)
B_GROUPS = 3
B_HD = 128
B_HEADS = D_MODEL // B_HD
B_WIDTH = B_HEADS * B_HD
B_SCALE = B_HD ** -0.5
POOL_WINDOWS = (2, 4, 8, 16)
C_GROUPS = 4
C_GW = D_MODEL // C_GROUPS
POOL_STATE = max(POOL_WINDOWS) - 1
D_FF = ((8 * D_MODEL + 3 * 256 - 1) // (3 * 256)) * 256
EPS = 1e-6

kernel_name = 'hybrid_gmlp_dilated_pool_decoder_step'


def rmsnorm(x, g):
    xf = x.astype(jnp.float32)
    r = lax.rsqrt(jnp.mean(xf * xf, axis=-1, keepdims=True) + EPS)
    return (xf * r).astype(x.dtype) * g


def layernorm(x, g, b):
    xf = x.astype(jnp.float32)
    mu = jnp.mean(xf, axis=-1, keepdims=True)
    var = jnp.mean(jnp.square(xf - mu), axis=-1, keepdims=True)
    return ((xf - mu) * lax.rsqrt(var + EPS)).astype(x.dtype) * g + b


def swiglu(h, w1, w3, w2):
    return (jax.nn.silu(h @ w1) * (h @ w3)) @ w2


def chunk_mlp(h, w_in, ln_g, ln_b, w_s, b_s, w_out):
    bsz, s_len, _ = h.shape
    z = jax.nn.gelu(h @ w_in)
    u, v = z[..., :A_WIDTH], z[..., A_WIDTH:]
    v = layernorm(v, ln_g, ln_b)
    t_len = min(s_len, A_CHUNK)
    n_chunks = s_len // t_len
    vb = v.reshape(bsz, n_chunks, t_len, A_HEADS, A_HD)
    w = w_s[:, :t_len, :t_len] * jnp.tril(jnp.ones((t_len, t_len), w_s.dtype))
    mixed = jnp.einsum('hij,bcjhd->bcihd', w, vb) + jnp.transpose(b_s[:, :t_len])[None, None, :, :, None]
    y = (u * mixed.reshape(bsz, s_len, A_WIDTH)) @ w_out
    return y, v


def qkv_groups(h, w_qkv, q_g, k_g):
    bsz, s_len, _ = h.shape
    qkv = (h @ w_qkv).reshape(bsz, s_len, B_GROUPS, 3, B_HEADS, B_HD)
    q = rmsnorm(qkv[:, :, :, 0], q_g[:, None, :])
    k = rmsnorm(qkv[:, :, :, 1], k_g[:, None, :])
    return q, k, qkv[:, :, :, 2]


def dilated_prompt_group(q, k, v, dil, n):
    bsz, s_len, nh, hd = q.shape
    L = s_len // dil
    nb = -(-L // n)
    lp = nb * n

    def to_blocks(a):
        a = a.reshape(bsz, L, dil, nh, hd).transpose(0, 2, 1, 3, 4)
        a = jnp.pad(a, ((0, 0), (0, 0), (0, lp - L), (0, 0), (0, 0)))
        return a.reshape(bsz, dil, nb, n, nh, hd)

    def with_prev(a):
        prev = jnp.pad(a, ((0, 0), (0, 0), (1, 0), (0, 0), (0, 0), (0, 0)))[:, :, :-1]
        return jnp.concatenate([prev, a], axis=3)

    qb = to_blocks(q)
    kb = with_prev(to_blocks(k))
    vb = with_prev(to_blocks(v))
    s = jnp.einsum('brcqhe,brckhe->brchqk', qb, kb).astype(jnp.float32) * B_SCALE
    qi = jnp.arange(n)[:, None]
    kj = jnp.arange(2 * n)[None, :]
    dist = n + qi - kj
    key_idx = jnp.arange(nb)[:, None, None] * n + kj - n
    valid = (dist >= 0) & (dist <= n) & (key_idx >= 0)
    s = jnp.where(valid[None, None, :, None], s, -jnp.inf)
    m = jnp.max(s, axis=-1)
    p = jnp.exp(s - m[..., None])
    l = jnp.sum(p, axis=-1)
    o = jnp.einsum('brchqk,brckhe->brcqhe', p.astype(v.dtype), vb).astype(jnp.float32)
    o = o / jnp.swapaxes(l, -1, -2)[..., None]

    def from_blocks(a):
        a = a.reshape((bsz, dil, lp) + a.shape[4:])[:, :, :L]
        a = jnp.swapaxes(a, 1, 2)
        return a.reshape((bsz, s_len) + a.shape[3:])

    return from_blocks(o), from_blocks(jnp.swapaxes(m, -1, -2)), from_blocks(jnp.swapaxes(l, -1, -2))


def dilated_sample_group(q, k_all, v_all, dil, n):
    t_len = q.shape[1]
    r = k_all.shape[1] - t_len
    idx = r + jnp.arange(t_len)[:, None] - dil * jnp.arange(n + 1)[None, :]
    valid = idx >= 0
    idx = jnp.maximum(idx, 0)
    kg = k_all[:, idx]
    vg = v_all[:, idx]
    s = jnp.einsum('bthe,btkhe->bthk', q, kg).astype(jnp.float32) * B_SCALE
    s = jnp.where(valid[None, :, None, :], s, -jnp.inf)
    m = jnp.max(s, axis=-1)
    p = jnp.exp(s - m[..., None])
    l = jnp.sum(p, axis=-1)
    o = jnp.einsum('bthk,btkhe->bthe', p.astype(v_all.dtype), vg).astype(jnp.float32) / l[..., None]
    return o, m, l


def combine_groups(results):
    o = jnp.stack([r[0] for r in results])
    m = jnp.stack([r[1] for r in results])
    l = jnp.stack([r[2] for r in results])
    w = l * jnp.exp(m - jnp.max(m, axis=0, keepdims=True))
    return jnp.sum(w[..., None] * o, axis=0) / jnp.sum(w, axis=0)[..., None]


def dilated_attention_prompt(h, w_qkv, q_g, k_g, w_out):
    bsz, s_len, _ = h.shape
    q, k, v = qkv_groups(h, w_qkv, q_g, k_g)
    results, new_kv = [], []
    for g in range(B_GROUPS):
        dil = B_DILATIONS[g]
        results.append(dilated_prompt_group(q[:, :, g], k[:, :, g], v[:, :, g], dil, B_WINDOWS[g] // dil))
        keep = min(B_WINDOWS[g], s_len)
        new_kv.append(jnp.stack([k[:, s_len - keep:, g], v[:, s_len - keep:, g]], axis=2))
    o = combine_groups(results)
    y = o.reshape(bsz, s_len, B_WIDTH).astype(h.dtype) @ w_out
    return y, new_kv


def dilated_attention_sample(h, kv_caches, w_qkv, q_g, k_g, w_out):
    bsz, t_len, _ = h.shape
    q, k, v = qkv_groups(h, w_qkv, q_g, k_g)
    results, new_kv = [], []
    for g in range(B_GROUPS):
        dil = B_DILATIONS[g]
        kv = kv_caches[g]
        k_all = jnp.concatenate([kv[:, :, 0], k[:, :, g]], axis=1)
        v_all = jnp.concatenate([kv[:, :, 1], v[:, :, g]], axis=1)
        results.append(dilated_sample_group(q[:, :, g], k_all, v_all, dil, B_WINDOWS[g] // dil))
        new_kv.append(jnp.stack([k[:, :, g], v[:, :, g]], axis=2))
    o = combine_groups(results)
    y = o.reshape(bsz, t_len, B_WIDTH).astype(h.dtype) @ w_out
    return y, new_kv


def pool_mix(seq, n_new, pos0, w_c, c_scale):
    bsz, L, d = seq.shape
    r = L - n_new
    sf = seq.astype(jnp.float32)
    cs = jnp.concatenate([jnp.zeros((bsz, 1, d), jnp.float32), jnp.cumsum(sf, axis=1)], axis=1)
    upper = cs[:, r + 1:]
    pos = pos0 + jnp.arange(n_new)
    outs = []
    for g, w in enumerate(POOL_WINDOWS):
        sl = slice(g * C_GW, (g + 1) * C_GW)
        lo_idx = jnp.maximum(jnp.arange(r + 1, L + 1) - w, 0)
        lower = cs[:, lo_idx, sl]
        cnt = jnp.minimum(w, pos + 1).astype(jnp.float32)
        pooled = (upper[:, :, sl] - lower) / cnt[:, None]
        outs.append(pooled - sf[:, r:, sl])
    z = jnp.stack(outs, axis=2).astype(seq.dtype)
    y = jnp.einsum('btgc,gce->btge', z, w_c).reshape(bsz, n_new, d)
    return y * c_scale


def setup_inputs(seed: int = 0) -> dict:
    key = jax.random.key(seed)
    ks = jax.random.split(key, 32)
    f32 = jnp.float32

    def nrm(k, shape, scale):
        return jax.random.normal(k, shape, f32) * scale

    inp = {}
    inp['x_prompt'] = nrm(ks[0], (BATCH, SEQ, D_MODEL), 1.0)
    inp['x_sample'] = nrm(ks[1], (DEC_BATCH, DEC_SEQ, D_MODEL), 1.0)
    for g in range(B_GROUPS):
        inp['cache_b_kv%d' % g] = nrm(ks[2 + g], (N_B, DEC_BATCH, min(B_WINDOWS[g], PAST_LEN), 2, B_HEADS, B_HD), 1.0)
    inp['state_c_pool'] = nrm(ks[5], (N_C, DEC_BATCH, POOL_STATE, D_MODEL), 1.0)
    inp['norm_mix_g'] = 1.0 + nrm(ks[6], (DEPTH, D_MODEL), 0.02)
    inp['norm_ffn_g'] = 1.0 + nrm(ks[7], (DEPTH, D_MODEL), 0.02)
    inp['a_w_in'] = nrm(ks[8], (N_A, D_MODEL, 2 * A_WIDTH), D_MODEL ** -0.5)
    inp['a_ln_g'] = 1.0 + nrm(ks[9], (N_A, A_WIDTH), 0.02)
    inp['a_ln_b'] = nrm(ks[10], (N_A, A_WIDTH), 0.02)
    inp['a_w_s'] = nrm(ks[11], (N_A, A_HEADS, A_CHUNK, A_CHUNK), A_CHUNK ** -0.5)
    inp['a_b_s'] = 1.0 + nrm(ks[12], (N_A, A_HEADS, A_CHUNK), 0.02)
    inp['a_w_out'] = nrm(ks[13], (N_A, A_WIDTH, D_MODEL), A_WIDTH ** -0.5)
    inp['b_w_qkv'] = nrm(ks[14], (N_B, D_MODEL, B_GROUPS * 3 * B_WIDTH), D_MODEL ** -0.5)
    inp['b_q_g'] = 1.0 + nrm(ks[15], (N_B, B_GROUPS, B_HD), 0.02)
    inp['b_k_g'] = 1.0 + nrm(ks[16], (N_B, B_GROUPS, B_HD), 0.02)
    inp['b_w_out'] = nrm(ks[17], (N_B, B_WIDTH, D_MODEL), B_WIDTH ** -0.5)
    inp['c_w'] = nrm(ks[18], (N_C, C_GROUPS, C_GW, C_GW), C_GW ** -0.5)
    inp['c_scale'] = 1.0 + nrm(ks[19], (N_C, D_MODEL), 0.02)
    inp['ffn_w1'] = nrm(ks[20], (DEPTH, D_MODEL, D_FF), D_MODEL ** -0.5)
    inp['ffn_w3'] = nrm(ks[21], (DEPTH, D_MODEL, D_FF), D_MODEL ** -0.5)
    inp['ffn_w2'] = nrm(ks[22], (DEPTH, D_FF, D_MODEL), D_FF ** -0.5)
    return inp


def reference(x_prompt, x_sample, cache_b_kv0, cache_b_kv1, cache_b_kv2, state_c_pool,
              norm_mix_g, norm_ffn_g,
              a_w_in, a_ln_g, a_ln_b, a_w_s, a_b_s, a_w_out,
              b_w_qkv, b_q_g, b_k_g, b_w_out,
              c_w, c_scale,
              ffn_w1, ffn_w3, ffn_w2):
    xp, xs = x_prompt, x_sample
    n_new = xs.shape[1]
    a_v_s, pool_p, pool_s = [], [], []
    kv_p = [[] for _ in range(B_GROUPS)]
    kv_s = [[] for _ in range(B_GROUPS)]
    ia = ib = ic = 0
    for layer in range(DEPTH):
        kind = layer % N_MIXERS
        hp = rmsnorm(xp, norm_mix_g[layer])
        hs = rmsnorm(xs, norm_mix_g[layer])
        if kind == 0:
            yp, _ = chunk_mlp(hp, a_w_in[ia], a_ln_g[ia], a_ln_b[ia], a_w_s[ia], a_b_s[ia], a_w_out[ia])
            ys, v_s = chunk_mlp(hs, a_w_in[ia], a_ln_g[ia], a_ln_b[ia], a_w_s[ia], a_b_s[ia], a_w_out[ia])
            a_v_s.append(v_s)
            ia += 1
        elif kind == 1:
            yp, kvp = dilated_attention_prompt(hp, b_w_qkv[ib], b_q_g[ib], b_k_g[ib], b_w_out[ib])
            caches = (cache_b_kv0[ib], cache_b_kv1[ib], cache_b_kv2[ib])
            ys, kvs = dilated_attention_sample(hs, caches, b_w_qkv[ib], b_q_g[ib], b_k_g[ib], b_w_out[ib])
            for g in range(B_GROUPS):
                kv_p[g].append(kvp[g])
                kv_s[g].append(kvs[g])
            ib += 1
        else:
            yp = pool_mix(hp, hp.shape[1], 0, c_w[ic], c_scale[ic])
            seq_s = jnp.concatenate([state_c_pool[ic], hs], axis=1)
            ys = pool_mix(seq_s, n_new, PAST_LEN, c_w[ic], c_scale[ic])
            pool_p.append(hp[:, -POOL_STATE:])
            pool_s.append(seq_s[:, -POOL_STATE:])
            ic += 1
        xp = xp + yp
        xs = xs + ys
        xp = xp + swiglu(rmsnorm(xp, norm_ffn_g[layer]), ffn_w1[layer], ffn_w3[layer], ffn_w2[layer])
        xs = xs + swiglu(rmsnorm(xs, norm_ffn_g[layer]), ffn_w1[layer], ffn_w3[layer], ffn_w2[layer])
    new_a_v_sample = jnp.stack(a_v_s)
    new_b_kv0_prompt = jnp.stack(kv_p[0])
    new_b_kv1_prompt = jnp.stack(kv_p[1])
    new_b_kv2_prompt = jnp.stack(kv_p[2])
    new_b_kv0_sample = jnp.stack(kv_s[0])
    new_b_kv1_sample = jnp.stack(kv_s[1])
    new_b_kv2_sample = jnp.stack(kv_s[2])
    new_c_pool_prompt = jnp.stack(pool_p)
    new_c_pool_sample = jnp.stack(pool_s)
    return (xp, xs, new_a_v_sample,
            new_b_kv0_prompt, new_b_kv1_prompt, new_b_kv2_prompt,
            new_b_kv0_sample, new_b_kv1_sample, new_b_kv2_sample,
            new_c_pool_prompt, new_c_pool_sample)
```

```python
import functools

import jax
import jax.numpy as jnp
from jax import lax
from jax.experimental import pallas as pl
from jax.experimental.pallas import tpu as pltpu

F32 = jnp.float32
BF16 = jnp.bfloat16

D_MODEL = 2048
BATCH = 4
SEQ = 2048
DEPTH = 4
DEC_BATCH = 8
DEC_SEQ = 4
PAST_LEN = 16384
N_MIXERS = 3
A_CHUNK = 128
A_WIDTH = D_MODEL
A_HEADS = 16
A_HD = A_WIDTH // A_HEADS
B_WINDOWS = (128, 512, 2048)
B_DILATIONS = (1, 4, 16)
B_GROUPS = 3
B_HD = 128
B_HEADS = D_MODEL // B_HD
B_WIDTH = B_HEADS * B_HD
B_SCALE = B_HD ** -0.5
B_BAND = 128
POOL_WINDOWS = (2, 4, 8, 16)
C_GROUPS = 4
C_GW = D_MODEL // C_GROUPS
POOL_STATE = max(POOL_WINDOWS) - 1
POOL_PAD = POOL_STATE + 1
D_FF = ((8 * D_MODEL + 3 * 256 - 1) // (3 * 256)) * 256
EPS = 1e-6

SUBLANES = 8
LANES = 128
VMEM_LIMIT_BYTES = 56 * 1024 * 1024

SAMPLE_PAD = SUBLANES
MP = BATCH * SEQ
MS = DEC_BATCH * SAMPLE_PAD
NEG_BIG = -1e30


def _params(semantics):
    return pltpu.CompilerParams(dimension_semantics=semantics,
                                vmem_limit_bytes=VMEM_LIMIT_BYTES)


def _dot(a, b):
    return jnp.dot(a, b, preferred_element_type=F32)


def _dot_nt(a, b):
    return lax.dot_general(a, b, (((1,), (1,)), ((), ())), preferred_element_type=F32)


def _rms_rows(x, g):
    r = lax.rsqrt(jnp.mean(x * x, axis=-1, keepdims=True) + EPS)
    return (x * r) * g


def _for_row_chunks(rows, chunk, fn):
    chunk = min(chunk, rows)
    n = rows // chunk
    if n == 1:
        fn(pl.ds(0, chunk))
        return

    def body(c, carry):
        fn(pl.ds(pl.multiple_of(c * chunk, chunk), chunk))
        return carry

    lax.fori_loop(0, n, body, 0)


def _ffn_body(x_ref, g_ref, w1_ref, w3_ref, w2_ref, o_ref, hn_ref):
    j = pl.program_id(1)

    @pl.when(j == 0)
    def _():
        def chunk(rows):
            x = x_ref[rows, :]
            hn_ref[rows, :] = _rms_rows(x, g_ref[...]).astype(BF16)
            o_ref[rows, :] = x
        _for_row_chunks(x_ref.shape[0], 256, chunk)

    h = hn_ref[...]
    a = _dot(h, w1_ref[...].astype(BF16))
    b = _dot(h, w3_ref[...].astype(BF16))
    gate = (jax.nn.silu(a) * b).astype(BF16)
    o_ref[...] += _dot(gate, w2_ref[...].astype(BF16))


def _ffn(x, g, w1, w3, w2, *, tm, tf):
    m = x.shape[0]
    return pl.pallas_call(
        _ffn_body,
        grid=(m // tm, D_FF // tf),
        in_specs=[
            pl.BlockSpec((tm, D_MODEL), lambda i, j: (i, 0)),
            pl.BlockSpec((1, D_MODEL), lambda i, j: (0, 0)),
            pl.BlockSpec((D_MODEL, tf), lambda i, j: (0, j)),
            pl.BlockSpec((D_MODEL, tf), lambda i, j: (0, j)),
            pl.BlockSpec((tf, D_MODEL), lambda i, j: (j, 0)),
        ],
        out_specs=pl.BlockSpec((tm, D_MODEL), lambda i, j: (i, 0)),
        out_shape=jax.ShapeDtypeStruct((m, D_MODEL), F32),
        scratch_shapes=[pltpu.VMEM((tm, D_MODEL), BF16)],
        compiler_params=_params(("parallel", "arbitrary")),
        name="ffn",
    )(x, g.reshape(1, D_MODEL), w1, w3, w2)


def _mm_body(*refs, norm, epilogue, heads_per_tile):
    it = iter(refs)
    x_ref = next(it)
    g_ref = next(it) if norm else None
    w_ref = next(it)
    gain_ref = next(it) if epilogue == "qknorm" else None
    res_ref = next(it) if epilogue == "residual" else None
    o_ref = next(it)
    h_ref = next(it)
    j = pl.program_id(1)

    @pl.when(j == 0)
    def _():
        def chunk(rows):
            x = x_ref[rows, :]
            if norm:
                x = _rms_rows(x, g_ref[...])
            h_ref[rows, :] = x.astype(BF16)
        _for_row_chunks(x_ref.shape[0], 256, chunk)

    acc = _dot(h_ref[...], w_ref[...].astype(BF16))
    if epilogue == "gelu":
        o_ref[...] = jax.nn.gelu(acc, approximate=True)
    elif epilogue == "residual":
        o_ref[...] = res_ref[...] + acc
    elif epilogue == "qknorm":
        tiles_per_block = B_WIDTH // o_ref.shape[1]
        kind = (j // tiles_per_block) % 3

        @pl.when(kind < 2)
        def _():
            for hh in range(heads_per_tile):
                cs = slice(hh * B_HD, (hh + 1) * B_HD)
                o_ref[:, cs] = _rms_rows(acc[:, cs], gain_ref[:, cs])

        @pl.when(kind == 2)
        def _():
            o_ref[...] = acc
    else:
        o_ref[...] = acc


def _matmul(x, w, *, tm, tn, g=None, epilogue=None, gain=None, res=None, name="matmul"):
    m, k = x.shape
    n = w.shape[1]
    norm = g is not None
    args = [x]
    in_specs = [pl.BlockSpec((tm, k), lambda i, j: (i, 0))]
    if norm:
        args.append(g.reshape(1, k))
        in_specs.append(pl.BlockSpec((1, k), lambda i, j: (0, 0)))
    args.append(w)
    in_specs.append(pl.BlockSpec((k, tn), lambda i, j: (0, j)))
    if epilogue == "qknorm":
        args.append(gain)
        in_specs.append(pl.BlockSpec((1, tn), lambda i, j: (0, j)))
    if epilogue == "residual":
        args.append(res)
        in_specs.append(pl.BlockSpec((tm, tn), lambda i, j: (i, j)))
    body = functools.partial(_mm_body, norm=norm, epilogue=epilogue,
                             heads_per_tile=tn // B_HD)
    return pl.pallas_call(
        body,
        grid=(m // tm, n // tn),
        in_specs=in_specs,
        out_specs=pl.BlockSpec((tm, tn), lambda i, j: (i, j)),
        out_shape=jax.ShapeDtypeStruct((m, n), F32),
        scratch_shapes=[pltpu.VMEM((tm, k), BF16)],
        compiler_params=_params(("parallel", "arbitrary")),
        name=name,
    )(*args)


def _amix_body(x_ref, u_ref, v_ref, lng_ref, lnb_ref, ws_ref, bias_ref, wout_ref,
               o_ref, vout_ref, gated_ref, wsm_ref, *, t_len, block):
    j = pl.program_id(1)

    @pl.when(j == 0)
    def _():
        r = lax.broadcasted_iota(jnp.int32, (t_len, t_len), 0)
        c = lax.broadcasted_iota(jnp.int32, (t_len, t_len), 1)
        keep = c <= r
        if block < t_len:
            keep = keep & ((r // block) == (c // block)) & ((c % block) < DEC_SEQ)
        for h in range(A_HEADS):
            wsm_ref[h] = jnp.where(keep, ws_ref[h], 0.0).astype(BF16)

        def chunk(rows):
            v = v_ref[rows, :]
            mu = jnp.mean(v, axis=-1, keepdims=True)
            d = v - mu
            var = jnp.mean(d * d, axis=-1, keepdims=True)
            vln = (d * lax.rsqrt(var + EPS)) * lng_ref[...] + lnb_ref[...]
            vout_ref[rows, :] = vln
            vb = vln.astype(BF16)
            for h in range(A_HEADS):
                cs = slice(h * A_HD, (h + 1) * A_HD)
                mixed = _dot(wsm_ref[h], vb[:, cs]) + bias_ref[:, cs]
                gated_ref[rows, cs] = (u_ref[rows, cs] * mixed).astype(BF16)
        _for_row_chunks(v_ref.shape[0], t_len, chunk)

    o_ref[...] = x_ref[...] + _dot(gated_ref[...], wout_ref[...].astype(BF16))


def _amix(x, z, ln_g, ln_b, ws, bias, w_out, *, tm, tn, t_len, block):
    m = x.shape[0]
    body = functools.partial(_amix_body, t_len=t_len, block=block)
    return pl.pallas_call(
        body,
        grid=(m // tm, D_MODEL // tn),
        in_specs=[
            pl.BlockSpec((tm, tn), lambda i, j: (i, j)),
            pl.BlockSpec((tm, A_WIDTH), lambda i, j: (i, 0)),
            pl.BlockSpec((tm, A_WIDTH), lambda i, j: (i, 1)),
            pl.BlockSpec((1, A_WIDTH), lambda i, j: (0, 0)),
            pl.BlockSpec((1, A_WIDTH), lambda i, j: (0, 0)),
            pl.BlockSpec((A_HEADS, t_len, t_len), lambda i, j: (0, 0, 0)),
            pl.BlockSpec((t_len, A_WIDTH), lambda i, j: (0, 0)),
            pl.BlockSpec((A_WIDTH, tn), lambda i, j: (0, j)),
        ],
        out_specs=[
            pl.BlockSpec((tm, tn), lambda i, j: (i, j)),
            pl.BlockSpec((tm, A_WIDTH), lambda i, j: (i, 0)),
        ],
        out_shape=[jax.ShapeDtypeStruct((m, D_MODEL), F32),
                   jax.ShapeDtypeStruct((m, A_WIDTH), F32)],
        scratch_shapes=[pltpu.VMEM((tm, A_WIDTH), BF16),
                        pltpu.VMEM((A_HEADS, t_len, t_len), BF16)],
        compiler_params=_params(("parallel", "arbitrary")),
        name="amix",
    )(x, z, z, ln_g.reshape(1, A_WIDTH), ln_b.reshape(1, A_WIDTH), ws, bias, w_out)


def _attn_prompt_body(q0, k0, v0, q1, k1, v1, q2, k2, v2, o_ref, acc_ref, m_ref, l_ref):
    groups = ((q0, k0, v0), (q1, k1, v1), (q2, k2, v2))
    n = B_BAND
    qi = lax.broadcasted_iota(jnp.int32, (n, 2 * n), 0)
    kj = lax.broadcasted_iota(jnp.int32, (n, 2 * n), 1)
    dist = n + qi - kj
    band_mask = (dist >= 0) & (dist <= n)
    qi1 = lax.broadcasted_iota(jnp.int32, (n, n), 0)
    kj1 = lax.broadcasted_iota(jnp.int32, (n, n), 1)
    causal_mask = kj1 <= qi1

    for g, (q_ref, k_ref, v_ref) in enumerate(groups):
        dil = B_DILATIONS[g]
        n_blocks = SEQ // dil // n
        for r in range(dil):
            for c in range(n_blocks):
                start = r + c * n * dil
                if dil == 1:
                    rows_q = pl.ds(start, n)
                else:
                    rows_q = pl.ds(start, n, stride=dil)
                if c == 0:
                    rows_k, mask = rows_q, causal_mask
                elif dil == 1:
                    rows_k, mask = pl.ds(start - n, 2 * n), band_mask
                else:
                    rows_k, mask = pl.ds(start - n * dil, 2 * n, stride=dil), band_mask
                q = q_ref[0, rows_q, :].astype(BF16)
                k = k_ref[0, rows_k, :].astype(BF16)
                v = v_ref[0, rows_k, :].astype(BF16)
                s = _dot_nt(q, k) * B_SCALE
                s = jnp.where(mask, s, NEG_BIG)
                m = jnp.max(s, axis=-1, keepdims=True)
                p = jnp.exp(s - m)
                l = jnp.sum(p, axis=-1, keepdims=True)
                acc = _dot(p.astype(BF16), v)
                if g == 0:
                    acc_ref[rows_q, :] = acc
                    m_ref[rows_q, :] = jnp.broadcast_to(m, (n, LANES))
                    l_ref[rows_q, :] = jnp.broadcast_to(l, (n, LANES))
                else:
                    m_old = m_ref[rows_q, :]
                    m_new = jnp.maximum(m_old, m)
                    a_old = jnp.exp(m_old - m_new)
                    a_new = jnp.exp(m - m_new)
                    acc_new = acc_ref[rows_q, :] * a_old + acc * a_new
                    l_new = l_ref[rows_q, :] * a_old + l * a_new
                    if g == B_GROUPS - 1:
                        o_ref[0, rows_q, :] = acc_new / l_new
                    else:
                        acc_ref[rows_q, :] = acc_new
                        l_ref[rows_q, :] = l_new
                        m_ref[rows_q, :] = m_new


def _attn_prompt(qkv):
    def spec(g, kind):
        base = (g * 3 + kind) * B_HEADS
        return pl.BlockSpec((1, SEQ, B_HD), lambda b, h: (b, 0, base + h))

    in_specs = [spec(g, kind) for g in range(B_GROUPS) for kind in range(3)]
    return pl.pallas_call(
        _attn_prompt_body,
        grid=(BATCH, B_HEADS),
        in_specs=in_specs,
        out_specs=pl.BlockSpec((1, SEQ, B_HD), lambda b, h: (b, 0, h)),
        out_shape=jax.ShapeDtypeStruct((BATCH, SEQ, B_WIDTH), F32),
        scratch_shapes=[pltpu.VMEM((SEQ, B_HD), F32),
                        pltpu.VMEM((SEQ, LANES), F32),
                        pltpu.VMEM((SEQ, LANES), F32)],
        compiler_params=_params(("parallel", "parallel")),
        name="attn_prompt",
    )(*([qkv] * 9))


def _attn_sample_body(qkv_ref, c0_ref, c1_ref, c2_ref, o_ref):
    t_q = lax.broadcasted_iota(jnp.int32, (SAMPLE_PAD, SAMPLE_PAD), 0)
    t_k = lax.broadcasted_iota(jnp.int32, (SAMPLE_PAD, SAMPLE_PAD), 1)

    def cache_mask(n_keys, key_index_fn, g):
        dil = B_DILATIONS[g]
        r_len = B_WINDOWS[g]
        tq = lax.broadcasted_iota(jnp.int32, (SAMPLE_PAD, n_keys), 0)
        col = lax.broadcasted_iota(jnp.int32, (SAMPLE_PAD, n_keys), 1)
        d = r_len + tq - key_index_fn(col)
        return (d >= 0) & (d <= B_BAND * dil) & ((d & (dil - 1)) == 0)

    masks_cache = (
        cache_mask(B_WINDOWS[0], lambda col: col, 0),
        cache_mask(B_WINDOWS[1], lambda col: col, 1),
        cache_mask(DEC_SEQ * B_BAND, lambda col: (col % B_BAND) * B_DILATIONS[2] + col // B_BAND, 2),
    )
    masks_new = []
    for g in range(B_GROUPS):
        d = t_q - t_k
        valid = (d >= 0) & ((d & (B_DILATIONS[g] - 1)) == 0) & ((t_k < DEC_SEQ) | (t_k == t_q))
        masks_new.append(valid)

    for h in range(B_HEADS):
        hs = slice(h * B_HD, (h + 1) * B_HD)
        pieces = []
        for g in range(B_GROUPS):
            base = g * 3 * B_WIDTH + h * B_HD
            q = qkv_ref[0, :, base:base + B_HD].astype(BF16)
            kn = qkv_ref[0, :, base + B_WIDTH:base + B_WIDTH + B_HD].astype(BF16)
            vn = qkv_ref[0, :, base + 2 * B_WIDTH:base + 2 * B_WIDTH + B_HD].astype(BF16)
            if g == 0:
                kc = c0_ref[0, :, hs]
                vc = c0_ref[0, :, B_WIDTH + h * B_HD:B_WIDTH + (h + 1) * B_HD]
            elif g == 1:
                kc = c1_ref[0, :, hs]
                vc = c1_ref[0, :, B_WIDTH + h * B_HD:B_WIDTH + (h + 1) * B_HD]
            else:
                kc = jnp.concatenate(
                    [c2_ref[0, :, r * 2 * B_WIDTH + h * B_HD:r * 2 * B_WIDTH + (h + 1) * B_HD]
                     for r in range(DEC_SEQ)], axis=0)
                vc = jnp.concatenate(
                    [c2_ref[0, :, (2 * r + 1) * B_WIDTH + h * B_HD:(2 * r + 1) * B_WIDTH + (h + 1) * B_HD]
                     for r in range(DEC_SEQ)], axis=0)
            kc = kc.astype(BF16)
            vc = vc.astype(BF16)
            pieces.append((_dot_nt(q, kc) * B_SCALE, masks_cache[g], vc))
            pieces.append((_dot_nt(q, kn) * B_SCALE, masks_new[g], vn))
        m = None
        for s, mask, _ in pieces:
            mp = jnp.max(jnp.where(mask, s, NEG_BIG), axis=-1, keepdims=True)
            m = mp if m is None else jnp.maximum(m, mp)
        l = jnp.zeros((SAMPLE_PAD, 1), F32)
        acc = jnp.zeros((SAMPLE_PAD, B_HD), F32)
        for s, mask, vals in pieces:
            p = jnp.where(mask, jnp.exp(s - m), 0.0)
            l = l + jnp.sum(p, axis=-1, keepdims=True)
            acc = acc + _dot(p.astype(BF16), vals)
        o_ref[0, :, hs] = acc / l


def _attn_sample(qkv_s, c0, c1, c2):
    n_qkv = B_GROUPS * 3 * B_WIDTH
    kv_w = 2 * B_WIDTH
    dil2 = B_DILATIONS[2]
    c2v = c2.reshape(DEC_BATCH, B_WINDOWS[2] // dil2, dil2 * kv_w)
    return pl.pallas_call(
        _attn_sample_body,
        grid=(DEC_BATCH,),
        in_specs=[
            pl.BlockSpec((1, SAMPLE_PAD, n_qkv), lambda b: (b, 0, 0)),
            pl.BlockSpec((1, B_WINDOWS[0], kv_w), lambda b: (b, 0, 0)),
            pl.BlockSpec((1, B_WINDOWS[1], kv_w), lambda b: (b, 0, 0)),
            pl.BlockSpec((1, B_WINDOWS[2] // dil2, DEC_SEQ * kv_w), lambda b: (b, 0, 0)),
        ],
        out_specs=pl.BlockSpec((1, SAMPLE_PAD, B_WIDTH), lambda b: (b, 0, 0)),
        out_shape=jax.ShapeDtypeStruct((DEC_BATCH, SAMPLE_PAD, B_WIDTH), F32),
        compiler_params=_params(("parallel",)),
        name="attn_sample",
    )(qkv_s, c0, c1, c2v)


def _rmsnorm_body(x_ref, g_ref, o_ref):
    def chunk(rows):
        o_ref[rows, :] = _rms_rows(x_ref[rows, :], g_ref[...])
    _for_row_chunks(x_ref.shape[0], 256, chunk)


def _rmsnorm(x, g, *, tm):
    m = x.shape[0]
    return pl.pallas_call(
        _rmsnorm_body,
        grid=(m // tm,),
        in_specs=[pl.BlockSpec((tm, D_MODEL), lambda i: (i, 0)),
                  pl.BlockSpec((1, D_MODEL), lambda i: (0, 0))],
        out_specs=pl.BlockSpec((tm, D_MODEL), lambda i: (i, 0)),
        out_shape=jax.ShapeDtypeStruct((m, D_MODEL), F32),
        compiler_params=_params(("parallel",)),
        name="rmsnorm",
    )(x, g.reshape(1, D_MODEL))


POOL_ROWS = 256


def _pool_prompt_body(h_ref, x_ref, w_ref, scale_ref, o_ref, hpad_ref, z_ref):
    grp = pl.program_id(1)
    hpad_ref[0:POOL_PAD, :] = jnp.zeros((POOL_PAD, C_GW), F32)
    hpad_ref[POOL_PAD:, :] = h_ref[0]

    for gi, w in enumerate(POOL_WINDOWS):
        @pl.when(grp == gi)
        def _(w=w):
            for c in range(SEQ // POOL_ROWS):
                r0 = POOL_PAD + c * POOL_ROWS
                tot = hpad_ref[r0:r0 + POOL_ROWS, :]
                cur = tot
                for k in range(1, w):
                    tot = tot + hpad_ref[r0 - k:r0 - k + POOL_ROWS, :]
                pos = c * POOL_ROWS + lax.broadcasted_iota(jnp.int32, (POOL_ROWS, 1), 0)
                cnt = jnp.minimum(w, pos + 1).astype(F32)
                z_ref[c * POOL_ROWS:(c + 1) * POOL_ROWS, :] = (tot / cnt - cur).astype(BF16)

    y = _dot(z_ref[...], w_ref[0].astype(BF16))
    o_ref[0] = x_ref[0] + y * scale_ref[...]


def _pool_prompt(hp, x, w_c, c_scale):
    blk = pl.BlockSpec((1, SEQ, C_GW), lambda b, g: (b, 0, g))
    return pl.pallas_call(
        _pool_prompt_body,
        grid=(BATCH, C_GROUPS),
        in_specs=[blk, blk,
                  pl.BlockSpec((1, C_GW, C_GW), lambda b, g: (g, 0, 0)),
                  pl.BlockSpec((1, C_GW), lambda b, g: (0, g))],
        out_specs=blk,
        out_shape=jax.ShapeDtypeStruct((BATCH, SEQ, D_MODEL), F32),
        scratch_shapes=[pltpu.VMEM((POOL_PAD + SEQ, C_GW), F32),
                        pltpu.VMEM((SEQ, C_GW), BF16)],
        compiler_params=_params(("parallel", "parallel")),
        name="pool_prompt",
    )(hp, x, w_c, c_scale.reshape(1, D_MODEL))


def _pool_sample_body(x_ref, g_ref, state_ref, w_ref, scale_ref, o_ref, seq_ref, z_ref):
    for b in range(DEC_BATCH):
        rows = slice(b * SAMPLE_PAD, (b + 1) * SAMPLE_PAD)
        seq_ref[b, 0:POOL_PAD, :] = state_ref[b]
        seq_ref[b, POOL_PAD:, :] = _rms_rows(x_ref[rows, :], g_ref[...])
    for b in range(DEC_BATCH):
        rows = slice(b * SAMPLE_PAD, (b + 1) * SAMPLE_PAD)
        for gi, w in enumerate(POOL_WINDOWS):
            cs = slice(gi * C_GW, (gi + 1) * C_GW)
            cur = seq_ref[b, POOL_PAD:POOL_PAD + SAMPLE_PAD, cs]
            tot = cur
            for k in range(1, w):
                tot = tot + seq_ref[b, POOL_PAD - k:POOL_PAD - k + SAMPLE_PAD, cs]
            pos = PAST_LEN + lax.broadcasted_iota(jnp.int32, (SAMPLE_PAD, 1), 0)
            cnt = jnp.minimum(w, pos + 1).astype(F32)
            z_ref[rows, cs] = (tot / cnt - cur).astype(BF16)
    for gi in range(C_GROUPS):
        cs = slice(gi * C_GW, (gi + 1) * C_GW)
        y = _dot(z_ref[:, cs], w_ref[gi].astype(BF16))
        o_ref[:, cs] = x_ref[:, cs] + y * scale_ref[:, cs]


def _pool_sample(xs, g, state_pad, w_c, c_scale):
    seq_rows = POOL_PAD + SAMPLE_PAD
    return pl.pallas_call(
        _pool_sample_body,
        grid=(1,),
        in_specs=[
            pl.BlockSpec((MS, D_MODEL), lambda i: (0, 0)),
            pl.BlockSpec((1, D_MODEL), lambda i: (0, 0)),
            pl.BlockSpec((DEC_BATCH, POOL_PAD, D_MODEL), lambda i: (0, 0, 0)),
            pl.BlockSpec((C_GROUPS, C_GW, C_GW), lambda i: (0, 0, 0)),
            pl.BlockSpec((1, D_MODEL), lambda i: (0, 0)),
        ],
        out_specs=[pl.BlockSpec((MS, D_MODEL), lambda i: (0, 0)),
                   pl.BlockSpec((DEC_BATCH, seq_rows, D_MODEL), lambda i: (0, 0, 0))],
        out_shape=[jax.ShapeDtypeStruct((MS, D_MODEL), F32),
                   jax.ShapeDtypeStruct((DEC_BATCH, seq_rows, D_MODEL), F32)],
        scratch_shapes=[pltpu.VMEM((MS, D_MODEL), BF16)],
        compiler_params=_params(("arbitrary",)),
        name="pool_sample",
    )(xs, g.reshape(1, D_MODEL), state_pad, w_c, c_scale.reshape(1, D_MODEL))


TM_PROMPT = 1024
TM_FFN = 512
TM_AMIX = 512
TF_FFN = 256


def _mixer_a(xp, xs, norm_g, w_in, ln_g, ln_b, w_s, b_s, w_out):
    zp = _matmul(xp, w_in, tm=TM_PROMPT, tn=512, g=norm_g, epilogue="gelu", name="a_in")
    zs = _matmul(xs, w_in, tm=MS, tn=512, g=norm_g, epilogue="gelu", name="a_in_s")
    bias_p = jnp.repeat(jnp.transpose(b_s), A_HD, axis=1)
    yp, _ = _amix(xp, zp, ln_g, ln_b, w_s, bias_p, w_out,
                  tm=TM_AMIX, tn=256, t_len=A_CHUNK, block=A_CHUNK)
    ws_s = jnp.tile(w_s[:, :SAMPLE_PAD, :SAMPLE_PAD], (1, DEC_BATCH, DEC_BATCH))
    bias_s = jnp.tile(jnp.repeat(jnp.transpose(b_s[:, :SAMPLE_PAD]), A_HD, axis=1), (DEC_BATCH, 1))
    ys, v_s = _amix(xs, zs, ln_g, ln_b, ws_s, bias_s, w_out,
                    tm=MS, tn=512, t_len=MS, block=SAMPLE_PAD)
    return yp, ys, v_s


def _mixer_b(xp, xs, norm_g, caches, w_qkv, q_g, k_g, w_out):
    ones = jnp.ones((B_GROUPS, B_WIDTH), F32)
    gain = jnp.stack([jnp.tile(q_g, (1, B_HEADS)), jnp.tile(k_g, (1, B_HEADS)), ones], axis=1)
    gain = gain.reshape(1, B_GROUPS * 3 * B_WIDTH)
    qkv_p = _matmul(xp, w_qkv, tm=TM_PROMPT, tn=512, g=norm_g, epilogue="qknorm", gain=gain, name="qkv")
    qkv_s = _matmul(xs, w_qkv, tm=MS, tn=512, g=norm_g, epilogue="qknorm", gain=gain, name="qkv_s")
    n_qkv = B_GROUPS * 3 * B_WIDTH
    op = _attn_prompt(qkv_p.reshape(BATCH, SEQ, n_qkv))
    c = [cc.reshape(DEC_BATCH, cc.shape[1], 2 * B_WIDTH) for cc in caches]
    os_ = _attn_sample(qkv_s.reshape(DEC_BATCH, SAMPLE_PAD, n_qkv), c[0], c[1], c[2])
    yp = _matmul(op.reshape(MP, B_WIDTH), w_out, tm=TM_PROMPT, tn=512, epilogue="residual", res=xp, name="b_out")
    ys = _matmul(os_.reshape(MS, B_WIDTH), w_out, tm=MS, tn=512, epilogue="residual", res=xs, name="b_out_s")
    qp6 = qkv_p.reshape(BATCH, SEQ, B_GROUPS, 3, B_HEADS, B_HD)
    qs6 = qkv_s.reshape(DEC_BATCH, SAMPLE_PAD, B_GROUPS, 3, B_HEADS, B_HD)
    kv_p = [qp6[:, SEQ - min(B_WINDOWS[g], SEQ):, g, 1:3] for g in range(B_GROUPS)]
    kv_s = [qs6[:, :DEC_SEQ, g, 1:3] for g in range(B_GROUPS)]
    return yp, ys, kv_p, kv_s


def _mixer_c(xp, xs, norm_g, state, w_c, c_scale):
    hp = _rmsnorm(xp, norm_g, tm=TM_PROMPT)
    hp3 = hp.reshape(BATCH, SEQ, D_MODEL)
    yp = _pool_prompt(hp3, xp.reshape(BATCH, SEQ, D_MODEL), w_c, c_scale).reshape(MP, D_MODEL)
    state_pad = jnp.pad(state, ((0, 0), (POOL_PAD - POOL_STATE, 0), (0, 0)))
    ys, seq = _pool_sample(xs, norm_g, state_pad, w_c, c_scale)
    pool_p = hp3[:, SEQ - POOL_STATE:]
    first = POOL_PAD + DEC_SEQ - POOL_STATE
    pool_s = seq[:, first:first + POOL_STATE]
    return yp, ys, pool_p, pool_s


def kernel(x_prompt, x_sample, cache_b_kv0, cache_b_kv1, cache_b_kv2, state_c_pool, norm_mix_g, norm_ffn_g, a_w_in, a_ln_g, a_ln_b, a_w_s, a_b_s, a_w_out, b_w_qkv, b_q_g, b_k_g, b_w_out, c_w, c_scale, ffn_w1, ffn_w3, ffn_w2):
    xp = x_prompt.reshape(MP, D_MODEL)
    xs = jnp.pad(x_sample, ((0, 0), (0, SAMPLE_PAD - DEC_SEQ), (0, 0))).reshape(MS, D_MODEL)
    a_v_s, pool_p, pool_s = [], [], []
    kv_p = [[] for _ in range(B_GROUPS)]
    kv_s = [[] for _ in range(B_GROUPS)]
    ia = ib = ic = 0
    for layer in range(DEPTH):
        kind = layer % N_MIXERS
        g_mix = norm_mix_g[layer]
        if kind == 0:
            xp, xs, v_s = _mixer_a(xp, xs, g_mix, a_w_in[ia], a_ln_g[ia], a_ln_b[ia],
                                   a_w_s[ia], a_b_s[ia], a_w_out[ia])
            a_v_s.append(v_s.reshape(DEC_BATCH, SAMPLE_PAD, A_WIDTH)[:, :DEC_SEQ])
            ia += 1
        elif kind == 1:
            caches = (cache_b_kv0[ib], cache_b_kv1[ib], cache_b_kv2[ib])
            xp, xs, kvp, kvs = _mixer_b(xp, xs, g_mix, caches, b_w_qkv[ib], b_q_g[ib], b_k_g[ib], b_w_out[ib])
            for g in range(B_GROUPS):
                kv_p[g].append(kvp[g])
                kv_s[g].append(kvs[g])
            ib += 1
        else:
            xp, xs, pp, ps = _mixer_c(xp, xs, g_mix, state_c_pool[ic], c_w[ic], c_scale[ic])
            pool_p.append(pp)
            pool_s.append(ps)
            ic += 1
        xp = _ffn(xp, norm_ffn_g[layer], ffn_w1[layer], ffn_w3[layer], ffn_w2[layer], tm=TM_FFN, tf=TF_FFN)
        xs = _ffn(xs, norm_ffn_g[layer], ffn_w1[layer], ffn_w3[layer], ffn_w2[layer], tm=MS, tf=TF_FFN)
    y_prompt = xp.reshape(BATCH, SEQ, D_MODEL)
    y_sample = xs.reshape(DEC_BATCH, SAMPLE_PAD, D_MODEL)[:, :DEC_SEQ]
    return (y_prompt, y_sample, jnp.stack(a_v_s),
            jnp.stack(kv_p[0]), jnp.stack(kv_p[1]), jnp.stack(kv_p[2]),
            jnp.stack(kv_s[0]), jnp.stack(kv_s[1]), jnp.stack(kv_s[2]),
            jnp.stack(pool_p), jnp.stack(pool_s))
```

```python
import functools

import jax
import jax.numpy as jnp
from jax import lax
from jax.experimental import pallas as pl
from jax.experimental.pallas import tpu as pltpu

F32 = jnp.float32
BF16 = jnp.bfloat16

D_MODEL = 2048
BATCH = 4
SEQ = 2048
DEPTH = 4
DEC_BATCH = 8
DEC_SEQ = 4
PAST_LEN = 16384
N_MIXERS = 3
A_CHUNK = 128
A_WIDTH = D_MODEL
A_HEADS = 16
A_HD = A_WIDTH // A_HEADS
B_WINDOWS = (128, 512, 2048)
B_DILATIONS = (1, 4, 16)
B_GROUPS = 3
B_HD = 128
B_HEADS = D_MODEL // B_HD
B_WIDTH = B_HEADS * B_HD
B_SCALE = B_HD ** -0.5
B_BAND = 128
POOL_WINDOWS = (2, 4, 8, 16)
C_GROUPS = 4
C_GW = D_MODEL // C_GROUPS
POOL_STATE = max(POOL_WINDOWS) - 1
POOL_PAD = POOL_STATE + 1
D_FF = ((8 * D_MODEL + 3 * 256 - 1) // (3 * 256)) * 256
EPS = 1e-6

SUBLANES = 8
LANES = 128
VMEM_LIMIT_BYTES = 56 * 1024 * 1024

SAMPLE_PAD = SUBLANES
MP = BATCH * SEQ
MS = DEC_BATCH * SAMPLE_PAD
NEG_BIG = -1e30

TM = 1024
TN = 512
TF_FFN = 256
TN_AMIX = 256
NORM_ROWS = 256


def _params(semantics):
    return pltpu.CompilerParams(dimension_semantics=semantics,
                                vmem_limit_bytes=VMEM_LIMIT_BYTES)


def _dot(a, b):
    return jnp.dot(a, b, preferred_element_type=F32)


def _dot_nt(a, b):
    return lax.dot_general(a, b, (((1,), (1,)), ((), ())), preferred_element_type=F32)


def _rms_rows(x, g):
    r = lax.rsqrt(jnp.mean(x * x, axis=-1, keepdims=True) + EPS)
    return (x * r) * g


def _for_row_chunks(rows, chunk, fn):
    chunk = min(chunk, rows)
    n = rows // chunk
    if n == 1:
        fn(pl.ds(0, chunk))
        return

    def body(c, carry):
        fn(pl.ds(pl.multiple_of(c * chunk, chunk), chunk))
        return carry

    lax.fori_loop(0, n, body, 0)


def _layer_vec(stacked):
    return stacked.reshape(stacked.shape[0], 1, stacked.shape[1])


def _hold_after_first_tile(col_fn, last):
    return lambda i, j: (0, jnp.where(i == 0, col_fn(j), last))


def _ffn_body(x_ref, xs_ref, g_ref, w1_ref, w3_ref, w2_ref, o_ref, os_ref, hn_ref):
    i = pl.program_id(0)
    j = pl.program_id(1)
    tm = x_ref.shape[0]

    @pl.when(j == 0)
    def _():
        def chunk(rows):
            x = x_ref[rows, :]
            hn_ref[rows, :] = _rms_rows(x, g_ref[...]).astype(BF16)
            o_ref[rows, :] = x
        _for_row_chunks(tm, NORM_ROWS, chunk)

    @pl.when((i == 0) & (j == 0))
    def _():
        xs = xs_ref[...]
        hn_ref[tm:, :] = _rms_rows(xs, g_ref[...]).astype(BF16)
        os_ref[...] = xs

    def gate(h):
        a = _dot(h, w1_ref[...].astype(BF16))
        b = _dot(h, w3_ref[...].astype(BF16))
        return (jax.nn.silu(a) * b).astype(BF16)

    @pl.when(i == 0)
    def _():
        gt = gate(hn_ref[...])
        w2 = w2_ref[...].astype(BF16)
        o_ref[...] += _dot(gt[:tm], w2)
        os_ref[...] += _dot(gt[tm:], w2)

    @pl.when(i != 0)
    def _():
        o_ref[...] += _dot(gate(hn_ref[0:tm, :]), w2_ref[...].astype(BF16))


def _ffn(x, xs, g_all, w1_all, w3_all, w2_all, layer):
    return pl.pallas_call(
        _ffn_body,
        grid=(MP // TM, D_FF // TF_FFN),
        in_specs=[
            pl.BlockSpec((TM, D_MODEL), lambda i, j: (i, 0)),
            pl.BlockSpec((MS, D_MODEL), lambda i, j: (0, 0)),
            pl.BlockSpec((None, 1, D_MODEL), lambda i, j: (layer, 0, 0)),
            pl.BlockSpec((None, D_MODEL, TF_FFN), lambda i, j: (layer, 0, j)),
            pl.BlockSpec((None, D_MODEL, TF_FFN), lambda i, j: (layer, 0, j)),
            pl.BlockSpec((None, TF_FFN, D_MODEL), lambda i, j: (layer, j, 0)),
        ],
        out_specs=[pl.BlockSpec((TM, D_MODEL), lambda i, j: (i, 0)),
                   pl.BlockSpec((MS, D_MODEL), lambda i, j: (0, 0))],
        out_shape=[jax.ShapeDtypeStruct((MP, D_MODEL), F32),
                   jax.ShapeDtypeStruct((MS, D_MODEL), F32)],
        scratch_shapes=[pltpu.VMEM((TM + MS, D_MODEL), BF16)],
        compiler_params=_params(("arbitrary", "arbitrary")),
        name="ffn",
    )(x, xs, _layer_vec(g_all), w1_all, w3_all, w2_all)


def _fill_lhs(h_ref, x_ref, xs_ref, g_ref, i, j):
    tm = x_ref.shape[0]

    def prep(x):
        if g_ref is not None:
            x = _rms_rows(x, g_ref[...])
        return x.astype(BF16)

    @pl.when(j == 0)
    def _():
        def chunk(rows):
            h_ref[rows, :] = prep(x_ref[rows, :])
        _for_row_chunks(tm, NORM_ROWS, chunk)

    @pl.when((i == 0) & (j == 0))
    def _():
        h_ref[tm:, :] = prep(xs_ref[...])


def _mm_body(*refs, norm, epilogue):
    it = iter(refs)
    x_ref = next(it)
    xs_ref = next(it)
    g_ref = next(it) if norm else None
    w_ref = next(it)
    res_ref = next(it) if epilogue == "residual" else None
    res_s_ref = next(it) if epilogue == "residual" else None
    o_ref = next(it)
    os_ref = next(it)
    h_ref = next(it)
    i = pl.program_id(0)
    j = pl.program_id(1)
    tm = x_ref.shape[0]
    _fill_lhs(h_ref, x_ref, xs_ref, g_ref, i, j)

    def emit(acc, out_ref, r_ref):
        if epilogue == "gelu":
            out_ref[...] = jax.nn.gelu(acc, approximate=True)
        elif epilogue == "residual":
            out_ref[...] = r_ref[...] + acc
        else:
            out_ref[...] = acc

    @pl.when(i == 0)
    def _():
        acc = _dot(h_ref[...], w_ref[...].astype(BF16))
        emit(acc[:tm], o_ref, res_ref)
        emit(acc[tm:], os_ref, res_s_ref)

    @pl.when(i != 0)
    def _():
        emit(_dot(h_ref[0:tm, :], w_ref[...].astype(BF16)), o_ref, res_ref)


def _matmul(x, xs, w_all, layer, *, g_all=None, g_layer=None, epilogue=None, res=None, res_s=None,
            name="matmul"):
    k = x.shape[1]
    n = w_all.shape[2]
    nj = n // TN
    norm = g_all is not None
    args = [x, xs]
    in_specs = [pl.BlockSpec((TM, k), lambda i, j: (i, 0)),
                pl.BlockSpec((MS, k), lambda i, j: (0, 0))]
    if norm:
        args.append(_layer_vec(g_all))
        in_specs.append(pl.BlockSpec((None, 1, k), lambda i, j: (g_layer, 0, 0)))
    args.append(w_all)
    in_specs.append(pl.BlockSpec((None, k, TN), lambda i, j: (layer, 0, j)))
    if epilogue == "residual":
        args += [res, res_s]
        in_specs += [pl.BlockSpec((TM, TN), lambda i, j: (i, j)),
                     pl.BlockSpec((MS, TN), lambda i, j: (0, j))]
    body = functools.partial(_mm_body, norm=norm, epilogue=epilogue)
    return pl.pallas_call(
        body,
        grid=(MP // TM, nj),
        in_specs=in_specs,
        out_specs=[pl.BlockSpec((TM, TN), lambda i, j: (i, j)),
                   pl.BlockSpec((MS, TN), _hold_after_first_tile(lambda j: j, nj - 1))],
        out_shape=[jax.ShapeDtypeStruct((MP, n), F32),
                   jax.ShapeDtypeStruct((MS, n), F32)],
        scratch_shapes=[pltpu.VMEM((TM + MS, k), BF16)],
        compiler_params=_params(("arbitrary", "arbitrary")),
        name=name,
    )(*args)


QKV_TILES = B_WIDTH // TN
QKV_COLS = B_GROUPS * 3 * B_WIDTH


def _q_col(j):
    return (j // (3 * QKV_TILES)) * QKV_TILES + jnp.minimum(j % (3 * QKV_TILES), QKV_TILES - 1)


def _kv_col(g):
    return lambda j: jnp.clip(j - (3 * g + 1) * QKV_TILES, 0, 2 * QKV_TILES - 1)


def _qkv_body(x_ref, xs_ref, g_ref, w_ref, gain_ref,
              q_ref, kv0_ref, kv1_ref, kv2_ref, qs_ref, kvs0_ref, kvs1_ref, kvs2_ref,
              h_ref, val_ref):
    i = pl.program_id(0)
    j = pl.program_id(1)
    tm = x_ref.shape[0]
    _fill_lhs(h_ref, x_ref, xs_ref, g_ref, i, j)
    kind = (j // QKV_TILES) % 3
    grp = j // (3 * QKV_TILES)

    def finish(acc, n_rows):
        rows = pl.ds(0, n_rows)

        @pl.when(kind < 2)
        def _():
            for hh in range(TN // B_HD):
                cs = slice(hh * B_HD, (hh + 1) * B_HD)
                val_ref[rows, cs] = _rms_rows(acc[:, cs], gain_ref[:, cs])

        @pl.when(kind == 2)
        def _():
            val_ref[rows, :] = acc

    def scatter(rows, q_out, kv_outs):
        @pl.when(kind == 0)
        def _():
            q_out[...] = val_ref[rows, :]

        for g, kv_out in enumerate(kv_outs):
            @pl.when((kind != 0) & (grp == g))
            def _(kv_out=kv_out):
                kv_out[...] = val_ref[rows, :]

    @pl.when(i == 0)
    def _():
        finish(_dot(h_ref[...], w_ref[...].astype(BF16)), tm + MS)
        scatter(pl.ds(tm, MS), qs_ref, (kvs0_ref, kvs1_ref, kvs2_ref))

    @pl.when(i != 0)
    def _():
        finish(_dot(h_ref[0:tm, :], w_ref[...].astype(BF16)), tm)

    scatter(pl.ds(0, tm), q_ref, (kv0_ref, kv1_ref, kv2_ref))


def _qkv(x, xs, g_all, g_layer, w_all, layer, gain):
    nj = QKV_COLS // TN
    kv_w = 2 * B_WIDTH
    q_last = B_GROUPS * QKV_TILES - 1
    kv_last = 2 * QKV_TILES - 1
    out_specs = [pl.BlockSpec((TM, TN), lambda i, j: (i, _q_col(j)))]
    out_specs += [pl.BlockSpec((TM, TN), lambda i, j, g=g: (i, _kv_col(g)(j))) for g in range(B_GROUPS)]
    out_specs += [pl.BlockSpec((MS, TN), _hold_after_first_tile(_q_col, q_last))]
    out_specs += [pl.BlockSpec((MS, TN), _hold_after_first_tile(_kv_col(g), kv_last)) for g in range(B_GROUPS)]
    out_shape = [jax.ShapeDtypeStruct((MP, B_GROUPS * B_WIDTH), F32)]
    out_shape += [jax.ShapeDtypeStruct((MP, kv_w), F32)] * B_GROUPS
    out_shape += [jax.ShapeDtypeStruct((MS, B_GROUPS * B_WIDTH), F32)]
    out_shape += [jax.ShapeDtypeStruct((MS, kv_w), F32)] * B_GROUPS
    return pl.pallas_call(
        _qkv_body,
        grid=(MP // TM, nj),
        in_specs=[
            pl.BlockSpec((TM, D_MODEL), lambda i, j: (i, 0)),
            pl.BlockSpec((MS, D_MODEL), lambda i, j: (0, 0)),
            pl.BlockSpec((None, 1, D_MODEL), lambda i, j: (g_layer, 0, 0)),
            pl.BlockSpec((None, D_MODEL, TN), lambda i, j: (layer, 0, j)),
            pl.BlockSpec((1, TN), lambda i, j: (0, j)),
        ],
        out_specs=out_specs,
        out_shape=out_shape,
        scratch_shapes=[pltpu.VMEM((TM + MS, D_MODEL), BF16),
                        pltpu.VMEM((TM + MS, TN), F32)],
        compiler_params=_params(("arbitrary", "arbitrary")),
        name="qkv",
    )(x, xs, _layer_vec(g_all), w_all, gain)


def _amix_body(*refs, t_len, block, emit_v):
    (x_ref, u_ref, v_ref, lng_ref, lnb_ref, ws_ref, bias_ref, wout_ref, o_ref) = refs[:9]
    vout_ref = refs[9] if emit_v else None
    gated_ref, wsm_ref = refs[-2:]
    j = pl.program_id(1)

    @pl.when(j == 0)
    def _():
        r = lax.broadcasted_iota(jnp.int32, (t_len, t_len), 0)
        c = lax.broadcasted_iota(jnp.int32, (t_len, t_len), 1)
        keep = c <= r
        if block < t_len:
            keep = keep & ((r // block) == (c // block)) & ((c % block) < DEC_SEQ)
        for h in range(A_HEADS):
            wsm_ref[h] = jnp.where(keep, ws_ref[h], 0.0).astype(BF16)

        def chunk(rows):
            v = v_ref[rows, :]
            mu = jnp.mean(v, axis=-1, keepdims=True)
            d = v - mu
            var = jnp.mean(d * d, axis=-1, keepdims=True)
            vln = (d * lax.rsqrt(var + EPS)) * lng_ref[...] + lnb_ref[...]
            if emit_v:
                vout_ref[rows, :] = vln
            vb = vln.astype(BF16)
            for h in range(A_HEADS):
                cs = slice(h * A_HD, (h + 1) * A_HD)
                mixed = _dot(wsm_ref[h], vb[:, cs]) + bias_ref[:, cs]
                gated_ref[rows, cs] = (u_ref[rows, cs] * mixed).astype(BF16)
        _for_row_chunks(v_ref.shape[0], t_len, chunk)

    o_ref[...] = x_ref[...] + _dot(gated_ref[...], wout_ref[...].astype(BF16))


def _amix(x, z, lng_all, lnb_all, ws_all, ws_idx, bias, wout_all, layer, *, tm, tn, t_len, block, emit_v):
    m = x.shape[0]
    body = functools.partial(_amix_body, t_len=t_len, block=block, emit_v=emit_v)
    out_specs = [pl.BlockSpec((tm, tn), lambda i, j: (i, j))]
    out_shape = [jax.ShapeDtypeStruct((m, D_MODEL), F32)]
    if emit_v:
        out_specs.append(pl.BlockSpec((tm, A_WIDTH), lambda i, j: (i, 0)))
        out_shape.append(jax.ShapeDtypeStruct((m, A_WIDTH), F32))
    return pl.pallas_call(
        body,
        grid=(m // tm, D_MODEL // tn),
        in_specs=[
            pl.BlockSpec((tm, tn), lambda i, j: (i, j)),
            pl.BlockSpec((tm, A_WIDTH), lambda i, j: (i, 0)),
            pl.BlockSpec((tm, A_WIDTH), lambda i, j: (i, 1)),
            pl.BlockSpec((None, 1, A_WIDTH), lambda i, j: (layer, 0, 0)),
            pl.BlockSpec((None, 1, A_WIDTH), lambda i, j: (layer, 0, 0)),
            pl.BlockSpec((None, A_HEADS, t_len, t_len), lambda i, j: (ws_idx, 0, 0, 0)),
            pl.BlockSpec((t_len, A_WIDTH), lambda i, j: (0, 0)),
            pl.BlockSpec((None, A_WIDTH, tn), lambda i, j: (layer, 0, j)),
        ],
        out_specs=out_specs,
        out_shape=out_shape,
        scratch_shapes=[pltpu.VMEM((tm, A_WIDTH), BF16),
                        pltpu.VMEM((A_HEADS, t_len, t_len), BF16)],
        compiler_params=_params(("parallel", "arbitrary")),
        name="amix",
    )(x, z, z, _layer_vec(lng_all), _layer_vec(lnb_all), ws_all, bias, wout_all)


def _attn_prompt_body(q0, k0, v0, q1, k1, v1, q2, k2, v2, o_ref, acc_ref, m_ref, l_ref):
    groups = ((q0, k0, v0), (q1, k1, v1), (q2, k2, v2))
    n = B_BAND
    qi = lax.broadcasted_iota(jnp.int32, (n, 2 * n), 0)
    kj = lax.broadcasted_iota(jnp.int32, (n, 2 * n), 1)
    dist = n + qi - kj
    band_mask = (dist >= 0) & (dist <= n)
    qi1 = lax.broadcasted_iota(jnp.int32, (n, n), 0)
    kj1 = lax.broadcasted_iota(jnp.int32, (n, n), 1)
    causal_mask = kj1 <= qi1

    for g, (q_ref, k_ref, v_ref) in enumerate(groups):
        dil = B_DILATIONS[g]
        n_blocks = SEQ // dil // n
        for r in range(dil):
            for c in range(n_blocks):
                start = r + c * n * dil
                if dil == 1:
                    rows_q = pl.ds(start, n)
                else:
                    rows_q = pl.ds(start, n, stride=dil)
                if c == 0:
                    rows_k, mask = rows_q, causal_mask
                elif dil == 1:
                    rows_k, mask = pl.ds(start - n, 2 * n), band_mask
                else:
                    rows_k, mask = pl.ds(start - n * dil, 2 * n, stride=dil), band_mask
                q = q_ref[0, rows_q, :].astype(BF16)
                k = k_ref[0, rows_k, :].astype(BF16)
                v = v_ref[0, rows_k, :].astype(BF16)
                s = _dot_nt(q, k) * B_SCALE
                s = jnp.where(mask, s, NEG_BIG)
                m = jnp.max(s, axis=-1, keepdims=True)
                p = jnp.exp(s - m)
                l = jnp.sum(p, axis=-1, keepdims=True)
                acc = _dot(p.astype(BF16), v)
                if g == 0:
                    acc_ref[rows_q, :] = acc
                    m_ref[rows_q, :] = jnp.broadcast_to(m, (n, LANES))
                    l_ref[rows_q, :] = jnp.broadcast_to(l, (n, LANES))
                else:
                    m_old = m_ref[rows_q, :]
                    m_new = jnp.maximum(m_old, m)
                    a_old = jnp.exp(m_old - m_new)
                    a_new = jnp.exp(m - m_new)
                    acc_new = acc_ref[rows_q, :] * a_old + acc * a_new
                    l_new = l_ref[rows_q, :] * a_old + l * a_new
                    if g == B_GROUPS - 1:
                        o_ref[0, rows_q, :] = acc_new / l_new
                    else:
                        acc_ref[rows_q, :] = acc_new
                        l_ref[rows_q, :] = l_new
                        m_ref[rows_q, :] = m_new


def _attn_prompt(q_all, kv):
    in_specs, args = [], []
    for g in range(B_GROUPS):
        in_specs += [pl.BlockSpec((1, SEQ, B_HD), lambda b, h, g=g: (b, 0, g * B_HEADS + h)),
                     pl.BlockSpec((1, SEQ, B_HD), lambda b, h: (b, 0, h)),
                     pl.BlockSpec((1, SEQ, B_HD), lambda b, h: (b, 0, B_HEADS + h))]
        args += [q_all, kv[g], kv[g]]
    return pl.pallas_call(
        _attn_prompt_body,
        grid=(BATCH, B_HEADS),
        in_specs=in_specs,
        out_specs=pl.BlockSpec((1, SEQ, B_HD), lambda b, h: (b, 0, h)),
        out_shape=jax.ShapeDtypeStruct((BATCH, SEQ, B_WIDTH), F32),
        scratch_shapes=[pltpu.VMEM((SEQ, B_HD), F32),
                        pltpu.VMEM((SEQ, LANES), F32),
                        pltpu.VMEM((SEQ, LANES), F32)],
        compiler_params=_params(("parallel", "parallel")),
        name="attn_prompt",
    )(*args)


def _attn_sample_body(q_ref, n0_ref, n1_ref, n2_ref, c0_ref, c1_ref, c2_ref, o_ref):
    new_refs = (n0_ref, n1_ref, n2_ref)
    t_q = lax.broadcasted_iota(jnp.int32, (SAMPLE_PAD, SAMPLE_PAD), 0)
    t_k = lax.broadcasted_iota(jnp.int32, (SAMPLE_PAD, SAMPLE_PAD), 1)

    def cache_mask(n_keys, key_index_fn, g):
        dil = B_DILATIONS[g]
        r_len = B_WINDOWS[g]
        tq = lax.broadcasted_iota(jnp.int32, (SAMPLE_PAD, n_keys), 0)
        col = lax.broadcasted_iota(jnp.int32, (SAMPLE_PAD, n_keys), 1)
        d = r_len + tq - key_index_fn(col)
        return (d >= 0) & (d <= B_BAND * dil) & ((d & (dil - 1)) == 0)

    masks_cache = (
        cache_mask(B_WINDOWS[0], lambda col: col, 0),
        cache_mask(B_WINDOWS[1], lambda col: col, 1),
        cache_mask(DEC_SEQ * B_BAND, lambda col: (col % B_BAND) * B_DILATIONS[2] + col // B_BAND, 2),
    )
    masks_new = []
    for g in range(B_GROUPS):
        d = t_q - t_k
        valid = (d >= 0) & ((d & (B_DILATIONS[g] - 1)) == 0) & ((t_k < DEC_SEQ) | (t_k == t_q))
        masks_new.append(valid)

    for h in range(B_HEADS):
        hs = slice(h * B_HD, (h + 1) * B_HD)
        vs = slice(B_WIDTH + h * B_HD, B_WIDTH + (h + 1) * B_HD)
        pieces = []
        for g in range(B_GROUPS):
            q = q_ref[0, :, g * B_WIDTH + h * B_HD:g * B_WIDTH + (h + 1) * B_HD].astype(BF16)
            kn = new_refs[g][0, :, hs].astype(BF16)
            vn = new_refs[g][0, :, vs].astype(BF16)
            if g == 0:
                kc, vc = c0_ref[0, :, hs], c0_ref[0, :, vs]
            elif g == 1:
                kc, vc = c1_ref[0, :, hs], c1_ref[0, :, vs]
            else:
                kc = jnp.concatenate(
                    [c2_ref[0, :, r * 2 * B_WIDTH + h * B_HD:r * 2 * B_WIDTH + (h + 1) * B_HD]
                     for r in range(DEC_SEQ)], axis=0)
                vc = jnp.concatenate(
                    [c2_ref[0, :, (2 * r + 1) * B_WIDTH + h * B_HD:(2 * r + 1) * B_WIDTH + (h + 1) * B_HD]
                     for r in range(DEC_SEQ)], axis=0)
            kc = kc.astype(BF16)
            vc = vc.astype(BF16)
            pieces.append((_dot_nt(q, kc) * B_SCALE, masks_cache[g], vc))
            pieces.append((_dot_nt(q, kn) * B_SCALE, masks_new[g], vn))
        m = None
        for s, mask, _ in pieces:
            mp = jnp.max(jnp.where(mask, s, NEG_BIG), axis=-1, keepdims=True)
            m = mp if m is None else jnp.maximum(m, mp)
        l = jnp.zeros((SAMPLE_PAD, 1), F32)
        acc = jnp.zeros((SAMPLE_PAD, B_HD), F32)
        for s, mask, vals in pieces:
            p = jnp.where(mask, jnp.exp(s - m), 0.0)
            l = l + jnp.sum(p, axis=-1, keepdims=True)
            acc = acc + _dot(p.astype(BF16), vals)
        o_ref[0, :, hs] = acc / l


def _attn_sample(q_s, kv_s, caches):
    kv_w = 2 * B_WIDTH
    dil2 = B_DILATIONS[2]
    c2v = caches[2].reshape(DEC_BATCH, B_WINDOWS[2] // dil2, dil2 * kv_w)
    new_spec = pl.BlockSpec((1, SAMPLE_PAD, kv_w), lambda b: (b, 0, 0))
    return pl.pallas_call(
        _attn_sample_body,
        grid=(DEC_BATCH,),
        in_specs=[
            pl.BlockSpec((1, SAMPLE_PAD, B_GROUPS * B_WIDTH), lambda b: (b, 0, 0)),
            new_spec, new_spec, new_spec,
            pl.BlockSpec((1, B_WINDOWS[0], kv_w), lambda b: (b, 0, 0)),
            pl.BlockSpec((1, B_WINDOWS[1], kv_w), lambda b: (b, 0, 0)),
            pl.BlockSpec((1, B_WINDOWS[2] // dil2, DEC_SEQ * kv_w), lambda b: (b, 0, 0)),
        ],
        out_specs=pl.BlockSpec((1, SAMPLE_PAD, B_WIDTH), lambda b: (b, 0, 0)),
        out_shape=jax.ShapeDtypeStruct((DEC_BATCH, SAMPLE_PAD, B_WIDTH), F32),
        compiler_params=_params(("parallel",)),
        name="attn_sample",
    )(q_s, kv_s[0], kv_s[1], kv_s[2], caches[0], caches[1], c2v)


def _rmsnorm_body(x_ref, g_ref, o_ref):
    def chunk(rows):
        o_ref[rows, :] = _rms_rows(x_ref[rows, :], g_ref[...])
    _for_row_chunks(x_ref.shape[0], NORM_ROWS, chunk)


def _rmsnorm(x, g_all, layer):
    m = x.shape[0]
    return pl.pallas_call(
        _rmsnorm_body,
        grid=(m // TM,),
        in_specs=[pl.BlockSpec((TM, D_MODEL), lambda i: (i, 0)),
                  pl.BlockSpec((None, 1, D_MODEL), lambda i: (layer, 0, 0))],
        out_specs=pl.BlockSpec((TM, D_MODEL), lambda i: (i, 0)),
        out_shape=jax.ShapeDtypeStruct((m, D_MODEL), F32),
        compiler_params=_params(("parallel",)),
        name="rmsnorm",
    )(x, _layer_vec(g_all))


POOL_ROWS = 256


def _pool_prompt_body(h_ref, x_ref, w_ref, scale_ref, o_ref, hpad_ref, z_ref):
    grp = pl.program_id(1)
    hpad_ref[0:POOL_PAD, :] = jnp.zeros((POOL_PAD, C_GW), F32)
    hpad_ref[POOL_PAD:, :] = h_ref[0]

    for gi, w in enumerate(POOL_WINDOWS):
        @pl.when(grp == gi)
        def _(w=w):
            for c in range(SEQ // POOL_ROWS):
                r0 = POOL_PAD + c * POOL_ROWS
                tot = hpad_ref[r0:r0 + POOL_ROWS, :]
                cur = tot
                for k in range(1, w):
                    tot = tot + hpad_ref[r0 - k:r0 - k + POOL_ROWS, :]
                pos = c * POOL_ROWS + lax.broadcasted_iota(jnp.int32, (POOL_ROWS, 1), 0)
                cnt = jnp.minimum(w, pos + 1).astype(F32)
                z_ref[c * POOL_ROWS:(c + 1) * POOL_ROWS, :] = (tot / cnt - cur).astype(BF16)

    y = _dot(z_ref[...], w_ref[...].astype(BF16))
    o_ref[0] = x_ref[0] + y * scale_ref[...]


def _pool_prompt(hp, x, w_all, scale_all, layer):
    blk = pl.BlockSpec((1, SEQ, C_GW), lambda b, g: (b, 0, g))
    return pl.pallas_call(
        _pool_prompt_body,
        grid=(BATCH, C_GROUPS),
        in_specs=[blk, blk,
                  pl.BlockSpec((None, None, C_GW, C_GW), lambda b, g: (layer, g, 0, 0)),
                  pl.BlockSpec((None, 1, C_GW), lambda b, g: (layer, 0, g))],
        out_specs=blk,
        out_shape=jax.ShapeDtypeStruct((BATCH, SEQ, D_MODEL), F32),
        scratch_shapes=[pltpu.VMEM((POOL_PAD + SEQ, C_GW), F32),
                        pltpu.VMEM((SEQ, C_GW), BF16)],
        compiler_params=_params(("parallel", "parallel")),
        name="pool_prompt",
    )(hp, x, w_all, _layer_vec(scale_all))


def _pool_sample_body(x_ref, g_ref, state_ref, w_ref, scale_ref, o_ref, seq_ref, z_ref):
    for b in range(DEC_BATCH):
        rows = slice(b * SAMPLE_PAD, (b + 1) * SAMPLE_PAD)
        seq_ref[b, 0:POOL_PAD, :] = state_ref[b]
        seq_ref[b, POOL_PAD:, :] = _rms_rows(x_ref[rows, :], g_ref[...])
    for b in range(DEC_BATCH):
        rows = slice(b * SAMPLE_PAD, (b + 1) * SAMPLE_PAD)
        for gi, w in enumerate(POOL_WINDOWS):
            cs = slice(gi * C_GW, (gi + 1) * C_GW)
            cur = seq_ref[b, POOL_PAD:POOL_PAD + SAMPLE_PAD, cs]
            tot = cur
            for k in range(1, w):
                tot = tot + seq_ref[b, POOL_PAD - k:POOL_PAD - k + SAMPLE_PAD, cs]
            pos = PAST_LEN + lax.broadcasted_iota(jnp.int32, (SAMPLE_PAD, 1), 0)
            cnt = jnp.minimum(w, pos + 1).astype(F32)
            z_ref[rows, cs] = (tot / cnt - cur).astype(BF16)
    for gi in range(C_GROUPS):
        cs = slice(gi * C_GW, (gi + 1) * C_GW)
        y = _dot(z_ref[:, cs], w_ref[gi].astype(BF16))
        o_ref[:, cs] = x_ref[:, cs] + y * scale_ref[:, cs]


def _pool_sample(xs, g_all, g_layer, state_pad, w_all, scale_all, layer):
    seq_rows = POOL_PAD + SAMPLE_PAD
    return pl.pallas_call(
        _pool_sample_body,
        grid=(1,),
        in_specs=[
            pl.BlockSpec((MS, D_MODEL), lambda i: (0, 0)),
            pl.BlockSpec((None, 1, D_MODEL), lambda i: (g_layer, 0, 0)),
            pl.BlockSpec((DEC_BATCH, POOL_PAD, D_MODEL), lambda i: (0, 0, 0)),
            pl.BlockSpec((None, C_GROUPS, C_GW, C_GW), lambda i: (layer, 0, 0, 0)),
            pl.BlockSpec((None, 1, D_MODEL), lambda i: (layer, 0, 0)),
        ],
        out_specs=[pl.BlockSpec((MS, D_MODEL), lambda i: (0, 0)),
                   pl.BlockSpec((DEC_BATCH, seq_rows, D_MODEL), lambda i: (0, 0, 0))],
        out_shape=[jax.ShapeDtypeStruct((MS, D_MODEL), F32),
                   jax.ShapeDtypeStruct((DEC_BATCH, seq_rows, D_MODEL), F32)],
        scratch_shapes=[pltpu.VMEM((MS, D_MODEL), BF16)],
        compiler_params=_params(("arbitrary",)),
        name="pool_sample",
    )(xs, _layer_vec(g_all), state_pad, w_all, _layer_vec(scale_all))


def _mixer_a(xp, xs, norm_g, layer, ia, a_w_in, a_ln_g, a_ln_b, a_w_s, a_b_s, a_w_out):
    zp, zs = _matmul(xp, xs, a_w_in, ia, g_all=norm_g, g_layer=layer, epilogue="gelu", name="a_in")
    b_s = a_b_s[ia]
    bias_p = jnp.repeat(jnp.transpose(b_s), A_HD, axis=1)
    (yp,) = _amix(xp, zp, a_ln_g, a_ln_b, a_w_s, ia, bias_p, a_w_out, ia,
                  tm=TM, tn=TN_AMIX, t_len=A_CHUNK, block=A_CHUNK, emit_v=False)
    ws_s = jnp.tile(a_w_s[ia][:, :SAMPLE_PAD, :SAMPLE_PAD], (1, DEC_BATCH, DEC_BATCH))[None]
    bias_s = jnp.tile(jnp.repeat(jnp.transpose(b_s[:, :SAMPLE_PAD]), A_HD, axis=1), (DEC_BATCH, 1))
    ys, v_s = _amix(xs, zs, a_ln_g, a_ln_b, ws_s, 0, bias_s, a_w_out, ia,
                    tm=MS, tn=TN, t_len=MS, block=SAMPLE_PAD, emit_v=True)
    return yp, ys, v_s


def _mixer_b(xp, xs, norm_g, layer, ib, caches, b_w_qkv, b_q_g, b_k_g, b_w_out):
    ones = jnp.ones((B_GROUPS, B_WIDTH), F32)
    gain = jnp.stack([jnp.tile(b_q_g[ib], (1, B_HEADS)), jnp.tile(b_k_g[ib], (1, B_HEADS)), ones], axis=1)
    gain = gain.reshape(1, QKV_COLS)
    q_p, kv0, kv1, kv2, q_s, kvs0, kvs1, kvs2 = _qkv(xp, xs, norm_g, layer, b_w_qkv, ib, gain)
    kv_w = 2 * B_WIDTH
    kv_p = [a.reshape(BATCH, SEQ, kv_w) for a in (kv0, kv1, kv2)]
    kv_s = [a.reshape(DEC_BATCH, SAMPLE_PAD, kv_w) for a in (kvs0, kvs1, kvs2)]
    op = _attn_prompt(q_p.reshape(BATCH, SEQ, B_GROUPS * B_WIDTH), kv_p)
    c = [cc[ib].reshape(DEC_BATCH, cc.shape[2], kv_w) for cc in caches]
    os_ = _attn_sample(q_s.reshape(DEC_BATCH, SAMPLE_PAD, B_GROUPS * B_WIDTH), kv_s, c)
    yp, ys = _matmul(op.reshape(MP, B_WIDTH), os_.reshape(MS, B_WIDTH), b_w_out, ib,
                     epilogue="residual", res=xp, res_s=xs, name="b_out")
    new_p = [kv_p[g].reshape(BATCH, SEQ, 2, B_HEADS, B_HD)[:, SEQ - min(B_WINDOWS[g], SEQ):]
             for g in range(B_GROUPS)]
    new_s = [kv_s[g].reshape(DEC_BATCH, SAMPLE_PAD, 2, B_HEADS, B_HD)[:, :DEC_SEQ]
             for g in range(B_GROUPS)]
    return yp, ys, new_p, new_s


def _mixer_c(xp, xs, norm_g, layer, ic, state, c_w, c_scale):
    hp = _rmsnorm(xp, norm_g, layer)
    hp3 = hp.reshape(BATCH, SEQ, D_MODEL)
    yp = _pool_prompt(hp3, xp.reshape(BATCH, SEQ, D_MODEL), c_w, c_scale, ic).reshape(MP, D_MODEL)
    state_pad = jnp.pad(state[ic], ((0, 0), (POOL_PAD - POOL_STATE, 0), (0, 0)))
    ys, seq = _pool_sample(xs, norm_g, layer, state_pad, c_w, c_scale, ic)
    pool_p = hp3[:, SEQ - POOL_STATE:]
    first = POOL_PAD + DEC_SEQ - POOL_STATE
    pool_s = seq[:, first:first + POOL_STATE]
    return yp, ys, pool_p, pool_s


def kernel(x_prompt, x_sample, cache_b_kv0, cache_b_kv1, cache_b_kv2, state_c_pool, norm_mix_g, norm_ffn_g, a_w_in, a_ln_g, a_ln_b, a_w_s, a_b_s, a_w_out, b_w_qkv, b_q_g, b_k_g, b_w_out, c_w, c_scale, ffn_w1, ffn_w3, ffn_w2):
    xp = x_prompt.reshape(MP, D_MODEL)
    xs = jnp.pad(x_sample, ((0, 0), (0, SAMPLE_PAD - DEC_SEQ), (0, 0))).reshape(MS, D_MODEL)
    a_v_s, pool_p, pool_s = [], [], []
    kv_p = [[] for _ in range(B_GROUPS)]
    kv_s = [[] for _ in range(B_GROUPS)]
    ia = ib = ic = 0
    for layer in range(DEPTH):
        kind = layer % N_MIXERS
        if kind == 0:
            xp, xs, v_s = _mixer_a(xp, xs, norm_mix_g, layer, ia, a_w_in, a_ln_g, a_ln_b, a_w_s, a_b_s, a_w_out)
            a_v_s.append(v_s.reshape(DEC_BATCH, SAMPLE_PAD, A_WIDTH)[:, :DEC_SEQ])
            ia += 1
        elif kind == 1:
            caches = (cache_b_kv0, cache_b_kv1, cache_b_kv2)
            xp, xs, kvp, kvs = _mixer_b(xp, xs, norm_mix_g, layer, ib, caches, b_w_qkv, b_q_g, b_k_g, b_w_out)
            for g in range(B_GROUPS):
                kv_p[g].append(kvp[g])
                kv_s[g].append(kvs[g])
            ib += 1
        else:
            xp, xs, pp, ps = _mixer_c(xp, xs, norm_mix_g, layer, ic, state_c_pool, c_w, c_scale)
            pool_p.append(pp)
            pool_s.append(ps)
            ic += 1
        xp, xs = _ffn(xp, xs, norm_ffn_g, ffn_w1, ffn_w3, ffn_w2, layer)
    y_prompt = xp.reshape(BATCH, SEQ, D_MODEL)
    y_sample = xs.reshape(DEC_BATCH, SAMPLE_PAD, D_MODEL)[:, :DEC_SEQ]
    return (y_prompt, y_sample, jnp.stack(a_v_s),
            jnp.stack(kv_p[0]), jnp.stack(kv_p[1]), jnp.stack(kv_p[2]),
            jnp.stack(kv_s[0]), jnp.stack(kv_s[1]), jnp.stack(kv_s[2]),
            jnp.stack(pool_p), jnp.stack(pool_s))
```

```python
import functools

import jax
import jax.numpy as jnp
from jax import lax
from jax.experimental import pallas as pl
from jax.experimental.pallas import tpu as pltpu

F32 = jnp.float32
BF16 = jnp.bfloat16

D_MODEL = 2048
BATCH = 4
SEQ = 2048
DEPTH = 4
DEC_BATCH = 8
DEC_SEQ = 4
PAST_LEN = 16384
N_MIXERS = 3
A_CHUNK = 128
A_WIDTH = D_MODEL
A_HEADS = 16
A_HD = A_WIDTH // A_HEADS
B_WINDOWS = (128, 512, 2048)
B_DILATIONS = (1, 4, 16)
B_GROUPS = 3
B_HD = 128
B_HEADS = D_MODEL // B_HD
B_WIDTH = B_HEADS * B_HD
B_SCALE = B_HD ** -0.5
B_BAND = 128
POOL_WINDOWS = (2, 4, 8, 16)
C_GROUPS = 4
C_GW = D_MODEL // C_GROUPS
POOL_STATE = max(POOL_WINDOWS) - 1
POOL_PAD = POOL_STATE + 1
D_FF = ((8 * D_MODEL + 3 * 256 - 1) // (3 * 256)) * 256
EPS = 1e-6

SUBLANES = 8
LANES = 128
VMEM_LIMIT_BYTES = 56 * 1024 * 1024

SAMPLE_PAD = SUBLANES
MP = BATCH * SEQ
MS = DEC_BATCH * SAMPLE_PAD
NEG_BIG = -1e30

TM = 1024
TN = 512
TF_FFN = 256
TN_AMIX = 256
NORM_ROWS = 256


def _params(semantics):
    return pltpu.CompilerParams(dimension_semantics=semantics,
                                vmem_limit_bytes=VMEM_LIMIT_BYTES)


def _dot(a, b):
    return jnp.dot(a, b, preferred_element_type=F32)


def _dot_nt(a, b):
    return lax.dot_general(a, b, (((1,), (1,)), ((), ())), preferred_element_type=F32)


def _rms_rows(x, g):
    r = lax.rsqrt(jnp.mean(x * x, axis=-1, keepdims=True) + EPS)
    return (x * r) * g


def _for_row_chunks(rows, chunk, fn):
    chunk = min(chunk, rows)
    n = rows // chunk
    if n == 1:
        fn(pl.ds(0, chunk))
        return

    def body(c, carry):
        fn(pl.ds(pl.multiple_of(c * chunk, chunk), chunk))
        return carry

    lax.fori_loop(0, n, body, 0)


def _layer_vec(stacked):
    return stacked.reshape(stacked.shape[0], 1, stacked.shape[1])


def _hold_after_first_tile(col_fn, last):
    return lambda i, j: (0, jnp.where(i == 0, col_fn(j), last))


def _ffn_body(x_ref, xs_ref, g_ref, w1_ref, w3_ref, w2_ref, o_ref, os_ref, hn_ref):
    i = pl.program_id(0)
    j = pl.program_id(1)
    tm = x_ref.shape[0]

    @pl.when(j == 0)
    def _():
        def chunk(rows):
            x = x_ref[rows, :]
            hn_ref[rows, :] = _rms_rows(x, g_ref[...]).astype(BF16)
            o_ref[rows, :] = x
        _for_row_chunks(tm, NORM_ROWS, chunk)

    @pl.when((i == 0) & (j == 0))
    def _():
        xs = xs_ref[...]
        hn_ref[tm:, :] = _rms_rows(xs, g_ref[...]).astype(BF16)
        os_ref[...] = xs

    def gate(h):
        a = _dot(h, w1_ref[...].astype(BF16))
        b = _dot(h, w3_ref[...].astype(BF16))
        return (jax.nn.silu(a) * b).astype(BF16)

    @pl.when(i == 0)
    def _():
        gt = gate(hn_ref[...])
        w2 = w2_ref[...].astype(BF16)
        o_ref[...] += _dot(gt[:tm], w2)
        os_ref[...] += _dot(gt[tm:], w2)

    @pl.when(i != 0)
    def _():
        o_ref[...] += _dot(gate(hn_ref[0:tm, :]), w2_ref[...].astype(BF16))


def _ffn(x, xs, g_all, w1_all, w3_all, w2_all, layer):
    return pl.pallas_call(
        _ffn_body,
        grid=(MP // TM, D_FF // TF_FFN),
        in_specs=[
            pl.BlockSpec((TM, D_MODEL), lambda i, j: (i, 0)),
            pl.BlockSpec((MS, D_MODEL), lambda i, j: (0, 0)),
            pl.BlockSpec((None, 1, D_MODEL), lambda i, j: (layer, 0, 0)),
            pl.BlockSpec((None, D_MODEL, TF_FFN), lambda i, j: (layer, 0, j)),
            pl.BlockSpec((None, D_MODEL, TF_FFN), lambda i, j: (layer, 0, j)),
            pl.BlockSpec((None, TF_FFN, D_MODEL), lambda i, j: (layer, j, 0)),
        ],
        out_specs=[pl.BlockSpec((TM, D_MODEL), lambda i, j: (i, 0)),
                   pl.BlockSpec((MS, D_MODEL), lambda i, j: (0, 0))],
        out_shape=[jax.ShapeDtypeStruct((MP, D_MODEL), F32),
                   jax.ShapeDtypeStruct((MS, D_MODEL), F32)],
        scratch_shapes=[pltpu.VMEM((TM + MS, D_MODEL), BF16)],
        compiler_params=_params(("arbitrary", "arbitrary")),
        name="ffn",
    )(x, xs, _layer_vec(g_all), w1_all, w3_all, w2_all)


def _fill_lhs(h_ref, x_ref, xs_ref, g_ref, i, j):
    tm = x_ref.shape[0]

    def prep(x):
        if g_ref is not None:
            x = _rms_rows(x, g_ref[...])
        return x.astype(BF16)

    @pl.when(j == 0)
    def _():
        def chunk(rows):
            h_ref[rows, :] = prep(x_ref[rows, :])
        _for_row_chunks(tm, NORM_ROWS, chunk)

    @pl.when((i == 0) & (j == 0))
    def _():
        h_ref[tm:, :] = prep(xs_ref[...])


def _mm_body(*refs, norm, epilogue):
    it = iter(refs)
    x_ref = next(it)
    xs_ref = next(it)
    g_ref = next(it) if norm else None
    w_ref = next(it)
    res_ref = next(it) if epilogue == "residual" else None
    res_s_ref = next(it) if epilogue == "residual" else None
    o_ref = next(it)
    os_ref = next(it)
    h_ref = next(it)
    i = pl.program_id(0)
    j = pl.program_id(1)
    tm = x_ref.shape[0]
    _fill_lhs(h_ref, x_ref, xs_ref, g_ref, i, j)

    def emit(acc, out_ref, r_ref):
        if epilogue == "gelu":
            out_ref[...] = jax.nn.gelu(acc, approximate=True)
        elif epilogue == "residual":
            out_ref[...] = r_ref[...] + acc
        else:
            out_ref[...] = acc

    @pl.when(i == 0)
    def _():
        acc = _dot(h_ref[...], w_ref[...].astype(BF16))
        emit(acc[:tm], o_ref, res_ref)
        emit(acc[tm:], os_ref, res_s_ref)

    @pl.when(i != 0)
    def _():
        emit(_dot(h_ref[0:tm, :], w_ref[...].astype(BF16)), o_ref, res_ref)


def _matmul(x, xs, w_all, layer, *, g_all=None, g_layer=None, epilogue=None, res=None, res_s=None,
            name="matmul"):
    k = x.shape[1]
    n = w_all.shape[2]
    nj = n // TN
    norm = g_all is not None
    args = [x, xs]
    in_specs = [pl.BlockSpec((TM, k), lambda i, j: (i, 0)),
                pl.BlockSpec((MS, k), lambda i, j: (0, 0))]
    if norm:
        args.append(_layer_vec(g_all))
        in_specs.append(pl.BlockSpec((None, 1, k), lambda i, j: (g_layer, 0, 0)))
    args.append(w_all)
    in_specs.append(pl.BlockSpec((None, k, TN), lambda i, j: (layer, 0, j)))
    if epilogue == "residual":
        args += [res, res_s]
        in_specs += [pl.BlockSpec((TM, TN), lambda i, j: (i, j)),
                     pl.BlockSpec((MS, TN), lambda i, j: (0, j))]
    body = functools.partial(_mm_body, norm=norm, epilogue=epilogue)
    return pl.pallas_call(
        body,
        grid=(MP // TM, nj),
        in_specs=in_specs,
        out_specs=[pl.BlockSpec((TM, TN), lambda i, j: (i, j)),
                   pl.BlockSpec((MS, TN), _hold_after_first_tile(lambda j: j, nj - 1))],
        out_shape=[jax.ShapeDtypeStruct((MP, n), F32),
                   jax.ShapeDtypeStruct((MS, n), F32)],
        scratch_shapes=[pltpu.VMEM((TM + MS, k), BF16)],
        compiler_params=_params(("arbitrary", "arbitrary")),
        name=name,
    )(*args)


TN_QKV = 1024
QKV_TILES = B_WIDTH // TN_QKV
QKV_COLS = B_GROUPS * 3 * B_WIDTH
KV_W = 2 * B_WIDTH
MXU_COLS = 256
QKV_SLABS = 2 + B_GROUPS


def _q_slab(g):
    return g // 2, (g % 2) * B_WIDTH


def _qkv_slab_tile(j):
    g = j // (3 * QKV_TILES)
    r = j % (3 * QKV_TILES)
    is_q = r < QKV_TILES
    slab = jnp.where(is_q, g // 2, 2 + g)
    col = jnp.where(is_q, (g % 2) * QKV_TILES + r, r - QKV_TILES)
    return slab, col


QKV_NJ = QKV_COLS // TN_QKV
QKV_LAST_KV = QKV_NJ - 2 * QKV_TILES


def _qkv_body(x_ref, xs_ref, g_ref, w_ref, gain_ref, o_ref, olast_ref, os_ref, h_ref):
    i = pl.program_id(0)
    j = pl.program_id(1)
    tm = x_ref.shape[0]
    _fill_lhs(h_ref, x_ref, xs_ref, g_ref, i, j)
    is_v = (j // QKV_TILES) % 3 == 2
    in_last = j >= QKV_LAST_KV

    def store_heads(acc, out_ref, c0):
        for hh in range(MXU_COLS // B_HD):
            a = acc[:, hh * B_HD:(hh + 1) * B_HD]
            cs = slice(c0 + hh * B_HD, c0 + (hh + 1) * B_HD)
            r = lax.rsqrt(jnp.mean(a * a, axis=-1, keepdims=True) + EPS)
            r = jnp.where(is_v, 1.0, r)
            out_ref[:, cs] = (a * r) * gain_ref[:, cs]

    def run(out_ref, with_sample):
        h = h_ref[...] if with_sample else h_ref[0:tm, :]
        for c0 in range(0, TN_QKV, MXU_COLS):
            acc = _dot(h, w_ref[:, c0:c0 + MXU_COLS].astype(BF16))
            if with_sample:
                store_heads(acc[:tm], out_ref, c0)
                store_heads(acc[tm:], os_ref, c0)
            else:
                store_heads(acc, out_ref, c0)

    for with_sample, tile_cond in ((True, i == 0), (False, i != 0)):
        for out_ref, dest_cond in ((o_ref, jnp.logical_not(in_last)), (olast_ref, in_last)):
            @pl.when(tile_cond & dest_cond)
            def _(out_ref=out_ref, with_sample=with_sample):
                run(out_ref, with_sample)


def _qkv(x, xs, g_all, g_layer, w_all, layer, gain):
    def out_map(i, j):
        slab, col = _qkv_slab_tile(jnp.minimum(j, QKV_LAST_KV - 1))
        return slab, i, col

    def out_last_map(i, j):
        return i, jnp.maximum(j - QKV_LAST_KV, 0)

    def out_s_map(i, j):
        slab, col = _qkv_slab_tile(j)
        first = i == 0
        return (jnp.where(first, slab, QKV_SLABS - 1), 0,
                jnp.where(first, col, KV_W // TN_QKV - 1))

    return pl.pallas_call(
        _qkv_body,
        grid=(MP // TM, QKV_NJ),
        in_specs=[
            pl.BlockSpec((TM, D_MODEL), lambda i, j: (i, 0), pipeline_mode=pl.Buffered(1)),
            pl.BlockSpec((MS, D_MODEL), lambda i, j: (0, 0)),
            pl.BlockSpec((None, 1, D_MODEL), lambda i, j: (g_layer, 0, 0)),
            pl.BlockSpec((None, D_MODEL, TN_QKV), lambda i, j: (layer, 0, j)),
            pl.BlockSpec((1, TN_QKV), lambda i, j: (0, j)),
        ],
        out_specs=[pl.BlockSpec((None, TM, TN_QKV), out_map),
                   pl.BlockSpec((TM, TN_QKV), out_last_map),
                   pl.BlockSpec((None, MS, TN_QKV), out_s_map)],
        out_shape=[jax.ShapeDtypeStruct((QKV_SLABS - 1, MP, KV_W), F32),
                   jax.ShapeDtypeStruct((MP, KV_W), F32),
                   jax.ShapeDtypeStruct((QKV_SLABS, MS, KV_W), F32)],
        scratch_shapes=[pltpu.VMEM((TM + MS, D_MODEL), BF16)],
        compiler_params=_params(("arbitrary", "arbitrary")),
        name="qkv",
    )(x, xs, _layer_vec(g_all), w_all, gain)


def _amix_body(*refs, t_len, block, emit_v):
    (x_ref, u_ref, v_ref, lng_ref, lnb_ref, ws_ref, bias_ref, wout_ref, o_ref) = refs[:9]
    vout_ref = refs[9] if emit_v else None
    gated_ref, wsm_ref = refs[-2:]
    j = pl.program_id(1)

    @pl.when(j == 0)
    def _():
        r = lax.broadcasted_iota(jnp.int32, (t_len, t_len), 0)
        c = lax.broadcasted_iota(jnp.int32, (t_len, t_len), 1)
        keep = c <= r
        if block < t_len:
            keep = keep & ((r // block) == (c // block)) & ((c % block) < DEC_SEQ)
        for h in range(A_HEADS):
            wsm_ref[h] = jnp.where(keep, ws_ref[h], 0.0).astype(BF16)

        def chunk(rows):
            v = v_ref[rows, :]
            mu = jnp.mean(v, axis=-1, keepdims=True)
            d = v - mu
            var = jnp.mean(d * d, axis=-1, keepdims=True)
            vln = (d * lax.rsqrt(var + EPS)) * lng_ref[...] + lnb_ref[...]
            if emit_v:
                vout_ref[rows, :] = vln
            vb = vln.astype(BF16)
            for h in range(A_HEADS):
                cs = slice(h * A_HD, (h + 1) * A_HD)
                mixed = _dot(wsm_ref[h], vb[:, cs]) + bias_ref[:, cs]
                gated_ref[rows, cs] = (u_ref[rows, cs] * mixed).astype(BF16)
        _for_row_chunks(v_ref.shape[0], t_len, chunk)

    o_ref[...] = x_ref[...] + _dot(gated_ref[...], wout_ref[...].astype(BF16))


def _amix(x, z, lng_all, lnb_all, ws_all, ws_idx, bias, wout_all, layer, *, tm, tn, t_len, block, emit_v):
    m = x.shape[0]
    body = functools.partial(_amix_body, t_len=t_len, block=block, emit_v=emit_v)
    out_specs = [pl.BlockSpec((tm, tn), lambda i, j: (i, j))]
    out_shape = [jax.ShapeDtypeStruct((m, D_MODEL), F32)]
    if emit_v:
        out_specs.append(pl.BlockSpec((tm, A_WIDTH), lambda i, j: (i, 0)))
        out_shape.append(jax.ShapeDtypeStruct((m, A_WIDTH), F32))
    return pl.pallas_call(
        body,
        grid=(m // tm, D_MODEL // tn),
        in_specs=[
            pl.BlockSpec((tm, tn), lambda i, j: (i, j)),
            pl.BlockSpec((tm, A_WIDTH), lambda i, j: (i, 0)),
            pl.BlockSpec((tm, A_WIDTH), lambda i, j: (i, 1)),
            pl.BlockSpec((None, 1, A_WIDTH), lambda i, j: (layer, 0, 0)),
            pl.BlockSpec((None, 1, A_WIDTH), lambda i, j: (layer, 0, 0)),
            pl.BlockSpec((None, A_HEADS, t_len, t_len), lambda i, j: (ws_idx, 0, 0, 0)),
            pl.BlockSpec((t_len, A_WIDTH), lambda i, j: (0, 0)),
            pl.BlockSpec((None, A_WIDTH, tn), lambda i, j: (layer, 0, j)),
        ],
        out_specs=out_specs,
        out_shape=out_shape,
        scratch_shapes=[pltpu.VMEM((tm, A_WIDTH), BF16),
                        pltpu.VMEM((A_HEADS, t_len, t_len), BF16)],
        compiler_params=_params(("parallel", "arbitrary")),
        name="amix",
    )(x, z, z, _layer_vec(lng_all), _layer_vec(lnb_all), ws_all, bias, wout_all)


def _attn_prompt_body(q0, k0, v0, q1, k1, v1, q2, k2, v2, o_ref, acc_ref, m_ref, l_ref):
    groups = ((q0, k0, v0), (q1, k1, v1), (q2, k2, v2))
    n = B_BAND
    qi = lax.broadcasted_iota(jnp.int32, (n, 2 * n), 0)
    kj = lax.broadcasted_iota(jnp.int32, (n, 2 * n), 1)
    dist = n + qi - kj
    band_mask = (dist >= 0) & (dist <= n)
    qi1 = lax.broadcasted_iota(jnp.int32, (n, n), 0)
    kj1 = lax.broadcasted_iota(jnp.int32, (n, n), 1)
    causal_mask = kj1 <= qi1

    for g, (q_ref, k_ref, v_ref) in enumerate(groups):
        dil = B_DILATIONS[g]
        n_blocks = SEQ // dil // n
        for r in range(dil):
            for c in range(n_blocks):
                start = r + c * n * dil
                if dil == 1:
                    rows_q = pl.ds(start, n)
                else:
                    rows_q = pl.ds(start, n, stride=dil)
                if c == 0:
                    rows_k, mask = rows_q, causal_mask
                elif dil == 1:
                    rows_k, mask = pl.ds(start - n, 2 * n), band_mask
                else:
                    rows_k, mask = pl.ds(start - n * dil, 2 * n, stride=dil), band_mask
                q = q_ref[0, rows_q, :].astype(BF16)
                k = k_ref[0, rows_k, :].astype(BF16)
                v = v_ref[0, rows_k, :].astype(BF16)
                s = _dot_nt(q, k) * B_SCALE
                s = jnp.where(mask, s, NEG_BIG)
                m = jnp.max(s, axis=-1, keepdims=True)
                p = jnp.exp(s - m)
                l = jnp.sum(p, axis=-1, keepdims=True)
                acc = _dot(p.astype(BF16), v)
                if g == 0:
                    acc_ref[rows_q, :] = acc
                    m_ref[rows_q, :] = jnp.broadcast_to(m, (n, LANES))
                    l_ref[rows_q, :] = jnp.broadcast_to(l, (n, LANES))
                else:
                    m_old = m_ref[rows_q, :]
                    m_new = jnp.maximum(m_old, m)
                    a_old = jnp.exp(m_old - m_new)
                    a_new = jnp.exp(m - m_new)
                    acc_new = acc_ref[rows_q, :] * a_old + acc * a_new
                    l_new = l_ref[rows_q, :] * a_old + l * a_new
                    if g == B_GROUPS - 1:
                        o_ref[0, rows_q, :] = acc_new / l_new
                    else:
                        acc_ref[rows_q, :] = acc_new
                        l_ref[rows_q, :] = l_new
                        m_ref[rows_q, :] = m_new


def _attn_prompt(slabs, kv_last):
    in_specs, args = [], []
    for g in range(B_GROUPS):
        q_slab, q_c0 = _q_slab(g)
        in_specs.append(
            pl.BlockSpec((None, 1, SEQ, B_HD), lambda b, h, s=q_slab, c=q_c0 // B_HD: (s, b, 0, c + h)))
        if g < B_GROUPS - 1:
            in_specs += [pl.BlockSpec((None, 1, SEQ, B_HD), lambda b, h, s=2 + g: (s, b, 0, h)),
                         pl.BlockSpec((None, 1, SEQ, B_HD), lambda b, h, s=2 + g: (s, b, 0, B_HEADS + h))]
            args += [slabs] * 3
        else:
            in_specs += [pl.BlockSpec((1, SEQ, B_HD), lambda b, h: (b, 0, h)),
                         pl.BlockSpec((1, SEQ, B_HD), lambda b, h: (b, 0, B_HEADS + h))]
            args += [slabs, kv_last, kv_last]
    return pl.pallas_call(
        _attn_prompt_body,
        grid=(BATCH, B_HEADS),
        in_specs=in_specs,
        out_specs=pl.BlockSpec((1, SEQ, B_HD), lambda b, h: (b, 0, h)),
        out_shape=jax.ShapeDtypeStruct((BATCH, SEQ, B_WIDTH), F32),
        scratch_shapes=[pltpu.VMEM((SEQ, B_HD), F32),
                        pltpu.VMEM((SEQ, LANES), F32),
                        pltpu.VMEM((SEQ, LANES), F32)],
        compiler_params=_params(("parallel", "parallel")),
        name="attn_prompt",
    )(*args)


assert all(B_WINDOWS[g] == B_BAND * B_DILATIONS[g] for g in range(B_GROUPS))
assert PAST_LEN >= max(B_WINDOWS) and B_DILATIONS[0] == 1 and DEC_SEQ <= min(B_DILATIONS[1:])


def _attn_sample_body(q_ref, kvn_ref, c0_ref, c1_ref, c2_ref, o_ref):
    k_heads = slice(0, B_HEADS)
    v_heads = slice(B_HEADS, 2 * B_HEADS)

    def piece(q, k, v, valid=None):
        s = jnp.sum(k * q[None], axis=-1, keepdims=True) * B_SCALE
        if valid is not None:
            s = jnp.where(valid, s, NEG_BIG)
        m = jnp.max(s, axis=0)
        p = jnp.exp(s - m[None])
        return m, jnp.sum(p, axis=0), jnp.sum(p * v, axis=0)

    row0 = lax.broadcasted_iota(jnp.int32, (B_WINDOWS[0], B_HEADS, 1), 0)
    strided = (None, c1_ref, c2_ref)
    for t in range(DEC_SEQ):
        terms = []
        for g in range(B_GROUPS):
            q = q_ref[g, 0, t]
            if g == 0:
                valid = B_WINDOWS[0] + t - row0 <= B_BAND
                terms.append(piece(q, c0_ref[0, :, k_heads, :], c0_ref[0, :, v_heads, :], valid))
                new = slice(0, t + 1)
            else:
                c_ref = strided[g]
                terms.append(piece(q, c_ref[0, :, t, k_heads, :], c_ref[0, :, t, v_heads, :]))
                new = slice(t, t + 1)
            terms.append(piece(q, kvn_ref[g, 0, new, k_heads, :], kvn_ref[g, 0, new, v_heads, :]))
        m_all = terms[0][0]
        for m, _, _ in terms[1:]:
            m_all = jnp.maximum(m_all, m)
        l_all = jnp.zeros((B_HEADS, 1), F32)
        acc_all = jnp.zeros((B_HEADS, B_HD), F32)
        for m, l, acc in terms:
            w = jnp.exp(m - m_all)
            l_all = l_all + l * w
            acc_all = acc_all + acc * w
        o_ref[0, t] = acc_all / l_all
    o_ref[0, DEC_SEQ:] = jnp.zeros((SAMPLE_PAD - DEC_SEQ, B_HEADS, B_HD), F32)


def _attn_sample(q_s, kv_new, caches):
    kv_rows = 2 * B_HEADS

    def by_residue(g):
        dil = B_DILATIONS[g]
        view = caches[g].reshape(DEC_BATCH, B_BAND, dil, kv_rows, B_HD)
        return view, pl.BlockSpec((1, B_BAND, DEC_SEQ, kv_rows, B_HD), lambda b: (b, 0, 0, 0, 0))

    c1, c1_spec = by_residue(1)
    c2, c2_spec = by_residue(2)
    return pl.pallas_call(
        _attn_sample_body,
        grid=(DEC_BATCH,),
        in_specs=[
            pl.BlockSpec((B_GROUPS, 1, SAMPLE_PAD, B_HEADS, B_HD), lambda b: (0, b, 0, 0, 0)),
            pl.BlockSpec((B_GROUPS, 1, SAMPLE_PAD, kv_rows, B_HD), lambda b: (0, b, 0, 0, 0)),
            pl.BlockSpec((1, B_WINDOWS[0], kv_rows, B_HD), lambda b: (b, 0, 0, 0)),
            c1_spec, c2_spec,
        ],
        out_specs=pl.BlockSpec((1, SAMPLE_PAD, B_HEADS, B_HD), lambda b: (b, 0, 0, 0)),
        out_shape=jax.ShapeDtypeStruct((DEC_BATCH, SAMPLE_PAD, B_HEADS, B_HD), F32),
        compiler_params=_params(("parallel",)),
        name="attn_sample",
    )(q_s, kv_new, caches[0], c1, c2)


def _rmsnorm_body(x_ref, g_ref, o_ref):
    def chunk(rows):
        o_ref[rows, :] = _rms_rows(x_ref[rows, :], g_ref[...])
    _for_row_chunks(x_ref.shape[0], NORM_ROWS, chunk)


def _rmsnorm(x, g_all, layer):
    m = x.shape[0]
    return pl.pallas_call(
        _rmsnorm_body,
        grid=(m // TM,),
        in_specs=[pl.BlockSpec((TM, D_MODEL), lambda i: (i, 0)),
                  pl.BlockSpec((None, 1, D_MODEL), lambda i: (layer, 0, 0))],
        out_specs=pl.BlockSpec((TM, D_MODEL), lambda i: (i, 0)),
        out_shape=jax.ShapeDtypeStruct((m, D_MODEL), F32),
        compiler_params=_params(("parallel",)),
        name="rmsnorm",
    )(x, _layer_vec(g_all))


POOL_ROWS = 256


def _pool_prompt_body(h_ref, x_ref, w_ref, scale_ref, o_ref, hpad_ref, z_ref):
    grp = pl.program_id(1)
    hpad_ref[0:POOL_PAD, :] = jnp.zeros((POOL_PAD, C_GW), F32)
    hpad_ref[POOL_PAD:, :] = h_ref[0]

    for gi, w in enumerate(POOL_WINDOWS):
        @pl.when(grp == gi)
        def _(w=w):
            for c in range(SEQ // POOL_ROWS):
                r0 = POOL_PAD + c * POOL_ROWS
                tot = hpad_ref[r0:r0 + POOL_ROWS, :]
                cur = tot
                for k in range(1, w):
                    tot = tot + hpad_ref[r0 - k:r0 - k + POOL_ROWS, :]
                pos = c * POOL_ROWS + lax.broadcasted_iota(jnp.int32, (POOL_ROWS, 1), 0)
                cnt = jnp.minimum(w, pos + 1).astype(F32)
                z_ref[c * POOL_ROWS:(c + 1) * POOL_ROWS, :] = (tot / cnt - cur).astype(BF16)

    y = _dot(z_ref[...], w_ref[...].astype(BF16))
    o_ref[0] = x_ref[0] + y * scale_ref[...]


def _pool_prompt(hp, x, w_all, scale_all, layer):
    blk = pl.BlockSpec((1, SEQ, C_GW), lambda b, g: (b, 0, g))
    return pl.pallas_call(
        _pool_prompt_body,
        grid=(BATCH, C_GROUPS),
        in_specs=[blk, blk,
                  pl.BlockSpec((None, None, C_GW, C_GW), lambda b, g: (layer, g, 0, 0)),
                  pl.BlockSpec((None, 1, C_GW), lambda b, g: (layer, 0, g))],
        out_specs=blk,
        out_shape=jax.ShapeDtypeStruct((BATCH, SEQ, D_MODEL), F32),
        scratch_shapes=[pltpu.VMEM((POOL_PAD + SEQ, C_GW), F32),
                        pltpu.VMEM((SEQ, C_GW), BF16)],
        compiler_params=_params(("parallel", "parallel")),
        name="pool_prompt",
    )(hp, x, w_all, _layer_vec(scale_all))


def _pool_sample_body(x_ref, g_ref, state_ref, w_ref, scale_ref, o_ref, seq_ref, z_ref):
    for b in range(DEC_BATCH):
        rows = slice(b * SAMPLE_PAD, (b + 1) * SAMPLE_PAD)
        seq_ref[b, 0:POOL_PAD, :] = state_ref[b]
        seq_ref[b, POOL_PAD:, :] = _rms_rows(x_ref[rows, :], g_ref[...])
    for b in range(DEC_BATCH):
        rows = slice(b * SAMPLE_PAD, (b + 1) * SAMPLE_PAD)
        for gi, w in enumerate(POOL_WINDOWS):
            cs = slice(gi * C_GW, (gi + 1) * C_GW)
            cur = seq_ref[b, POOL_PAD:POOL_PAD + SAMPLE_PAD, cs]
            tot = cur
            for k in range(1, w):
                tot = tot + seq_ref[b, POOL_PAD - k:POOL_PAD - k + SAMPLE_PAD, cs]
            pos = PAST_LEN + lax.broadcasted_iota(jnp.int32, (SAMPLE_PAD, 1), 0)
            cnt = jnp.minimum(w, pos + 1).astype(F32)
            z_ref[rows, cs] = (tot / cnt - cur).astype(BF16)
    for gi in range(C_GROUPS):
        cs = slice(gi * C_GW, (gi + 1) * C_GW)
        y = _dot(z_ref[:, cs], w_ref[gi].astype(BF16))
        o_ref[:, cs] = x_ref[:, cs] + y * scale_ref[:, cs]


def _pool_sample(xs, g_all, g_layer, state_pad, w_all, scale_all, layer):
    seq_rows = POOL_PAD + SAMPLE_PAD
    return pl.pallas_call(
        _pool_sample_body,
        grid=(1,),
        in_specs=[
            pl.BlockSpec((MS, D_MODEL), lambda i: (0, 0)),
            pl.BlockSpec((None, 1, D_MODEL), lambda i: (g_layer, 0, 0)),
            pl.BlockSpec((DEC_BATCH, POOL_PAD, D_MODEL), lambda i: (0, 0, 0)),
            pl.BlockSpec((None, C_GROUPS, C_GW, C_GW), lambda i: (layer, 0, 0, 0)),
            pl.BlockSpec((None, 1, D_MODEL), lambda i: (layer, 0, 0)),
        ],
        out_specs=[pl.BlockSpec((MS, D_MODEL), lambda i: (0, 0)),
                   pl.BlockSpec((DEC_BATCH, seq_rows, D_MODEL), lambda i: (0, 0, 0))],
        out_shape=[jax.ShapeDtypeStruct((MS, D_MODEL), F32),
                   jax.ShapeDtypeStruct((DEC_BATCH, seq_rows, D_MODEL), F32)],
        scratch_shapes=[pltpu.VMEM((MS, D_MODEL), BF16)],
        compiler_params=_params(("arbitrary",)),
        name="pool_sample",
    )(xs, _layer_vec(g_all), state_pad, w_all, _layer_vec(scale_all))


def _mixer_a(xp, xs, norm_g, layer, ia, a_w_in, a_ln_g, a_ln_b, a_w_s, a_b_s, a_w_out):
    zp, zs = _matmul(xp, xs, a_w_in, ia, g_all=norm_g, g_layer=layer, epilogue="gelu", name="a_in")
    b_s = a_b_s[ia]
    bias_p = jnp.repeat(jnp.transpose(b_s), A_HD, axis=1)
    (yp,) = _amix(xp, zp, a_ln_g, a_ln_b, a_w_s, ia, bias_p, a_w_out, ia,
                  tm=TM, tn=TN_AMIX, t_len=A_CHUNK, block=A_CHUNK, emit_v=False)
    ws_s = jnp.tile(a_w_s[ia][:, :SAMPLE_PAD, :SAMPLE_PAD], (1, DEC_BATCH, DEC_BATCH))[None]
    bias_s = jnp.tile(jnp.repeat(jnp.transpose(b_s[:, :SAMPLE_PAD]), A_HD, axis=1), (DEC_BATCH, 1))
    ys, v_s = _amix(xs, zs, a_ln_g, a_ln_b, ws_s, 0, bias_s, a_w_out, ia,
                    tm=MS, tn=TN, t_len=MS, block=SAMPLE_PAD, emit_v=True)
    return yp, ys, v_s


def _mixer_b(xp, xs, norm_g, layer, ib, caches, b_w_qkv, b_q_g, b_k_g, b_w_out):
    ones = jnp.ones((B_GROUPS, B_WIDTH), F32)
    gain = jnp.stack([jnp.tile(b_q_g[ib], (1, B_HEADS)), jnp.tile(b_k_g[ib], (1, B_HEADS)), ones], axis=1)
    gain = gain.reshape(1, QKV_COLS)
    slabs_p, kv_last, slabs_s = _qkv(xp, xs, norm_g, layer, b_w_qkv, ib, gain)
    slabs_p = slabs_p.reshape(QKV_SLABS - 1, BATCH, SEQ, KV_W)
    kv_last = kv_last.reshape(BATCH, SEQ, KV_W)
    op = _attn_prompt(slabs_p, kv_last)
    q_s = jnp.stack([slabs_s[_q_slab(g)[0], :, _q_slab(g)[1]:_q_slab(g)[1] + B_WIDTH]
                     for g in range(B_GROUPS)])
    q_s = q_s.reshape(B_GROUPS, DEC_BATCH, SAMPLE_PAD, B_HEADS, B_HD)
    kv_new = slabs_s[2:].reshape(B_GROUPS, DEC_BATCH, SAMPLE_PAD, 2 * B_HEADS, B_HD)
    c = [cc[ib].reshape(DEC_BATCH, cc.shape[2], 2 * B_HEADS, B_HD) for cc in caches]
    os_ = _attn_sample(q_s, kv_new, c)
    yp, ys = _matmul(op.reshape(MP, B_WIDTH), os_.reshape(MS, B_WIDTH), b_w_out, ib,
                     epilogue="residual", res=xp, res_s=xs, name="b_out")
    new_p = []
    for g in range(B_GROUPS):
        keep = min(B_WINDOWS[g], SEQ)
        kv = kv_last[:, SEQ - keep:] if g == B_GROUPS - 1 else slabs_p[2 + g, :, SEQ - keep:]
        new_p.append(kv.reshape(BATCH, keep, 2, B_HEADS, B_HD))
    new_s = [kv_new[g, :, :DEC_SEQ].reshape(DEC_BATCH, DEC_SEQ, 2, B_HEADS, B_HD)
             for g in range(B_GROUPS)]
    return yp, ys, new_p, new_s


def _mixer_c(xp, xs, norm_g, layer, ic, state, c_w, c_scale):
    hp = _rmsnorm(xp, norm_g, layer)
    hp3 = hp.reshape(BATCH, SEQ, D_MODEL)
    yp = _pool_prompt(hp3, xp.reshape(BATCH, SEQ, D_MODEL), c_w, c_scale, ic).reshape(MP, D_MODEL)
    state_pad = jnp.pad(state[ic], ((0, 0), (POOL_PAD - POOL_STATE, 0), (0, 0)))
    ys, seq = _pool_sample(xs, norm_g, layer, state_pad, c_w, c_scale, ic)
    pool_p = hp3[:, SEQ - POOL_STATE:]
    first = POOL_PAD + DEC_SEQ - POOL_STATE
    pool_s = seq[:, first:first + POOL_STATE]
    return yp, ys, pool_p, pool_s


def kernel(x_prompt, x_sample, cache_b_kv0, cache_b_kv1, cache_b_kv2, state_c_pool, norm_mix_g, norm_ffn_g, a_w_in, a_ln_g, a_ln_b, a_w_s, a_b_s, a_w_out, b_w_qkv, b_q_g, b_k_g, b_w_out, c_w, c_scale, ffn_w1, ffn_w3, ffn_w2):
    xp = x_prompt.reshape(MP, D_MODEL)
    xs = jnp.pad(x_sample, ((0, 0), (0, SAMPLE_PAD - DEC_SEQ), (0, 0))).reshape(MS, D_MODEL)
    a_v_s, pool_p, pool_s = [], [], []
    kv_p = [[] for _ in range(B_GROUPS)]
    kv_s = [[] for _ in range(B_GROUPS)]
    ia = ib = ic = 0
    for layer in range(DEPTH):
        kind = layer % N_MIXERS
        if kind == 0:
            xp, xs, v_s = _mixer_a(xp, xs, norm_mix_g, layer, ia, a_w_in, a_ln_g, a_ln_b, a_w_s, a_b_s, a_w_out)
            a_v_s.append(v_s.reshape(DEC_BATCH, SAMPLE_PAD, A_WIDTH)[:, :DEC_SEQ])
            ia += 1
        elif kind == 1:
            caches = (cache_b_kv0, cache_b_kv1, cache_b_kv2)
            xp, xs, kvp, kvs = _mixer_b(xp, xs, norm_mix_g, layer, ib, caches, b_w_qkv, b_q_g, b_k_g, b_w_out)
            for g in range(B_GROUPS):
                kv_p[g].append(kvp[g])
                kv_s[g].append(kvs[g])
            ib += 1
        else:
            xp, xs, pp, ps = _mixer_c(xp, xs, norm_mix_g, layer, ic, state_c_pool, c_w, c_scale)
            pool_p.append(pp)
            pool_s.append(ps)
            ic += 1
        xp, xs = _ffn(xp, xs, norm_ffn_g, ffn_w1, ffn_w3, ffn_w2, layer)
    y_prompt = xp.reshape(BATCH, SEQ, D_MODEL)
    y_sample = xs.reshape(DEC_BATCH, SAMPLE_PAD, D_MODEL)[:, :DEC_SEQ]
    return (y_prompt, y_sample, jnp.stack(a_v_s),
            jnp.stack(kv_p[0]), jnp.stack(kv_p[1]), jnp.stack(kv_p[2]),
            jnp.stack(kv_s[0]), jnp.stack(kv_s[1]), jnp.stack(kv_s[2]),
            jnp.stack(pool_p), jnp.stack(pool_s))
```

```python
import jax
import jax.numpy as jnp
from jax import lax
from jax.experimental import pallas as pl
from jax.experimental.pallas import tpu as pltpu

F32 = jnp.float32
BF16 = jnp.bfloat16

D_MODEL = 2048
BATCH = 4
SEQ = 2048
DEPTH = 4
DEC_BATCH = 8
DEC_SEQ = 4
PAST_LEN = 16384
N_MIXERS = 3
A_CHUNK = 128
A_WIDTH = D_MODEL
A_HEADS = 16
A_HD = A_WIDTH // A_HEADS
B_WINDOWS = (128, 512, 2048)
B_DILATIONS = (1, 4, 16)
B_GROUPS = 3
B_HD = 128
B_HEADS = D_MODEL // B_HD
B_WIDTH = B_HEADS * B_HD
B_SCALE = B_HD ** -0.5
B_BAND = 128
POOL_WINDOWS = (2, 4, 8, 16)
C_GROUPS = 4
C_GW = D_MODEL // C_GROUPS
POOL_STATE = max(POOL_WINDOWS) - 1
POOL_PAD = POOL_STATE + 1
D_FF = ((8 * D_MODEL + 3 * 256 - 1) // (3 * 256)) * 256
EPS = 1e-6

SUBLANES = 8
LANES = 128
VMEM_LIMIT_BYTES = 56 * 1024 * 1024

SAMPLE_PAD = SUBLANES
MP = BATCH * SEQ
MS = DEC_BATCH * SAMPLE_PAD
NEG_BIG = -1e30

TM = 1024
TN = 512
TF_FFN = 256
NORM_ROWS = 256


def _params(semantics):
    return pltpu.CompilerParams(dimension_semantics=semantics,
                                vmem_limit_bytes=VMEM_LIMIT_BYTES)


def _dot(a, b):
    return jnp.dot(a, b, preferred_element_type=F32)


def _dot_nt(a, b):
    return lax.dot_general(a, b, (((1,), (1,)), ((), ())), preferred_element_type=F32)


def _rms_rows(x, g):
    r = lax.rsqrt(jnp.mean(x * x, axis=-1, keepdims=True) + EPS)
    return (x * r) * g


def _for_row_chunks(rows, chunk, fn):
    chunk = min(chunk, rows)
    n = rows // chunk
    if n == 1:
        fn(pl.ds(0, chunk))
        return

    def body(c, carry):
        fn(pl.ds(pl.multiple_of(c * chunk, chunk), chunk))
        return carry

    lax.fori_loop(0, n, body, 0)


def _layer_vec(stacked):
    return stacked.reshape(stacked.shape[0], 1, stacked.shape[1])


def _hold_after_first_tile(col_fn, last):
    return lambda i, j: (0, jnp.where(i == 0, col_fn(j), last))


def _ffn_body(x_ref, xs_ref, g_ref, w1_ref, w3_ref, w2_ref, o_ref, os_ref, hn_ref):
    i = pl.program_id(0)
    j = pl.program_id(1)
    tm = x_ref.shape[0]

    @pl.when(j == 0)
    def _():
        def chunk(rows):
            x = x_ref[rows, :]
            hn_ref[rows, :] = _rms_rows(x, g_ref[...]).astype(BF16)
            o_ref[rows, :] = x
        _for_row_chunks(tm, NORM_ROWS, chunk)

    @pl.when((i == 0) & (j == 0))
    def _():
        xs = xs_ref[...]
        hn_ref[tm:, :] = _rms_rows(xs, g_ref[...]).astype(BF16)
        os_ref[...] = xs

    def gate(h):
        a = _dot(h, w1_ref[...].astype(BF16))
        b = _dot(h, w3_ref[...].astype(BF16))
        return (jax.nn.silu(a) * b).astype(BF16)

    @pl.when(i == 0)
    def _():
        gt = gate(hn_ref[...])
        w2 = w2_ref[...].astype(BF16)
        o_ref[...] += _dot(gt[:tm], w2)
        os_ref[...] += _dot(gt[tm:], w2)

    @pl.when(i != 0)
    def _():
        o_ref[...] += _dot(gate(hn_ref[0:tm, :]), w2_ref[...].astype(BF16))


def _ffn(x, xs, g_all, w1_all, w3_all, w2_all, layer):
    return pl.pallas_call(
        _ffn_body,
        grid=(MP // TM, D_FF // TF_FFN),
        in_specs=[
            pl.BlockSpec((TM, D_MODEL), lambda i, j: (i, 0)),
            pl.BlockSpec((MS, D_MODEL), lambda i, j: (0, 0)),
            pl.BlockSpec((None, 1, D_MODEL), lambda i, j: (layer, 0, 0)),
            pl.BlockSpec((None, D_MODEL, TF_FFN), lambda i, j: (layer, 0, j)),
            pl.BlockSpec((None, D_MODEL, TF_FFN), lambda i, j: (layer, 0, j)),
            pl.BlockSpec((None, TF_FFN, D_MODEL), lambda i, j: (layer, j, 0)),
        ],
        out_specs=[pl.BlockSpec((TM, D_MODEL), lambda i, j: (i, 0)),
                   pl.BlockSpec((MS, D_MODEL), lambda i, j: (0, 0))],
        out_shape=[jax.ShapeDtypeStruct((MP, D_MODEL), F32),
                   jax.ShapeDtypeStruct((MS, D_MODEL), F32)],
        scratch_shapes=[pltpu.VMEM((TM + MS, D_MODEL), BF16)],
        compiler_params=_params(("arbitrary", "arbitrary")),
        name="ffn",
    )(x, xs, _layer_vec(g_all), w1_all, w3_all, w2_all)


def _fill_lhs(h_ref, x_ref, xs_ref, g_ref, i, j):
    tm = x_ref.shape[0]

    def prep(x):
        if g_ref is not None:
            x = _rms_rows(x, g_ref[...])
        return x.astype(BF16)

    @pl.when(j == 0)
    def _():
        def chunk(rows):
            h_ref[rows, :] = prep(x_ref[rows, :])
        _for_row_chunks(tm, NORM_ROWS, chunk)

    @pl.when((i == 0) & (j == 0))
    def _():
        h_ref[tm:, :] = prep(xs_ref[...])


def _proj_residual_body(x_ref, xs_ref, w_ref, res_ref, res_s_ref, o_ref, os_ref, h_ref):
    i = pl.program_id(0)
    j = pl.program_id(1)
    tm = x_ref.shape[0]
    _fill_lhs(h_ref, x_ref, xs_ref, None, i, j)

    @pl.when(i == 0)
    def _():
        acc = _dot(h_ref[...], w_ref[...].astype(BF16))
        o_ref[...] = res_ref[...] + acc[:tm]
        os_ref[...] = res_s_ref[...] + acc[tm:]

    @pl.when(i != 0)
    def _():
        o_ref[...] = res_ref[...] + _dot(h_ref[0:tm, :], w_ref[...].astype(BF16))


def _proj_residual(x, xs, w_all, layer, res, res_s, *, name):
    k = x.shape[1]
    n = w_all.shape[2]
    nj = n // TN
    return pl.pallas_call(
        _proj_residual_body,
        grid=(MP // TM, nj),
        in_specs=[pl.BlockSpec((TM, k), lambda i, j: (i, 0)),
                  pl.BlockSpec((MS, k), lambda i, j: (0, 0)),
                  pl.BlockSpec((None, k, TN), lambda i, j: (layer, 0, j)),
                  pl.BlockSpec((TM, TN), lambda i, j: (i, j)),
                  pl.BlockSpec((MS, TN), lambda i, j: (0, j))],
        out_specs=[pl.BlockSpec((TM, TN), lambda i, j: (i, j)),
                   pl.BlockSpec((MS, TN), _hold_after_first_tile(lambda j: j, nj - 1))],
        out_shape=[jax.ShapeDtypeStruct((MP, n), F32),
                   jax.ShapeDtypeStruct((MS, n), F32)],
        scratch_shapes=[pltpu.VMEM((TM + MS, k), BF16)],
        compiler_params=_params(("arbitrary", "arbitrary")),
        name=name,
    )(x, xs, w_all, res, res_s)


TN_QKV = 1024
QKV_TILES = B_WIDTH // TN_QKV
QKV_COLS = B_GROUPS * 3 * B_WIDTH
KV_W = 2 * B_WIDTH
MXU_COLS = 256
QKV_SLABS = 2 + B_GROUPS


def _q_slab(g):
    return g // 2, (g % 2) * B_WIDTH


def _qkv_slab_tile(j):
    g = j // (3 * QKV_TILES)
    r = j % (3 * QKV_TILES)
    is_q = r < QKV_TILES
    slab = jnp.where(is_q, g // 2, 2 + g)
    col = jnp.where(is_q, (g % 2) * QKV_TILES + r, r - QKV_TILES)
    return slab, col


QKV_NJ = QKV_COLS // TN_QKV
QKV_LAST_KV = QKV_NJ - 2 * QKV_TILES


def _qkv_body(x_ref, xs_ref, g_ref, w_ref, gain_ref, o_ref, olast_ref, os_ref, h_ref):
    i = pl.program_id(0)
    j = pl.program_id(1)
    tm = x_ref.shape[0]
    _fill_lhs(h_ref, x_ref, xs_ref, g_ref, i, j)
    is_v = (j // QKV_TILES) % 3 == 2
    in_last = j >= QKV_LAST_KV

    def store_heads(acc, out_ref, c0):
        for hh in range(MXU_COLS // B_HD):
            a = acc[:, hh * B_HD:(hh + 1) * B_HD]
            cs = slice(c0 + hh * B_HD, c0 + (hh + 1) * B_HD)
            r = lax.rsqrt(jnp.mean(a * a, axis=-1, keepdims=True) + EPS)
            r = jnp.where(is_v, 1.0, r)
            out_ref[:, cs] = (a * r) * gain_ref[:, cs]

    def run(out_ref, with_sample):
        h = h_ref[...] if with_sample else h_ref[0:tm, :]
        for c0 in range(0, TN_QKV, MXU_COLS):
            acc = _dot(h, w_ref[:, c0:c0 + MXU_COLS].astype(BF16))
            if with_sample:
                store_heads(acc[:tm], out_ref, c0)
                store_heads(acc[tm:], os_ref, c0)
            else:
                store_heads(acc, out_ref, c0)

    for with_sample, tile_cond in ((True, i == 0), (False, i != 0)):
        for out_ref, dest_cond in ((o_ref, jnp.logical_not(in_last)), (olast_ref, in_last)):
            @pl.when(tile_cond & dest_cond)
            def _(out_ref=out_ref, with_sample=with_sample):
                run(out_ref, with_sample)


def _qkv(x, xs, g_all, g_layer, w_all, layer, gain):
    def out_map(i, j):
        slab, col = _qkv_slab_tile(jnp.minimum(j, QKV_LAST_KV - 1))
        return slab, i, col

    def out_last_map(i, j):
        return i, jnp.maximum(j - QKV_LAST_KV, 0)

    def out_s_map(i, j):
        slab, col = _qkv_slab_tile(j)
        first = i == 0
        return (jnp.where(first, slab, QKV_SLABS - 1), 0,
                jnp.where(first, col, KV_W // TN_QKV - 1))

    return pl.pallas_call(
        _qkv_body,
        grid=(MP // TM, QKV_NJ),
        in_specs=[
            pl.BlockSpec((TM, D_MODEL), lambda i, j: (i, 0), pipeline_mode=pl.Buffered(1)),
            pl.BlockSpec((MS, D_MODEL), lambda i, j: (0, 0)),
            pl.BlockSpec((None, 1, D_MODEL), lambda i, j: (g_layer, 0, 0)),
            pl.BlockSpec((None, D_MODEL, TN_QKV), lambda i, j: (layer, 0, j)),
            pl.BlockSpec((1, TN_QKV), lambda i, j: (0, j)),
        ],
        out_specs=[pl.BlockSpec((None, TM, TN_QKV), out_map),
                   pl.BlockSpec((TM, TN_QKV), out_last_map),
                   pl.BlockSpec((None, MS, TN_QKV), out_s_map)],
        out_shape=[jax.ShapeDtypeStruct((QKV_SLABS - 1, MP, KV_W), F32),
                   jax.ShapeDtypeStruct((MP, KV_W), F32),
                   jax.ShapeDtypeStruct((QKV_SLABS, MS, KV_W), F32)],
        scratch_shapes=[pltpu.VMEM((TM + MS, D_MODEL), BF16)],
        compiler_params=_params(("arbitrary", "arbitrary")),
        name="qkv",
    )(x, xs, _layer_vec(g_all), w_all, gain)


def _mixing_weights(wsm_ref, ws_ref, t_len, block):
    r = lax.broadcasted_iota(jnp.int32, (t_len, t_len), 0)
    c = lax.broadcasted_iota(jnp.int32, (t_len, t_len), 1)
    keep = c <= r
    if block < t_len:
        keep = keep & ((r // block) == (c // block)) & ((c % block) < DEC_SEQ)
    for h in range(A_HEADS):
        wsm_ref[h] = jnp.where(keep, ws_ref[h], 0.0).astype(BF16)


TN_A_IN = 512
TN_A_OUT = 256
A_UV = A_WIDTH // TN_A_IN
A_P1 = 2 * A_UV
A_P2 = D_MODEL // TN_A_OUT


def _amix_prompt_body(x_ref, xs_ref, g_ref, win_ref, lng_ref, lnb_ref, ws_ref, bias_ref, wout_ref,
                      xres_ref, o_ref, zs_ref, h_ref, u_ref, v_ref, wsm_ref):
    i = pl.program_id(0)
    j = pl.program_id(1)
    tm = x_ref.shape[0]
    _fill_lhs(h_ref, x_ref, xs_ref, g_ref, i, j)

    def phase1(with_sample, dst_ref, slot):
        h = h_ref[...] if with_sample else h_ref[0:tm, :]
        for c0 in range(0, TN_A_IN, MXU_COLS):
            cs = slice(c0, c0 + MXU_COLS)
            z = jax.nn.gelu(_dot(h, win_ref[:, cs].astype(BF16)), approximate=True)
            if with_sample:
                dst_ref[slot, :, cs] = z[:tm].astype(dst_ref.dtype)
                zs_ref[:, cs] = z[tm:]
            else:
                dst_ref[slot, :, cs] = z.astype(dst_ref.dtype)

    for with_sample, tile_cond in ((True, i == 0), (False, i != 0)):
        for dst_ref, half_cond, slot in ((u_ref, j < A_UV, j),
                                         (v_ref, (j >= A_UV) & (j < A_P1), j - A_UV)):
            @pl.when(tile_cond & half_cond)
            def _(with_sample=with_sample, dst_ref=dst_ref, slot=slot):
                phase1(with_sample, dst_ref, slot)

    @pl.when(j == A_P1)
    def _():
        _mixing_weights(wsm_ref, ws_ref, A_CHUNK, A_CHUNK)

        def chunk(rows):
            vs = [v_ref[k, rows, :] for k in range(A_UV)]
            mu = sum(jnp.sum(vk, axis=-1, keepdims=True) for vk in vs) / A_WIDTH
            ds = [vk - mu for vk in vs]
            var = sum(jnp.sum(dk * dk, axis=-1, keepdims=True) for dk in ds) / A_WIDTH
            r = lax.rsqrt(var + EPS)
            for h in range(A_HEADS):
                k, c = divmod(h * A_HD, TN_A_IN)
                cs = slice(h * A_HD, (h + 1) * A_HD)
                vln = (ds[k][:, c:c + A_HD] * r) * lng_ref[:, cs] + lnb_ref[:, cs]
                mixed = _dot(wsm_ref[h], vln.astype(BF16)) + bias_ref[:, cs]
                h_ref[rows, cs] = (u_ref[k, rows, c:c + A_HD] * mixed).astype(BF16)
        _for_row_chunks(tm, A_CHUNK, chunk)

    @pl.when(j >= A_P1)
    def _():
        o_ref[...] = xres_ref[...] + _dot(h_ref[0:tm, :], wout_ref[...].astype(BF16))


def _amix_prompt(x, xs, g_all, g_layer, win_all, lng_all, lnb_all, ws_all, bias, wout_all, layer):
    def out_col(j):
        return jnp.maximum(j - A_P1, 0)

    def z_col(j):
        return jnp.minimum(j, A_P1 - 1)

    return pl.pallas_call(
        _amix_prompt_body,
        grid=(MP // TM, A_P1 + A_P2),
        in_specs=[
            pl.BlockSpec((TM, D_MODEL), lambda i, j: (i, 0), pipeline_mode=pl.Buffered(1)),
            pl.BlockSpec((MS, D_MODEL), lambda i, j: (0, 0)),
            pl.BlockSpec((None, 1, D_MODEL), lambda i, j: (g_layer, 0, 0)),
            pl.BlockSpec((None, D_MODEL, TN_A_IN), lambda i, j: (layer, 0, z_col(j))),
            pl.BlockSpec((None, 1, A_WIDTH), lambda i, j: (layer, 0, 0)),
            pl.BlockSpec((None, 1, A_WIDTH), lambda i, j: (layer, 0, 0)),
            pl.BlockSpec((None, A_HEADS, A_CHUNK, A_CHUNK), lambda i, j: (layer, 0, 0, 0)),
            pl.BlockSpec((A_CHUNK, A_WIDTH), lambda i, j: (0, 0)),
            pl.BlockSpec((None, A_WIDTH, TN_A_OUT), lambda i, j: (layer, 0, out_col(j))),
            pl.BlockSpec((TM, TN_A_OUT), lambda i, j: (i, out_col(j))),
        ],
        out_specs=[pl.BlockSpec((TM, TN_A_OUT), lambda i, j: (i, out_col(j))),
                   pl.BlockSpec((MS, TN_A_IN), _hold_after_first_tile(z_col, A_P1 - 1))],
        out_shape=[jax.ShapeDtypeStruct((MP, D_MODEL), F32),
                   jax.ShapeDtypeStruct((MS, 2 * A_WIDTH), F32)],
        scratch_shapes=[pltpu.VMEM((TM + MS, D_MODEL), BF16),
                        pltpu.VMEM((A_UV, TM, TN_A_IN), BF16),
                        pltpu.VMEM((A_UV, TM, TN_A_IN), F32),
                        pltpu.VMEM((A_HEADS, A_CHUNK, A_CHUNK), BF16)],
        compiler_params=_params(("arbitrary", "arbitrary")),
        name="amix_prompt",
    )(x, xs, _layer_vec(g_all), win_all, _layer_vec(lng_all), _layer_vec(lnb_all), ws_all, bias,
      wout_all, x)


def _amix_sample_body(x_ref, u_ref, v_ref, lng_ref, lnb_ref, ws_ref, bias_ref, wout_ref,
                      o_ref, vout_ref, gated_ref, wsm_ref):
    j = pl.program_id(0)

    @pl.when(j == 0)
    def _():
        _mixing_weights(wsm_ref, ws_ref, MS, SAMPLE_PAD)
        v = v_ref[...]
        mu = jnp.mean(v, axis=-1, keepdims=True)
        d = v - mu
        var = jnp.mean(d * d, axis=-1, keepdims=True)
        vln = (d * lax.rsqrt(var + EPS)) * lng_ref[...] + lnb_ref[...]
        vout_ref[...] = vln
        vb = vln.astype(BF16)
        for h in range(A_HEADS):
            cs = slice(h * A_HD, (h + 1) * A_HD)
            mixed = _dot(wsm_ref[h], vb[:, cs]) + bias_ref[:, cs]
            gated_ref[:, cs] = (u_ref[:, cs] * mixed).astype(BF16)

    o_ref[...] = x_ref[...] + _dot(gated_ref[...], wout_ref[...].astype(BF16))


def _amix_sample(xs, zs, lng_all, lnb_all, ws_s, bias_s, wout_all, layer):
    return pl.pallas_call(
        _amix_sample_body,
        grid=(D_MODEL // TN,),
        in_specs=[
            pl.BlockSpec((MS, TN), lambda j: (0, j)),
            pl.BlockSpec((MS, A_WIDTH), lambda j: (0, 0)),
            pl.BlockSpec((MS, A_WIDTH), lambda j: (0, 1)),
            pl.BlockSpec((None, 1, A_WIDTH), lambda j: (layer, 0, 0)),
            pl.BlockSpec((None, 1, A_WIDTH), lambda j: (layer, 0, 0)),
            pl.BlockSpec((A_HEADS, MS, MS), lambda j: (0, 0, 0)),
            pl.BlockSpec((MS, A_WIDTH), lambda j: (0, 0)),
            pl.BlockSpec((None, A_WIDTH, TN), lambda j: (layer, 0, j)),
        ],
        out_specs=[pl.BlockSpec((MS, TN), lambda j: (0, j)),
                   pl.BlockSpec((MS, A_WIDTH), lambda j: (0, 0))],
        out_shape=[jax.ShapeDtypeStruct((MS, D_MODEL), F32),
                   jax.ShapeDtypeStruct((MS, A_WIDTH), F32)],
        scratch_shapes=[pltpu.VMEM((MS, A_WIDTH), BF16),
                        pltpu.VMEM((A_HEADS, MS, MS), BF16)],
        compiler_params=_params(("arbitrary",)),
        name="amix_sample",
    )(xs, zs, zs, _layer_vec(lng_all), _layer_vec(lnb_all), ws_s, bias_s, wout_all)


def _attn_prompt_body(q0, k0, v0, q1, k1, v1, q2, k2, v2, o_ref, acc_ref, m_ref, l_ref):
    groups = ((q0, k0, v0), (q1, k1, v1), (q2, k2, v2))
    n = B_BAND
    qi = lax.broadcasted_iota(jnp.int32, (n, 2 * n), 0)
    kj = lax.broadcasted_iota(jnp.int32, (n, 2 * n), 1)
    dist = n + qi - kj
    band_mask = (dist >= 0) & (dist <= n)
    qi1 = lax.broadcasted_iota(jnp.int32, (n, n), 0)
    kj1 = lax.broadcasted_iota(jnp.int32, (n, n), 1)
    causal_mask = kj1 <= qi1

    for g, (q_ref, k_ref, v_ref) in enumerate(groups):
        dil = B_DILATIONS[g]
        n_blocks = SEQ // dil // n
        for r in range(dil):
            for c in range(n_blocks):
                start = r + c * n * dil
                if dil == 1:
                    rows_q = pl.ds(start, n)
                else:
                    rows_q = pl.ds(start, n, stride=dil)
                if c == 0:
                    rows_k, mask = rows_q, causal_mask
                elif dil == 1:
                    rows_k, mask = pl.ds(start - n, 2 * n), band_mask
                else:
                    rows_k, mask = pl.ds(start - n * dil, 2 * n, stride=dil), band_mask
                q = q_ref[0, rows_q, :].astype(BF16)
                k = k_ref[0, rows_k, :].astype(BF16)
                v = v_ref[0, rows_k, :].astype(BF16)
                s = _dot_nt(q, k) * B_SCALE
                s = jnp.where(mask, s, NEG_BIG)
                m = jnp.max(s, axis=-1, keepdims=True)
                p = jnp.exp(s - m)
                l = jnp.sum(p, axis=-1, keepdims=True)
                acc = _dot(p.astype(BF16), v)
                if g == 0:
                    acc_ref[rows_q, :] = acc
                    m_ref[rows_q, :] = jnp.broadcast_to(m, (n, LANES))
                    l_ref[rows_q, :] = jnp.broadcast_to(l, (n, LANES))
                else:
                    m_old = m_ref[rows_q, :]
                    m_new = jnp.maximum(m_old, m)
                    a_old = jnp.exp(m_old - m_new)
                    a_new = jnp.exp(m - m_new)
                    acc_new = acc_ref[rows_q, :] * a_old + acc * a_new
                    l_new = l_ref[rows_q, :] * a_old + l * a_new
                    if g == B_GROUPS - 1:
                        o_ref[0, rows_q, :] = acc_new / l_new
                    else:
                        acc_ref[rows_q, :] = acc_new
                        l_ref[rows_q, :] = l_new
                        m_ref[rows_q, :] = m_new


def _attn_prompt(slabs, kv_last):
    in_specs, args = [], []
    for g in range(B_GROUPS):
        q_slab, q_c0 = _q_slab(g)
        in_specs.append(
            pl.BlockSpec((None, 1, SEQ, B_HD), lambda b, h, s=q_slab, c=q_c0 // B_HD: (s, b, 0, c + h)))
        if g < B_GROUPS - 1:
            in_specs += [pl.BlockSpec((None, 1, SEQ, B_HD), lambda b, h, s=2 + g: (s, b, 0, h)),
                         pl.BlockSpec((None, 1, SEQ, B_HD), lambda b, h, s=2 + g: (s, b, 0, B_HEADS + h))]
            args += [slabs] * 3
        else:
            in_specs += [pl.BlockSpec((1, SEQ, B_HD), lambda b, h: (b, 0, h)),
                         pl.BlockSpec((1, SEQ, B_HD), lambda b, h: (b, 0, B_HEADS + h))]
            args += [slabs, kv_last, kv_last]
    return pl.pallas_call(
        _attn_prompt_body,
        grid=(BATCH, B_HEADS),
        in_specs=in_specs,
        out_specs=pl.BlockSpec((1, SEQ, B_HD), lambda b, h: (b, 0, h)),
        out_shape=jax.ShapeDtypeStruct((BATCH, SEQ, B_WIDTH), F32),
        scratch_shapes=[pltpu.VMEM((SEQ, B_HD), F32),
                        pltpu.VMEM((SEQ, LANES), F32),
                        pltpu.VMEM((SEQ, LANES), F32)],
        compiler_params=_params(("parallel", "parallel")),
        name="attn_prompt",
    )(*args)


assert all(B_WINDOWS[g] == B_BAND * B_DILATIONS[g] for g in range(B_GROUPS))
assert PAST_LEN >= max(B_WINDOWS) and B_DILATIONS[0] == 1 and DEC_SEQ <= min(B_DILATIONS[1:])


def _attn_sample_body(q_ref, kvn_ref, c0_ref, c1_ref, c2_ref, o_ref):
    k_heads = slice(0, B_HEADS)
    v_heads = slice(B_HEADS, 2 * B_HEADS)

    def piece(q, k, v, valid=None):
        s = jnp.sum(k * q[None], axis=-1, keepdims=True) * B_SCALE
        if valid is not None:
            s = jnp.where(valid, s, NEG_BIG)
        m = jnp.max(s, axis=0)
        p = jnp.exp(s - m[None])
        return m, jnp.sum(p, axis=0), jnp.sum(p * v, axis=0)

    row0 = lax.broadcasted_iota(jnp.int32, (B_WINDOWS[0], B_HEADS, 1), 0)
    strided = (None, c1_ref, c2_ref)
    for t in range(DEC_SEQ):
        terms = []
        for g in range(B_GROUPS):
            q = q_ref[g, 0, t]
            if g == 0:
                valid = B_WINDOWS[0] + t - row0 <= B_BAND
                terms.append(piece(q, c0_ref[0, :, k_heads, :], c0_ref[0, :, v_heads, :], valid))
                new = slice(0, t + 1)
            else:
                c_ref = strided[g]
                terms.append(piece(q, c_ref[0, :, t, k_heads, :], c_ref[0, :, t, v_heads, :]))
                new = slice(t, t + 1)
            terms.append(piece(q, kvn_ref[g, 0, new, k_heads, :], kvn_ref[g, 0, new, v_heads, :]))
        m_all = terms[0][0]
        for m, _, _ in terms[1:]:
            m_all = jnp.maximum(m_all, m)
        l_all = jnp.zeros((B_HEADS, 1), F32)
        acc_all = jnp.zeros((B_HEADS, B_HD), F32)
        for m, l, acc in terms:
            w = jnp.exp(m - m_all)
            l_all = l_all + l * w
            acc_all = acc_all + acc * w
        o_ref[0, t] = acc_all / l_all
    o_ref[0, DEC_SEQ:] = jnp.zeros((SAMPLE_PAD - DEC_SEQ, B_HEADS, B_HD), F32)


def _attn_sample(q_s, kv_new, caches):
    kv_rows = 2 * B_HEADS

    def by_residue(g):
        dil = B_DILATIONS[g]
        view = caches[g].reshape(DEC_BATCH, B_BAND, dil, kv_rows, B_HD)
        return view, pl.BlockSpec((1, B_BAND, DEC_SEQ, kv_rows, B_HD), lambda b: (b, 0, 0, 0, 0))

    c1, c1_spec = by_residue(1)
    c2, c2_spec = by_residue(2)
    return pl.pallas_call(
        _attn_sample_body,
        grid=(DEC_BATCH,),
        in_specs=[
            pl.BlockSpec((B_GROUPS, 1, SAMPLE_PAD, B_HEADS, B_HD), lambda b: (0, b, 0, 0, 0)),
            pl.BlockSpec((B_GROUPS, 1, SAMPLE_PAD, kv_rows, B_HD), lambda b: (0, b, 0, 0, 0)),
            pl.BlockSpec((1, B_WINDOWS[0], kv_rows, B_HD), lambda b: (b, 0, 0, 0)),
            c1_spec, c2_spec,
        ],
        out_specs=pl.BlockSpec((1, SAMPLE_PAD, B_HEADS, B_HD), lambda b: (b, 0, 0, 0)),
        out_shape=jax.ShapeDtypeStruct((DEC_BATCH, SAMPLE_PAD, B_HEADS, B_HD), F32),
        compiler_params=_params(("parallel",)),
        name="attn_sample",
    )(q_s, kv_new, caches[0], c1, c2)


def _rmsnorm_body(x_ref, g_ref, o_ref):
    def chunk(rows):
        o_ref[rows, :] = _rms_rows(x_ref[rows, :], g_ref[...])
    _for_row_chunks(x_ref.shape[0], NORM_ROWS, chunk)


def _rmsnorm(x, g_all, layer):
    m = x.shape[0]
    return pl.pallas_call(
        _rmsnorm_body,
        grid=(m // TM,),
        in_specs=[pl.BlockSpec((TM, D_MODEL), lambda i: (i, 0)),
                  pl.BlockSpec((None, 1, D_MODEL), lambda i: (layer, 0, 0))],
        out_specs=pl.BlockSpec((TM, D_MODEL), lambda i: (i, 0)),
        out_shape=jax.ShapeDtypeStruct((m, D_MODEL), F32),
        compiler_params=_params(("parallel",)),
        name="rmsnorm",
    )(x, _layer_vec(g_all))


POOL_ROWS = 256


def _pool_prompt_body(h_ref, x_ref, w_ref, scale_ref, o_ref, hpad_ref, z_ref):
    grp = pl.program_id(1)
    hpad_ref[0:POOL_PAD, :] = jnp.zeros((POOL_PAD, C_GW), F32)
    hpad_ref[POOL_PAD:, :] = h_ref[0]

    for gi, w in enumerate(POOL_WINDOWS):
        @pl.when(grp == gi)
        def _(w=w):
            for c in range(SEQ // POOL_ROWS):
                r0 = POOL_PAD + c * POOL_ROWS
                tot = hpad_ref[r0:r0 + POOL_ROWS, :]
                cur = tot
                for k in range(1, w):
                    tot = tot + hpad_ref[r0 - k:r0 - k + POOL_ROWS, :]
                pos = c * POOL_ROWS + lax.broadcasted_iota(jnp.int32, (POOL_ROWS, 1), 0)
                cnt = jnp.minimum(w, pos + 1).astype(F32)
                z_ref[c * POOL_ROWS:(c + 1) * POOL_ROWS, :] = (tot / cnt - cur).astype(BF16)

    y = _dot(z_ref[...], w_ref[...].astype(BF16))
    o_ref[0] = x_ref[0] + y * scale_ref[...]


def _pool_prompt(hp, x, w_all, scale_all, layer):
    blk = pl.BlockSpec((1, SEQ, C_GW), lambda b, g: (b, 0, g))
    return pl.pallas_call(
        _pool_prompt_body,
        grid=(BATCH, C_GROUPS),
        in_specs=[blk, blk,
                  pl.BlockSpec((None, None, C_GW, C_GW), lambda b, g: (layer, g, 0, 0)),
                  pl.BlockSpec((None, 1, C_GW), lambda b, g: (layer, 0, g))],
        out_specs=blk,
        out_shape=jax.ShapeDtypeStruct((BATCH, SEQ, D_MODEL), F32),
        scratch_shapes=[pltpu.VMEM((POOL_PAD + SEQ, C_GW), F32),
                        pltpu.VMEM((SEQ, C_GW), BF16)],
        compiler_params=_params(("parallel", "parallel")),
        name="pool_prompt",
    )(hp, x, w_all, _layer_vec(scale_all))


def _pool_sample_body(x_ref, g_ref, state_ref, w_ref, scale_ref, o_ref, seq_ref, z_ref):
    for b in range(DEC_BATCH):
        rows = slice(b * SAMPLE_PAD, (b + 1) * SAMPLE_PAD)
        seq_ref[b, 0:POOL_PAD, :] = state_ref[b]
        seq_ref[b, POOL_PAD:, :] = _rms_rows(x_ref[rows, :], g_ref[...])
    for b in range(DEC_BATCH):
        rows = slice(b * SAMPLE_PAD, (b + 1) * SAMPLE_PAD)
        for gi, w in enumerate(POOL_WINDOWS):
            cs = slice(gi * C_GW, (gi + 1) * C_GW)
            cur = seq_ref[b, POOL_PAD:POOL_PAD + SAMPLE_PAD, cs]
            tot = cur
            for k in range(1, w):
                tot = tot + seq_ref[b, POOL_PAD - k:POOL_PAD - k + SAMPLE_PAD, cs]
            pos = PAST_LEN + lax.broadcasted_iota(jnp.int32, (SAMPLE_PAD, 1), 0)
            cnt = jnp.minimum(w, pos + 1).astype(F32)
            z_ref[rows, cs] = (tot / cnt - cur).astype(BF16)
    for gi in range(C_GROUPS):
        cs = slice(gi * C_GW, (gi + 1) * C_GW)
        y = _dot(z_ref[:, cs], w_ref[gi].astype(BF16))
        o_ref[:, cs] = x_ref[:, cs] + y * scale_ref[:, cs]


def _pool_sample(xs, g_all, g_layer, state_pad, w_all, scale_all, layer):
    seq_rows = POOL_PAD + SAMPLE_PAD
    return pl.pallas_call(
        _pool_sample_body,
        grid=(1,),
        in_specs=[
            pl.BlockSpec((MS, D_MODEL), lambda i: (0, 0)),
            pl.BlockSpec((None, 1, D_MODEL), lambda i: (g_layer, 0, 0)),
            pl.BlockSpec((DEC_BATCH, POOL_PAD, D_MODEL), lambda i: (0, 0, 0)),
            pl.BlockSpec((None, C_GROUPS, C_GW, C_GW), lambda i: (layer, 0, 0, 0)),
            pl.BlockSpec((None, 1, D_MODEL), lambda i: (layer, 0, 0)),
        ],
        out_specs=[pl.BlockSpec((MS, D_MODEL), lambda i: (0, 0)),
                   pl.BlockSpec((DEC_BATCH, seq_rows, D_MODEL), lambda i: (0, 0, 0))],
        out_shape=[jax.ShapeDtypeStruct((MS, D_MODEL), F32),
                   jax.ShapeDtypeStruct((DEC_BATCH, seq_rows, D_MODEL), F32)],
        scratch_shapes=[pltpu.VMEM((MS, D_MODEL), BF16)],
        compiler_params=_params(("arbitrary",)),
        name="pool_sample",
    )(xs, _layer_vec(g_all), state_pad, w_all, _layer_vec(scale_all))


def _mixer_a(xp, xs, norm_g, layer, ia, a_w_in, a_ln_g, a_ln_b, a_w_s, a_b_s, a_w_out):
    b_s = a_b_s[ia]
    bias_p = jnp.repeat(jnp.transpose(b_s), A_HD, axis=1)
    yp, zs = _amix_prompt(xp, xs, norm_g, layer, a_w_in, a_ln_g, a_ln_b, a_w_s, bias_p, a_w_out, ia)
    ws_s = jnp.tile(a_w_s[ia][:, :SAMPLE_PAD, :SAMPLE_PAD], (1, DEC_BATCH, DEC_BATCH))
    bias_s = jnp.tile(jnp.repeat(jnp.transpose(b_s[:, :SAMPLE_PAD]), A_HD, axis=1), (DEC_BATCH, 1))
    ys, v_s = _amix_sample(xs, zs, a_ln_g, a_ln_b, ws_s, bias_s, a_w_out, ia)
    return yp, ys, v_s


def _mixer_b(xp, xs, norm_g, layer, ib, caches, b_w_qkv, b_q_g, b_k_g, b_w_out):
    ones = jnp.ones((B_GROUPS, B_WIDTH), F32)
    gain = jnp.stack([jnp.tile(b_q_g[ib], (1, B_HEADS)), jnp.tile(b_k_g[ib], (1, B_HEADS)), ones], axis=1)
    gain = gain.reshape(1, QKV_COLS)
    slabs_p, kv_last, slabs_s = _qkv(xp, xs, norm_g, layer, b_w_qkv, ib, gain)
    slabs_p = slabs_p.reshape(QKV_SLABS - 1, BATCH, SEQ, KV_W)
    kv_last = kv_last.reshape(BATCH, SEQ, KV_W)
    op = _attn_prompt(slabs_p, kv_last)
    q_s = jnp.stack([slabs_s[_q_slab(g)[0], :, _q_slab(g)[1]:_q_slab(g)[1] + B_WIDTH]
                     for g in range(B_GROUPS)])
    q_s = q_s.reshape(B_GROUPS, DEC_BATCH, SAMPLE_PAD, B_HEADS, B_HD)
    kv_new = slabs_s[2:].reshape(B_GROUPS, DEC_BATCH, SAMPLE_PAD, 2 * B_HEADS, B_HD)
    c = [cc[ib].reshape(DEC_BATCH, cc.shape[2], 2 * B_HEADS, B_HD) for cc in caches]
    os_ = _attn_sample(q_s, kv_new, c)
    yp, ys = _proj_residual(op.reshape(MP, B_WIDTH), os_.reshape(MS, B_WIDTH), b_w_out, ib, xp, xs,
                            name="b_out")
    new_p = []
    for g in range(B_GROUPS):
        keep = min(B_WINDOWS[g], SEQ)
        kv = kv_last[:, SEQ - keep:] if g == B_GROUPS - 1 else slabs_p[2 + g, :, SEQ - keep:]
        new_p.append(kv.reshape(BATCH, keep, 2, B_HEADS, B_HD))
    new_s = [kv_new[g, :, :DEC_SEQ].reshape(DEC_BATCH, DEC_SEQ, 2, B_HEADS, B_HD)
             for g in range(B_GROUPS)]
    return yp, ys, new_p, new_s


def _mixer_c(xp, xs, norm_g, layer, ic, state, c_w, c_scale):
    hp = _rmsnorm(xp, norm_g, layer)
    hp3 = hp.reshape(BATCH, SEQ, D_MODEL)
    yp = _pool_prompt(hp3, xp.reshape(BATCH, SEQ, D_MODEL), c_w, c_scale, ic).reshape(MP, D_MODEL)
    state_pad = jnp.pad(state[ic], ((0, 0), (POOL_PAD - POOL_STATE, 0), (0, 0)))
    ys, seq = _pool_sample(xs, norm_g, layer, state_pad, c_w, c_scale, ic)
    pool_p = hp3[:, SEQ - POOL_STATE:]
    first = POOL_PAD + DEC_SEQ - POOL_STATE
    pool_s = seq[:, first:first + POOL_STATE]
    return yp, ys, pool_p, pool_s


def kernel(x_prompt, x_sample, cache_b_kv0, cache_b_kv1, cache_b_kv2, state_c_pool, norm_mix_g, norm_ffn_g, a_w_in, a_ln_g, a_ln_b, a_w_s, a_b_s, a_w_out, b_w_qkv, b_q_g, b_k_g, b_w_out, c_w, c_scale, ffn_w1, ffn_w3, ffn_w2):
    xp = x_prompt.reshape(MP, D_MODEL)
    xs = jnp.pad(x_sample, ((0, 0), (0, SAMPLE_PAD - DEC_SEQ), (0, 0))).reshape(MS, D_MODEL)
    a_v_s, pool_p, pool_s = [], [], []
    kv_p = [[] for _ in range(B_GROUPS)]
    kv_s = [[] for _ in range(B_GROUPS)]
    ia = ib = ic = 0
    for layer in range(DEPTH):
        kind = layer % N_MIXERS
        if kind == 0:
            xp, xs, v_s = _mixer_a(xp, xs, norm_mix_g, layer, ia, a_w_in, a_ln_g, a_ln_b, a_w_s, a_b_s, a_w_out)
            a_v_s.append(v_s.reshape(DEC_BATCH, SAMPLE_PAD, A_WIDTH)[:, :DEC_SEQ])
            ia += 1
        elif kind == 1:
            caches = (cache_b_kv0, cache_b_kv1, cache_b_kv2)
            xp, xs, kvp, kvs = _mixer_b(xp, xs, norm_mix_g, layer, ib, caches, b_w_qkv, b_q_g, b_k_g, b_w_out)
            for g in range(B_GROUPS):
                kv_p[g].append(kvp[g])
                kv_s[g].append(kvs[g])
            ib += 1
        else:
            xp, xs, pp, ps = _mixer_c(xp, xs, norm_mix_g, layer, ic, state_c_pool, c_w, c_scale)
            pool_p.append(pp)
            pool_s.append(ps)
            ic += 1
        xp, xs = _ffn(xp, xs, norm_ffn_g, ffn_w1, ffn_w3, ffn_w2, layer)
    y_prompt = xp.reshape(BATCH, SEQ, D_MODEL)
    y_sample = xs.reshape(DEC_BATCH, SAMPLE_PAD, D_MODEL)[:, :DEC_SEQ]
    return (y_prompt, y_sample, jnp.stack(a_v_s),
            jnp.stack(kv_p[0]), jnp.stack(kv_p[1]), jnp.stack(kv_p[2]),
            jnp.stack(kv_s[0]), jnp.stack(kv_s[1]), jnp.stack(kv_s[2]),
            jnp.stack(pool_p), jnp.stack(pool_s))
```

```python
import jax
import jax.numpy as jnp
from jax import lax
from jax.experimental import pallas as pl
from jax.experimental.pallas import tpu as pltpu

F32 = jnp.float32
BF16 = jnp.bfloat16

D_MODEL = 2048
BATCH = 4
SEQ = 2048
DEPTH = 4
DEC_BATCH = 8
DEC_SEQ = 4
PAST_LEN = 16384
N_MIXERS = 3
A_CHUNK = 128
A_WIDTH = D_MODEL
A_HEADS = 16
A_HD = A_WIDTH // A_HEADS
B_WINDOWS = (128, 512, 2048)
B_DILATIONS = (1, 4, 16)
B_GROUPS = 3
B_HD = 128
B_HEADS = D_MODEL // B_HD
B_WIDTH = B_HEADS * B_HD
B_SCALE = B_HD ** -0.5
B_BAND = 128
POOL_WINDOWS = (2, 4, 8, 16)
C_GROUPS = 4
C_GW = D_MODEL // C_GROUPS
POOL_STATE = max(POOL_WINDOWS) - 1
POOL_PAD = POOL_STATE + 1
D_FF = ((8 * D_MODEL + 3 * 256 - 1) // (3 * 256)) * 256
EPS = 1e-6

SUBLANES = 8
LANES = 128
VMEM_LIMIT_BYTES = 56 * 1024 * 1024

SAMPLE_PAD = SUBLANES
MP = BATCH * SEQ
MS = DEC_BATCH * SAMPLE_PAD
NEG_BIG = -1e30

TM = 1024
TN = 512
TF_FFN = 256
NORM_ROWS = 256


def _params(semantics):
    return pltpu.CompilerParams(dimension_semantics=semantics,
                                vmem_limit_bytes=VMEM_LIMIT_BYTES)


def _dot(a, b):
    return jnp.dot(a, b, preferred_element_type=F32)


def _dot_nt(a, b):
    return lax.dot_general(a, b, (((1,), (1,)), ((), ())), preferred_element_type=F32)


def _rms_rows(x, g):
    r = lax.rsqrt(jnp.mean(x * x, axis=-1, keepdims=True) + EPS)
    return (x * r) * g


def _for_row_chunks(rows, chunk, fn):
    chunk = min(chunk, rows)
    n = rows // chunk
    if n == 1:
        fn(pl.ds(0, chunk))
        return

    def body(c, carry):
        fn(pl.ds(pl.multiple_of(c * chunk, chunk), chunk))
        return carry

    lax.fori_loop(0, n, body, 0)


def _layer_vec(stacked):
    return stacked.reshape(stacked.shape[0], 1, stacked.shape[1])


def _hold_after_first_tile(col_fn, last):
    return lambda i, j: (0, jnp.where(i == 0, col_fn(j), last))


def _ffn_body(x_ref, xs_ref, g_ref, w1_ref, w3_ref, w2_ref, o_ref, os_ref, hn_ref):
    i = pl.program_id(0)
    j = pl.program_id(1)
    tm = x_ref.shape[0]

    @pl.when(j == 0)
    def _():
        def chunk(rows):
            x = x_ref[rows, :]
            hn_ref[rows, :] = _rms_rows(x, g_ref[...]).astype(BF16)
            o_ref[rows, :] = x
        _for_row_chunks(tm, NORM_ROWS, chunk)

    @pl.when((i == 0) & (j == 0))
    def _():
        xs = xs_ref[...]
        hn_ref[tm:, :] = _rms_rows(xs, g_ref[...]).astype(BF16)
        os_ref[...] = xs

    def gate(h):
        a = _dot(h, w1_ref[...].astype(BF16))
        b = _dot(h, w3_ref[...].astype(BF16))
        return (jax.nn.silu(a) * b).astype(BF16)

    @pl.when(i == 0)
    def _():
        gt = gate(hn_ref[...])
        w2 = w2_ref[...].astype(BF16)
        o_ref[...] += _dot(gt[:tm], w2)
        os_ref[...] += _dot(gt[tm:], w2)

    @pl.when(i != 0)
    def _():
        o_ref[...] += _dot(gate(hn_ref[0:tm, :]), w2_ref[...].astype(BF16))


def _ffn(x, xs, g_all, w1_all, w3_all, w2_all, layer):
    return pl.pallas_call(
        _ffn_body,
        grid=(MP // TM, D_FF // TF_FFN),
        in_specs=[
            pl.BlockSpec((TM, D_MODEL), lambda i, j: (i, 0)),
            pl.BlockSpec((MS, D_MODEL), lambda i, j: (0, 0)),
            pl.BlockSpec((None, 1, D_MODEL), lambda i, j: (layer, 0, 0)),
            pl.BlockSpec((None, D_MODEL, TF_FFN), lambda i, j: (layer, 0, j)),
            pl.BlockSpec((None, D_MODEL, TF_FFN), lambda i, j: (layer, 0, j)),
            pl.BlockSpec((None, TF_FFN, D_MODEL), lambda i, j: (layer, j, 0)),
        ],
        out_specs=[pl.BlockSpec((TM, D_MODEL), lambda i, j: (i, 0)),
                   pl.BlockSpec((MS, D_MODEL), lambda i, j: (0, 0))],
        out_shape=[jax.ShapeDtypeStruct((MP, D_MODEL), F32),
                   jax.ShapeDtypeStruct((MS, D_MODEL), F32)],
        scratch_shapes=[pltpu.VMEM((TM + MS, D_MODEL), BF16)],
        compiler_params=_params(("arbitrary", "arbitrary")),
        name="ffn",
    )(x, xs, _layer_vec(g_all), w1_all, w3_all, w2_all)


def _fill_lhs(h_ref, x_ref, xs_ref, g_ref, i, j):
    tm = x_ref.shape[0]

    def prep(x):
        if g_ref is not None:
            x = _rms_rows(x, g_ref[...])
        return x.astype(BF16)

    @pl.when(j == 0)
    def _():
        def chunk(rows):
            h_ref[rows, :] = prep(x_ref[rows, :])
        _for_row_chunks(tm, NORM_ROWS, chunk)

    @pl.when((i == 0) & (j == 0))
    def _():
        h_ref[tm:, :] = prep(xs_ref[...])


def _proj_residual_body(x_ref, xs_ref, w_ref, res_ref, res_s_ref, o_ref, os_ref, h_ref):
    i = pl.program_id(0)
    j = pl.program_id(1)
    tm = x_ref.shape[0]
    _fill_lhs(h_ref, x_ref, xs_ref, None, i, j)

    @pl.when(i == 0)
    def _():
        acc = _dot(h_ref[...], w_ref[...].astype(BF16))
        o_ref[...] = res_ref[...] + acc[:tm]
        os_ref[...] = res_s_ref[...] + acc[tm:]

    @pl.when(i != 0)
    def _():
        o_ref[...] = res_ref[...] + _dot(h_ref[0:tm, :], w_ref[...].astype(BF16))


def _proj_residual(x, xs, w_all, layer, res, res_s, *, name):
    k = x.shape[1]
    n = w_all.shape[2]
    nj = n // TN
    return pl.pallas_call(
        _proj_residual_body,
        grid=(MP // TM, nj),
        in_specs=[pl.BlockSpec((TM, k), lambda i, j: (i, 0)),
                  pl.BlockSpec((MS, k), lambda i, j: (0, 0)),
                  pl.BlockSpec((None, k, TN), lambda i, j: (layer, 0, j)),
                  pl.BlockSpec((TM, TN), lambda i, j: (i, j)),
                  pl.BlockSpec((MS, TN), lambda i, j: (0, j))],
        out_specs=[pl.BlockSpec((TM, TN), lambda i, j: (i, j)),
                   pl.BlockSpec((MS, TN), _hold_after_first_tile(lambda j: j, nj - 1))],
        out_shape=[jax.ShapeDtypeStruct((MP, n), F32),
                   jax.ShapeDtypeStruct((MS, n), F32)],
        scratch_shapes=[pltpu.VMEM((TM + MS, k), BF16)],
        compiler_params=_params(("arbitrary", "arbitrary")),
        name=name,
    )(x, xs, w_all, res, res_s)


TN_QKV = 1024
QKV_TILES = B_WIDTH // TN_QKV
QKV_COLS = B_GROUPS * 3 * B_WIDTH
KV_W = 2 * B_WIDTH
MXU_COLS = 256
QKV_SLABS = 2 + B_GROUPS


def _q_slab(g):
    return g // 2, (g % 2) * B_WIDTH


def _qkv_slab_tile(j):
    g = j // (3 * QKV_TILES)
    r = j % (3 * QKV_TILES)
    is_q = r < QKV_TILES
    slab = jnp.where(is_q, g // 2, 2 + g)
    col = jnp.where(is_q, (g % 2) * QKV_TILES + r, r - QKV_TILES)
    return slab, col


QKV_NJ = QKV_COLS // TN_QKV
QKV_LAST_KV = QKV_NJ - 2 * QKV_TILES


def _qkv_body(x_ref, xs_ref, g_ref, w_ref, gain_ref, o_ref, olast_ref, os_ref, h_ref):
    i = pl.program_id(0)
    j = pl.program_id(1)
    tm = x_ref.shape[0]
    _fill_lhs(h_ref, x_ref, xs_ref, g_ref, i, j)
    is_v = (j // QKV_TILES) % 3 == 2
    in_last = j >= QKV_LAST_KV

    def store_heads(acc, out_ref, c0):
        for hh in range(MXU_COLS // B_HD):
            a = acc[:, hh * B_HD:(hh + 1) * B_HD]
            cs = slice(c0 + hh * B_HD, c0 + (hh + 1) * B_HD)
            r = lax.rsqrt(jnp.mean(a * a, axis=-1, keepdims=True) + EPS)
            r = jnp.where(is_v, 1.0, r)
            out_ref[:, cs] = (a * r) * gain_ref[:, cs]

    def run(out_ref, with_sample):
        h = h_ref[...] if with_sample else h_ref[0:tm, :]
        for c0 in range(0, TN_QKV, MXU_COLS):
            acc = _dot(h, w_ref[:, c0:c0 + MXU_COLS].astype(BF16))
            if with_sample:
                store_heads(acc[:tm], out_ref, c0)
                store_heads(acc[tm:], os_ref, c0)
            else:
                store_heads(acc, out_ref, c0)

    for with_sample, tile_cond in ((True, i == 0), (False, i != 0)):
        for out_ref, dest_cond in ((o_ref, jnp.logical_not(in_last)), (olast_ref, in_last)):
            @pl.when(tile_cond & dest_cond)
            def _(out_ref=out_ref, with_sample=with_sample):
                run(out_ref, with_sample)


def _qkv(x, xs, g_all, g_layer, w_all, layer, gain):
    def out_map(i, j):
        slab, col = _qkv_slab_tile(jnp.minimum(j, QKV_LAST_KV - 1))
        return slab, i, col

    def out_last_map(i, j):
        return i, jnp.maximum(j - QKV_LAST_KV, 0)

    def out_s_map(i, j):
        slab, col = _qkv_slab_tile(j)
        first = i == 0
        return (jnp.where(first, slab, QKV_SLABS - 1), 0,
                jnp.where(first, col, KV_W // TN_QKV - 1))

    return pl.pallas_call(
        _qkv_body,
        grid=(MP // TM, QKV_NJ),
        in_specs=[
            pl.BlockSpec((TM, D_MODEL), lambda i, j: (i, 0), pipeline_mode=pl.Buffered(1)),
            pl.BlockSpec((MS, D_MODEL), lambda i, j: (0, 0)),
            pl.BlockSpec((None, 1, D_MODEL), lambda i, j: (g_layer, 0, 0)),
            pl.BlockSpec((None, D_MODEL, TN_QKV), lambda i, j: (layer, 0, j)),
            pl.BlockSpec((1, TN_QKV), lambda i, j: (0, j)),
        ],
        out_specs=[pl.BlockSpec((None, TM, TN_QKV), out_map),
                   pl.BlockSpec((TM, TN_QKV), out_last_map),
                   pl.BlockSpec((None, MS, TN_QKV), out_s_map)],
        out_shape=[jax.ShapeDtypeStruct((QKV_SLABS - 1, MP, KV_W), F32),
                   jax.ShapeDtypeStruct((MP, KV_W), F32),
                   jax.ShapeDtypeStruct((QKV_SLABS, MS, KV_W), F32)],
        scratch_shapes=[pltpu.VMEM((TM + MS, D_MODEL), BF16)],
        compiler_params=_params(("arbitrary", "arbitrary")),
        name="qkv",
    )(x, xs, _layer_vec(g_all), w_all, gain)


def _mixing_weights(wsm_ref, ws_ref, t_len, block):
    r = lax.broadcasted_iota(jnp.int32, (t_len, t_len), 0)
    c = lax.broadcasted_iota(jnp.int32, (t_len, t_len), 1)
    keep = c <= r
    if block < t_len:
        keep = keep & ((r // block) == (c // block)) & ((c % block) < DEC_SEQ)
    for h in range(A_HEADS):
        wsm_ref[h] = jnp.where(keep, ws_ref[h], 0.0).astype(BF16)


TN_A_IN = 1024
TN_A_OUT = 256
A_UV = A_WIDTH // TN_A_IN
A_P1 = 2 * A_UV
A_P2 = D_MODEL // TN_A_OUT


def _amix_prompt_body(x_ref, xs_ref, g_ref, win_ref, lng_ref, lnb_ref, ws_ref, bias_ref, wout_ref,
                      o_ref, zs_ref, h_ref, u_ref, v_ref, wsm_ref):
    i = pl.program_id(0)
    j = pl.program_id(1)
    tm = x_ref.shape[0]
    _fill_lhs(h_ref, x_ref, xs_ref, g_ref, i, j)

    def phase1(with_sample, dst_ref, slot):
        h = h_ref[...] if with_sample else h_ref[0:tm, :]
        for c0 in range(0, TN_A_IN, MXU_COLS):
            cs = slice(c0, c0 + MXU_COLS)
            z = jax.nn.gelu(_dot(h, win_ref[:, cs].astype(BF16)), approximate=True)
            if with_sample:
                dst_ref[slot, :, cs] = z[:tm].astype(dst_ref.dtype)
                zs_ref[:, cs] = z[tm:]
            else:
                dst_ref[slot, :, cs] = z.astype(dst_ref.dtype)

    for with_sample, tile_cond in ((True, i == 0), (False, i != 0)):
        for dst_ref, half_cond, slot in ((u_ref, j < A_UV, j),
                                         (v_ref, (j >= A_UV) & (j < A_P1), j - A_UV)):
            @pl.when(tile_cond & half_cond)
            def _(with_sample=with_sample, dst_ref=dst_ref, slot=slot):
                phase1(with_sample, dst_ref, slot)

    @pl.when(j == A_P1)
    def _():
        _mixing_weights(wsm_ref, ws_ref, A_CHUNK, A_CHUNK)

        def chunk(rows):
            vs = [v_ref[k, rows, :] for k in range(A_UV)]
            mu = sum(jnp.sum(vk, axis=-1, keepdims=True) for vk in vs) / A_WIDTH
            ds = [vk - mu for vk in vs]
            var = sum(jnp.sum(dk * dk, axis=-1, keepdims=True) for dk in ds) / A_WIDTH
            r = lax.rsqrt(var + EPS)
            for h in range(A_HEADS):
                k, c = divmod(h * A_HD, TN_A_IN)
                cs = slice(h * A_HD, (h + 1) * A_HD)
                vln = (ds[k][:, c:c + A_HD] * r) * lng_ref[:, cs] + lnb_ref[:, cs]
                mixed = _dot(wsm_ref[h], vln.astype(BF16)) + bias_ref[:, cs]
                h_ref[rows, cs] = (u_ref[k, rows, c:c + A_HD] * mixed).astype(BF16)
        _for_row_chunks(tm, A_CHUNK, chunk)

    @pl.when(j >= A_P1)
    def _():
        cols = pl.ds(pl.multiple_of((j - A_P1) * TN_A_OUT, TN_A_OUT), TN_A_OUT)
        o_ref[...] = x_ref[:, cols] + _dot(h_ref[0:tm, :], wout_ref[...].astype(BF16))


def _amix_prompt(x, xs, g_all, g_layer, win_all, lng_all, lnb_all, ws_all, bias, wout_all, layer):
    def out_col(j):
        return jnp.maximum(j - A_P1, 0)

    def z_col(j):
        return jnp.minimum(j, A_P1 - 1)

    return pl.pallas_call(
        _amix_prompt_body,
        grid=(MP // TM, A_P1 + A_P2),
        in_specs=[
            pl.BlockSpec((TM, D_MODEL), lambda i, j: (i, 0), pipeline_mode=pl.Buffered(1)),
            pl.BlockSpec((MS, D_MODEL), lambda i, j: (0, 0)),
            pl.BlockSpec((None, 1, D_MODEL), lambda i, j: (g_layer, 0, 0)),
            pl.BlockSpec((None, D_MODEL, TN_A_IN), lambda i, j: (layer, 0, z_col(j))),
            pl.BlockSpec((None, 1, A_WIDTH), lambda i, j: (layer, 0, 0)),
            pl.BlockSpec((None, 1, A_WIDTH), lambda i, j: (layer, 0, 0)),
            pl.BlockSpec((None, A_HEADS, A_CHUNK, A_CHUNK), lambda i, j: (layer, 0, 0, 0)),
            pl.BlockSpec((A_CHUNK, A_WIDTH), lambda i, j: (0, 0)),
            pl.BlockSpec((None, A_WIDTH, TN_A_OUT), lambda i, j: (layer, 0, out_col(j))),
        ],
        out_specs=[pl.BlockSpec((TM, TN_A_OUT), lambda i, j: (i, out_col(j))),
                   pl.BlockSpec((MS, TN_A_IN), _hold_after_first_tile(z_col, A_P1 - 1))],
        out_shape=[jax.ShapeDtypeStruct((MP, D_MODEL), F32),
                   jax.ShapeDtypeStruct((MS, 2 * A_WIDTH), F32)],
        scratch_shapes=[pltpu.VMEM((TM + MS, D_MODEL), BF16),
                        pltpu.VMEM((A_UV, TM, TN_A_IN), BF16),
                        pltpu.VMEM((A_UV, TM, TN_A_IN), F32),
                        pltpu.VMEM((A_HEADS, A_CHUNK, A_CHUNK), BF16)],
        compiler_params=_params(("arbitrary", "arbitrary")),
        name="amix_prompt",
    )(x, xs, _layer_vec(g_all), win_all, _layer_vec(lng_all), _layer_vec(lnb_all), ws_all, bias,
      wout_all)


def _amix_sample_body(x_ref, u_ref, v_ref, lng_ref, lnb_ref, ws_ref, bias_ref, wout_ref,
                      o_ref, vout_ref, gated_ref, wsm_ref):
    j = pl.program_id(0)

    @pl.when(j == 0)
    def _():
        _mixing_weights(wsm_ref, ws_ref, MS, SAMPLE_PAD)
        v = v_ref[...]
        mu = jnp.mean(v, axis=-1, keepdims=True)
        d = v - mu
        var = jnp.mean(d * d, axis=-1, keepdims=True)
        vln = (d * lax.rsqrt(var + EPS)) * lng_ref[...] + lnb_ref[...]
        vout_ref[...] = vln
        vb = vln.astype(BF16)
        for h in range(A_HEADS):
            cs = slice(h * A_HD, (h + 1) * A_HD)
            mixed = _dot(wsm_ref[h], vb[:, cs]) + bias_ref[:, cs]
            gated_ref[:, cs] = (u_ref[:, cs] * mixed).astype(BF16)

    o_ref[...] = x_ref[...] + _dot(gated_ref[...], wout_ref[...].astype(BF16))


def _amix_sample(xs, zs, lng_all, lnb_all, ws_s, bias_s, wout_all, layer):
    return pl.pallas_call(
        _amix_sample_body,
        grid=(D_MODEL // TN,),
        in_specs=[
            pl.BlockSpec((MS, TN), lambda j: (0, j)),
            pl.BlockSpec((MS, A_WIDTH), lambda j: (0, 0)),
            pl.BlockSpec((MS, A_WIDTH), lambda j: (0, 1)),
            pl.BlockSpec((None, 1, A_WIDTH), lambda j: (layer, 0, 0)),
            pl.BlockSpec((None, 1, A_WIDTH), lambda j: (layer, 0, 0)),
            pl.BlockSpec((A_HEADS, MS, MS), lambda j: (0, 0, 0)),
            pl.BlockSpec((MS, A_WIDTH), lambda j: (0, 0)),
            pl.BlockSpec((None, A_WIDTH, TN), lambda j: (layer, 0, j)),
        ],
        out_specs=[pl.BlockSpec((MS, TN), lambda j: (0, j)),
                   pl.BlockSpec((MS, A_WIDTH), lambda j: (0, 0))],
        out_shape=[jax.ShapeDtypeStruct((MS, D_MODEL), F32),
                   jax.ShapeDtypeStruct((MS, A_WIDTH), F32)],
        scratch_shapes=[pltpu.VMEM((MS, A_WIDTH), BF16),
                        pltpu.VMEM((A_HEADS, MS, MS), BF16)],
        compiler_params=_params(("arbitrary",)),
        name="amix_sample",
    )(xs, zs, zs, _layer_vec(lng_all), _layer_vec(lnb_all), ws_s, bias_s, wout_all)


def _attn_prompt_body(q0, k0, v0, q1, k1, v1, q2, k2, v2, o_ref, acc_ref, m_ref, l_ref):
    groups = ((q0, k0, v0), (q1, k1, v1), (q2, k2, v2))
    n = B_BAND
    qi = lax.broadcasted_iota(jnp.int32, (n, 2 * n), 0)
    kj = lax.broadcasted_iota(jnp.int32, (n, 2 * n), 1)
    dist = n + qi - kj
    band_mask = (dist >= 0) & (dist <= n)
    qi1 = lax.broadcasted_iota(jnp.int32, (n, n), 0)
    kj1 = lax.broadcasted_iota(jnp.int32, (n, n), 1)
    causal_mask = kj1 <= qi1

    for g, (q_ref, k_ref, v_ref) in enumerate(groups):
        dil = B_DILATIONS[g]
        n_blocks = SEQ // dil // n
        for r in range(dil):
            for c in range(n_blocks):
                start = r + c * n * dil
                if dil == 1:
                    rows_q = pl.ds(start, n)
                else:
                    rows_q = pl.ds(start, n, stride=dil)
                if c == 0:
                    rows_k, mask = rows_q, causal_mask
                elif dil == 1:
                    rows_k, mask = pl.ds(start - n, 2 * n), band_mask
                else:
                    rows_k, mask = pl.ds(start - n * dil, 2 * n, stride=dil), band_mask
                q = (q_ref[0, rows_q, :] * B_SCALE).astype(BF16)
                k = k_ref[0, rows_k, :].astype(BF16)
                v = v_ref[0, rows_k, :].astype(BF16)
                s = jnp.where(mask, _dot_nt(q, k), NEG_BIG)
                m = jnp.max(s, axis=-1, keepdims=True)
                p = jnp.exp(s - m)
                pv = _dot(p.astype(BF16), jnp.concatenate([v, jnp.ones_like(v)], axis=1))
                acc = pv[:, :B_HD]
                l = pv[:, B_HD:]
                if g == 0:
                    acc_ref[rows_q, :] = acc
                    m_ref[rows_q, :] = jnp.broadcast_to(m, (n, LANES))
                    l_ref[rows_q, :] = l
                else:
                    m_old = m_ref[rows_q, :]
                    m_new = jnp.maximum(m_old, m)
                    a_old = jnp.exp(m_old - m_new)
                    a_new = jnp.exp(m - m_new)
                    acc_new = acc_ref[rows_q, :] * a_old + acc * a_new
                    l_new = l_ref[rows_q, :] * a_old + l * a_new
                    if g == B_GROUPS - 1:
                        o_ref[0, rows_q, :] = acc_new / l_new
                    else:
                        acc_ref[rows_q, :] = acc_new
                        l_ref[rows_q, :] = l_new
                        m_ref[rows_q, :] = m_new


def _attn_prompt(slabs, kv_last):
    in_specs, args = [], []
    for g in range(B_GROUPS):
        q_slab, q_c0 = _q_slab(g)
        in_specs.append(
            pl.BlockSpec((None, 1, SEQ, B_HD), lambda b, h, s=q_slab, c=q_c0 // B_HD: (s, b, 0, c + h)))
        if g < B_GROUPS - 1:
            in_specs += [pl.BlockSpec((None, 1, SEQ, B_HD), lambda b, h, s=2 + g: (s, b, 0, h)),
                         pl.BlockSpec((None, 1, SEQ, B_HD), lambda b, h, s=2 + g: (s, b, 0, B_HEADS + h))]
            args += [slabs] * 3
        else:
            in_specs += [pl.BlockSpec((1, SEQ, B_HD), lambda b, h: (b, 0, h)),
                         pl.BlockSpec((1, SEQ, B_HD), lambda b, h: (b, 0, B_HEADS + h))]
            args += [slabs, kv_last, kv_last]
    return pl.pallas_call(
        _attn_prompt_body,
        grid=(BATCH, B_HEADS),
        in_specs=in_specs,
        out_specs=pl.BlockSpec((1, SEQ, B_HD), lambda b, h: (b, 0, h)),
        out_shape=jax.ShapeDtypeStruct((BATCH, SEQ, B_WIDTH), F32),
        scratch_shapes=[pltpu.VMEM((SEQ, B_HD), F32),
                        pltpu.VMEM((SEQ, LANES), F32),
                        pltpu.VMEM((SEQ, LANES), F32)],
        compiler_params=_params(("parallel", "parallel")),
        name="attn_prompt",
    )(*args)


assert all(B_WINDOWS[g] == B_BAND * B_DILATIONS[g] for g in range(B_GROUPS))
assert PAST_LEN >= max(B_WINDOWS) and B_DILATIONS[0] == 1 and DEC_SEQ <= min(B_DILATIONS[1:])


def _attn_sample_body(q_ref, kvn_ref, c0_ref, c1_ref, c2_ref, o_ref):
    k_heads = slice(0, B_HEADS)
    v_heads = slice(B_HEADS, 2 * B_HEADS)

    def piece(q, k, v, valid=None):
        s = jnp.sum(k * q[None], axis=-1, keepdims=True) * B_SCALE
        if valid is not None:
            s = jnp.where(valid, s, NEG_BIG)
        m = jnp.max(s, axis=0)
        p = jnp.exp(s - m[None])
        return m, jnp.sum(p, axis=0), jnp.sum(p * v, axis=0)

    row0 = lax.broadcasted_iota(jnp.int32, (B_WINDOWS[0], B_HEADS, 1), 0)
    strided = (None, c1_ref, c2_ref)
    for t in range(DEC_SEQ):
        terms = []
        for g in range(B_GROUPS):
            q = q_ref[g, 0, t]
            if g == 0:
                valid = B_WINDOWS[0] + t - row0 <= B_BAND
                terms.append(piece(q, c0_ref[0, :, k_heads, :], c0_ref[0, :, v_heads, :], valid))
                new = slice(0, t + 1)
            else:
                c_ref = strided[g]
                terms.append(piece(q, c_ref[0, :, t, k_heads, :], c_ref[0, :, t, v_heads, :]))
                new = slice(t, t + 1)
            terms.append(piece(q, kvn_ref[g, 0, new, k_heads, :], kvn_ref[g, 0, new, v_heads, :]))
        m_all = terms[0][0]
        for m, _, _ in terms[1:]:
            m_all = jnp.maximum(m_all, m)
        l_all = jnp.zeros((B_HEADS, 1), F32)
        acc_all = jnp.zeros((B_HEADS, B_HD), F32)
        for m, l, acc in terms:
            w = jnp.exp(m - m_all)
            l_all = l_all + l * w
            acc_all = acc_all + acc * w
        o_ref[0, t] = acc_all / l_all
    o_ref[0, DEC_SEQ:] = jnp.zeros((SAMPLE_PAD - DEC_SEQ, B_HEADS, B_HD), F32)


def _attn_sample(q_s, kv_new, caches):
    kv_rows = 2 * B_HEADS

    def by_residue(g):
        dil = B_DILATIONS[g]
        view = caches[g].reshape(DEC_BATCH, B_BAND, dil, kv_rows, B_HD)
        return view, pl.BlockSpec((1, B_BAND, DEC_SEQ, kv_rows, B_HD), lambda b: (b, 0, 0, 0, 0))

    c1, c1_spec = by_residue(1)
    c2, c2_spec = by_residue(2)
    return pl.pallas_call(
        _attn_sample_body,
        grid=(DEC_BATCH,),
        in_specs=[
            pl.BlockSpec((B_GROUPS, 1, SAMPLE_PAD, B_HEADS, B_HD), lambda b: (0, b, 0, 0, 0)),
            pl.BlockSpec((B_GROUPS, 1, SAMPLE_PAD, kv_rows, B_HD), lambda b: (0, b, 0, 0, 0)),
            pl.BlockSpec((1, B_WINDOWS[0], kv_rows, B_HD), lambda b: (b, 0, 0, 0)),
            c1_spec, c2_spec,
        ],
        out_specs=pl.BlockSpec((1, SAMPLE_PAD, B_HEADS, B_HD), lambda b: (b, 0, 0, 0)),
        out_shape=jax.ShapeDtypeStruct((DEC_BATCH, SAMPLE_PAD, B_HEADS, B_HD), F32),
        compiler_params=_params(("parallel",)),
        name="attn_sample",
    )(q_s, kv_new, caches[0], c1, c2)


POOL_ROWS = 256
assert all(w & (w - 1) == 0 for w in POOL_WINDOWS)


def _pool_prompt_body(x_ref, g_ref, w_ref, scale_ref, o_ref, tail_ref, r_ref, sum_ref, z_ref):
    grp = pl.program_id(1)
    n_chunks = SEQ // POOL_ROWS

    @pl.when(grp == 0)
    def _():
        def chunk(rows):
            x = x_ref[0, rows, :]
            r_ref[rows, :] = lax.rsqrt(jnp.mean(x * x, axis=-1, keepdims=True) + EPS)
        _for_row_chunks(SEQ, NORM_ROWS, chunk)

    for gi, w in enumerate(POOL_WINDOWS):
        @pl.when(grp == gi)
        def _(gi=gi, w=w):
            cs = slice(gi * C_GW, (gi + 1) * C_GW)

            def normed(c):
                rows = slice(c * POOL_ROWS, (c + 1) * POOL_ROWS)
                return (x_ref[0, rows, cs] * r_ref[rows, :]) * g_ref[:, cs]

            sum_ref[0:POOL_PAD, :] = jnp.zeros((POOL_PAD, C_GW), F32)
            for c in range(n_chunks):
                h = normed(c)
                sum_ref[POOL_PAD + c * POOL_ROWS:POOL_PAD + (c + 1) * POOL_ROWS, :] = h
                if c == n_chunks - 1:
                    tail_ref[0, :, cs] = h[POOL_ROWS - POOL_PAD:]
            k = 1
            while k < w:
                for c in reversed(range(n_chunks)):
                    r0 = POOL_PAD + c * POOL_ROWS
                    sum_ref[r0:r0 + POOL_ROWS, :] = (sum_ref[r0:r0 + POOL_ROWS, :]
                                                    + sum_ref[r0 - k:r0 - k + POOL_ROWS, :])
                k *= 2
            for c in range(n_chunks):
                r0 = POOL_PAD + c * POOL_ROWS
                pos = c * POOL_ROWS + lax.broadcasted_iota(jnp.int32, (POOL_ROWS, 1), 0)
                cnt = jnp.minimum(w, pos + 1).astype(F32)
                z_ref[c * POOL_ROWS:(c + 1) * POOL_ROWS, :] = (
                    sum_ref[r0:r0 + POOL_ROWS, :] / cnt - normed(c)).astype(BF16)
            y = _dot(z_ref[...], w_ref[...].astype(BF16))
            o_ref[0] = x_ref[0, :, cs] + y * scale_ref[...]


def _pool_prompt(x, g_all, g_layer, w_all, scale_all, layer):
    return pl.pallas_call(
        _pool_prompt_body,
        grid=(BATCH, C_GROUPS),
        in_specs=[pl.BlockSpec((1, SEQ, D_MODEL), lambda b, g: (b, 0, 0)),
                  pl.BlockSpec((None, 1, D_MODEL), lambda b, g: (g_layer, 0, 0)),
                  pl.BlockSpec((None, None, C_GW, C_GW), lambda b, g: (layer, g, 0, 0)),
                  pl.BlockSpec((None, 1, C_GW), lambda b, g: (layer, 0, g))],
        out_specs=[pl.BlockSpec((1, SEQ, C_GW), lambda b, g: (b, 0, g)),
                   pl.BlockSpec((1, POOL_PAD, D_MODEL), lambda b, g: (b, 0, 0))],
        out_shape=[jax.ShapeDtypeStruct((BATCH, SEQ, D_MODEL), F32),
                   jax.ShapeDtypeStruct((BATCH, POOL_PAD, D_MODEL), F32)],
        scratch_shapes=[pltpu.VMEM((SEQ, 1), F32),
                        pltpu.VMEM((POOL_PAD + SEQ, C_GW), F32),
                        pltpu.VMEM((SEQ, C_GW), BF16)],
        compiler_params=_params(("arbitrary", "arbitrary")),
        name="pool_prompt",
    )(x, _layer_vec(g_all), w_all, _layer_vec(scale_all))


def _pool_sample_body(x_ref, g_ref, state_ref, w_ref, scale_ref, o_ref, seq_ref, z_ref):
    for b in range(DEC_BATCH):
        rows = slice(b * SAMPLE_PAD, (b + 1) * SAMPLE_PAD)
        seq_ref[b, 0:POOL_PAD, :] = state_ref[b]
        seq_ref[b, POOL_PAD:, :] = _rms_rows(x_ref[rows, :], g_ref[...])
    for b in range(DEC_BATCH):
        rows = slice(b * SAMPLE_PAD, (b + 1) * SAMPLE_PAD)
        for gi, w in enumerate(POOL_WINDOWS):
            cs = slice(gi * C_GW, (gi + 1) * C_GW)
            cur = seq_ref[b, POOL_PAD:POOL_PAD + SAMPLE_PAD, cs]
            tot = cur
            for k in range(1, w):
                tot = tot + seq_ref[b, POOL_PAD - k:POOL_PAD - k + SAMPLE_PAD, cs]
            pos = PAST_LEN + lax.broadcasted_iota(jnp.int32, (SAMPLE_PAD, 1), 0)
            cnt = jnp.minimum(w, pos + 1).astype(F32)
            z_ref[rows, cs] = (tot / cnt - cur).astype(BF16)
    for gi in range(C_GROUPS):
        cs = slice(gi * C_GW, (gi + 1) * C_GW)
        y = _dot(z_ref[:, cs], w_ref[gi].astype(BF16))
        o_ref[:, cs] = x_ref[:, cs] + y * scale_ref[:, cs]


def _pool_sample(xs, g_all, g_layer, state_pad, w_all, scale_all, layer):
    seq_rows = POOL_PAD + SAMPLE_PAD
    return pl.pallas_call(
        _pool_sample_body,
        grid=(1,),
        in_specs=[
            pl.BlockSpec((MS, D_MODEL), lambda i: (0, 0)),
            pl.BlockSpec((None, 1, D_MODEL), lambda i: (g_layer, 0, 0)),
            pl.BlockSpec((DEC_BATCH, POOL_PAD, D_MODEL), lambda i: (0, 0, 0)),
            pl.BlockSpec((None, C_GROUPS, C_GW, C_GW), lambda i: (layer, 0, 0, 0)),
            pl.BlockSpec((None, 1, D_MODEL), lambda i: (layer, 0, 0)),
        ],
        out_specs=[pl.BlockSpec((MS, D_MODEL), lambda i: (0, 0)),
                   pl.BlockSpec((DEC_BATCH, seq_rows, D_MODEL), lambda i: (0, 0, 0))],
        out_shape=[jax.ShapeDtypeStruct((MS, D_MODEL), F32),
                   jax.ShapeDtypeStruct((DEC_BATCH, seq_rows, D_MODEL), F32)],
        scratch_shapes=[pltpu.VMEM((MS, D_MODEL), BF16)],
        compiler_params=_params(("arbitrary",)),
        name="pool_sample",
    )(xs, _layer_vec(g_all), state_pad, w_all, _layer_vec(scale_all))


def _mixer_a(xp, xs, norm_g, layer, ia, a_w_in, a_ln_g, a_ln_b, a_w_s, a_b_s, a_w_out):
    b_s = a_b_s[ia]
    bias_p = jnp.repeat(jnp.transpose(b_s), A_HD, axis=1)
    yp, zs = _amix_prompt(xp, xs, norm_g, layer, a_w_in, a_ln_g, a_ln_b, a_w_s, bias_p, a_w_out, ia)
    ws_s = jnp.tile(a_w_s[ia][:, :SAMPLE_PAD, :SAMPLE_PAD], (1, DEC_BATCH, DEC_BATCH))
    bias_s = jnp.tile(jnp.repeat(jnp.transpose(b_s[:, :SAMPLE_PAD]), A_HD, axis=1), (DEC_BATCH, 1))
    ys, v_s = _amix_sample(xs, zs, a_ln_g, a_ln_b, ws_s, bias_s, a_w_out, ia)
    return yp, ys, v_s


def _mixer_b(xp, xs, norm_g, layer, ib, caches, b_w_qkv, b_q_g, b_k_g, b_w_out):
    ones = jnp.ones((B_GROUPS, B_WIDTH), F32)
    gain = jnp.stack([jnp.tile(b_q_g[ib], (1, B_HEADS)), jnp.tile(b_k_g[ib], (1, B_HEADS)), ones], axis=1)
    gain = gain.reshape(1, QKV_COLS)
    slabs_p, kv_last, slabs_s = _qkv(xp, xs, norm_g, layer, b_w_qkv, ib, gain)
    slabs_p = slabs_p.reshape(QKV_SLABS - 1, BATCH, SEQ, KV_W)
    kv_last = kv_last.reshape(BATCH, SEQ, KV_W)
    op = _attn_prompt(slabs_p, kv_last)
    q_s = jnp.stack([slabs_s[_q_slab(g)[0], :, _q_slab(g)[1]:_q_slab(g)[1] + B_WIDTH]
                     for g in range(B_GROUPS)])
    q_s = q_s.reshape(B_GROUPS, DEC_BATCH, SAMPLE_PAD, B_HEADS, B_HD)
    kv_new = slabs_s[2:].reshape(B_GROUPS, DEC_BATCH, SAMPLE_PAD, 2 * B_HEADS, B_HD)
    c = [cc[ib].reshape(DEC_BATCH, cc.shape[2], 2 * B_HEADS, B_HD) for cc in caches]
    os_ = _attn_sample(q_s, kv_new, c)
    yp, ys = _proj_residual(op.reshape(MP, B_WIDTH), os_.reshape(MS, B_WIDTH), b_w_out, ib, xp, xs,
                            name="b_out")
    new_p = []
    for g in range(B_GROUPS):
        keep = min(B_WINDOWS[g], SEQ)
        kv = kv_last[:, SEQ - keep:] if g == B_GROUPS - 1 else slabs_p[2 + g, :, SEQ - keep:]
        new_p.append(kv.reshape(BATCH, keep, 2, B_HEADS, B_HD))
    new_s = [kv_new[g, :, :DEC_SEQ].reshape(DEC_BATCH, DEC_SEQ, 2, B_HEADS, B_HD)
             for g in range(B_GROUPS)]
    return yp, ys, new_p, new_s


def _mixer_c(xp, xs, norm_g, layer, ic, state, c_w, c_scale):
    yp, tail = _pool_prompt(xp.reshape(BATCH, SEQ, D_MODEL), norm_g, layer, c_w, c_scale, ic)
    yp = yp.reshape(MP, D_MODEL)
    state_pad = jnp.pad(state[ic], ((0, 0), (POOL_PAD - POOL_STATE, 0), (0, 0)))
    ys, seq = _pool_sample(xs, norm_g, layer, state_pad, c_w, c_scale, ic)
    pool_p = tail[:, POOL_PAD - POOL_STATE:]
    first = POOL_PAD + DEC_SEQ - POOL_STATE
    pool_s = seq[:, first:first + POOL_STATE]
    return yp, ys, pool_p, pool_s


def kernel(x_prompt, x_sample, cache_b_kv0, cache_b_kv1, cache_b_kv2, state_c_pool, norm_mix_g, norm_ffn_g, a_w_in, a_ln_g, a_ln_b, a_w_s, a_b_s, a_w_out, b_w_qkv, b_q_g, b_k_g, b_w_out, c_w, c_scale, ffn_w1, ffn_w3, ffn_w2):
    xp = x_prompt.reshape(MP, D_MODEL)
    xs = jnp.pad(x_sample, ((0, 0), (0, SAMPLE_PAD - DEC_SEQ), (0, 0))).reshape(MS, D_MODEL)
    a_v_s, pool_p, pool_s = [], [], []
    kv_p = [[] for _ in range(B_GROUPS)]
    kv_s = [[] for _ in range(B_GROUPS)]
    ia = ib = ic = 0
    for layer in range(DEPTH):
        kind = layer % N_MIXERS
        if kind == 0:
            xp, xs, v_s = _mixer_a(xp, xs, norm_mix_g, layer, ia, a_w_in, a_ln_g, a_ln_b, a_w_s, a_b_s, a_w_out)
            a_v_s.append(v_s.reshape(DEC_BATCH, SAMPLE_PAD, A_WIDTH)[:, :DEC_SEQ])
            ia += 1
        elif kind == 1:
            caches = (cache_b_kv0, cache_b_kv1, cache_b_kv2)
            xp, xs, kvp, kvs = _mixer_b(xp, xs, norm_mix_g, layer, ib, caches, b_w_qkv, b_q_g, b_k_g, b_w_out)
            for g in range(B_GROUPS):
                kv_p[g].append(kvp[g])
                kv_s[g].append(kvs[g])
            ib += 1
        else:
            xp, xs, pp, ps = _mixer_c(xp, xs, norm_mix_g, layer, ic, state_c_pool, c_w, c_scale)
            pool_p.append(pp)
            pool_s.append(ps)
            ic += 1
        xp, xs = _ffn(xp, xs, norm_ffn_g, ffn_w1, ffn_w3, ffn_w2, layer)
    y_prompt = xp.reshape(BATCH, SEQ, D_MODEL)
    y_sample = xs.reshape(DEC_BATCH, SAMPLE_PAD, D_MODEL)[:, :DEC_SEQ]
    return (y_prompt, y_sample, jnp.stack(a_v_s),
            jnp.stack(kv_p[0]), jnp.stack(kv_p[1]), jnp.stack(kv_p[2]),
            jnp.stack(kv_s[0]), jnp.stack(kv_s[1]), jnp.stack(kv_s[2]),
            jnp.stack(pool_p), jnp.stack(pool_s))
```

```python
import jax
import jax.numpy as jnp
from jax import lax
from jax.experimental import pallas as pl
from jax.experimental.pallas import tpu as pltpu

F32 = jnp.float32
BF16 = jnp.bfloat16

D_MODEL = 2048
BATCH = 4
SEQ = 2048
DEPTH = 4
DEC_BATCH = 8
DEC_SEQ = 4
PAST_LEN = 16384
N_MIXERS = 3
A_CHUNK = 128
A_WIDTH = D_MODEL
A_HEADS = 16
A_HD = A_WIDTH // A_HEADS
B_WINDOWS = (128, 512, 2048)
B_DILATIONS = (1, 4, 16)
B_GROUPS = 3
B_HD = 128
B_HEADS = D_MODEL // B_HD
B_WIDTH = B_HEADS * B_HD
B_SCALE = B_HD ** -0.5
B_BAND = 128
POOL_WINDOWS = (2, 4, 8, 16)
C_GROUPS = 4
C_GW = D_MODEL // C_GROUPS
POOL_STATE = max(POOL_WINDOWS) - 1
POOL_PAD = POOL_STATE + 1
D_FF = ((8 * D_MODEL + 3 * 256 - 1) // (3 * 256)) * 256
EPS = 1e-6

SUBLANES = 8
LANES = 128
VMEM_LIMIT_BYTES = 56 * 1024 * 1024

SAMPLE_PAD = SUBLANES
MP = BATCH * SEQ
MS = DEC_BATCH * SAMPLE_PAD
NEG_BIG = -1e30

TM = 1024
TN = 512
TF_FFN = 256
NORM_ROWS = 256


def _params(semantics):
    return pltpu.CompilerParams(dimension_semantics=semantics,
                                vmem_limit_bytes=VMEM_LIMIT_BYTES)


def _dot(a, b):
    return jnp.dot(a, b, preferred_element_type=F32)


def _dot_nt(a, b):
    return lax.dot_general(a, b, (((1,), (1,)), ((), ())), preferred_element_type=F32)


def _rms_rows(x, g):
    r = lax.rsqrt(jnp.mean(x * x, axis=-1, keepdims=True) + EPS)
    return (x * r) * g


def _for_row_chunks(rows, chunk, fn):
    chunk = min(chunk, rows)
    n = rows // chunk
    if n == 1:
        fn(pl.ds(0, chunk))
        return

    def body(c, carry):
        fn(pl.ds(pl.multiple_of(c * chunk, chunk), chunk))
        return carry

    lax.fori_loop(0, n, body, 0)


def _layer_vec(stacked):
    return stacked.reshape(stacked.shape[0], 1, stacked.shape[1])


def _hold_after_first_tile(col_fn, last):
    return lambda i, j: (0, jnp.where(i == 0, col_fn(j), last))


def _ffn_body(x_ref, xs_ref, g_ref, w1_ref, w3_ref, w2_ref, o_ref, os_ref, hn_ref):
    i = pl.program_id(0)
    j = pl.program_id(1)
    tm = x_ref.shape[0]

    @pl.when(j == 0)
    def _():
        def chunk(rows):
            x = x_ref[rows, :]
            hn_ref[rows, :] = _rms_rows(x, g_ref[...]).astype(BF16)
            o_ref[rows, :] = x
        _for_row_chunks(tm, NORM_ROWS, chunk)

    @pl.when((i == 0) & (j == 0))
    def _():
        xs = xs_ref[...]
        hn_ref[tm:, :] = _rms_rows(xs, g_ref[...]).astype(BF16)
        os_ref[...] = xs

    def gate(h):
        a = _dot(h, w1_ref[...].astype(BF16))
        b = _dot(h, w3_ref[...].astype(BF16))
        return (jax.nn.silu(a) * b).astype(BF16)

    @pl.when(i == 0)
    def _():
        gt = gate(hn_ref[...])
        w2 = w2_ref[...].astype(BF16)
        o_ref[...] += _dot(gt[:tm], w2)
        os_ref[...] += _dot(gt[tm:], w2)

    @pl.when(i != 0)
    def _():
        o_ref[...] += _dot(gate(hn_ref[0:tm, :]), w2_ref[...].astype(BF16))


def _ffn(x, xs, g_all, w1_all, w3_all, w2_all, layer):
    return pl.pallas_call(
        _ffn_body,
        grid=(MP // TM, D_FF // TF_FFN),
        in_specs=[
            pl.BlockSpec((TM, D_MODEL), lambda i, j: (i, 0)),
            pl.BlockSpec((MS, D_MODEL), lambda i, j: (0, 0)),
            pl.BlockSpec((None, 1, D_MODEL), lambda i, j: (layer, 0, 0)),
            pl.BlockSpec((None, D_MODEL, TF_FFN), lambda i, j: (layer, 0, j)),
            pl.BlockSpec((None, D_MODEL, TF_FFN), lambda i, j: (layer, 0, j)),
            pl.BlockSpec((None, TF_FFN, D_MODEL), lambda i, j: (layer, j, 0)),
        ],
        out_specs=[pl.BlockSpec((TM, D_MODEL), lambda i, j: (i, 0)),
                   pl.BlockSpec((MS, D_MODEL), lambda i, j: (0, 0))],
        out_shape=[jax.ShapeDtypeStruct((MP, D_MODEL), F32),
                   jax.ShapeDtypeStruct((MS, D_MODEL), F32)],
        scratch_shapes=[pltpu.VMEM((TM + MS, D_MODEL), BF16)],
        compiler_params=_params(("arbitrary", "arbitrary")),
        name="ffn",
    )(x, xs, _layer_vec(g_all), w1_all, w3_all, w2_all)


def _fill_lhs(h_ref, x_ref, xs_ref, g_ref, i, j):
    tm = x_ref.shape[0]

    def prep(x):
        if g_ref is not None:
            x = _rms_rows(x, g_ref[...])
        return x.astype(BF16)

    @pl.when(j == 0)
    def _():
        def chunk(rows):
            h_ref[rows, :] = prep(x_ref[rows, :])
        _for_row_chunks(tm, NORM_ROWS, chunk)

    @pl.when((i == 0) & (j == 0))
    def _():
        h_ref[tm:, :] = prep(xs_ref[...])


def _proj_residual_body(x_ref, xs_ref, w_ref, res_ref, res_s_ref, o_ref, os_ref, h_ref):
    i = pl.program_id(0)
    j = pl.program_id(1)
    tm = x_ref.shape[0]
    _fill_lhs(h_ref, x_ref, xs_ref, None, i, j)
    cols = pl.ds(pl.multiple_of(j * TN, TN), TN)

    @pl.when(i == 0)
    def _():
        acc = _dot(h_ref[...], w_ref[:, cols].astype(BF16))
        o_ref[...] = res_ref[...] + acc[:tm]
        os_ref[...] = res_s_ref[...] + acc[tm:]

    @pl.when(i != 0)
    def _():
        o_ref[...] = res_ref[...] + _dot(h_ref[0:tm, :], w_ref[:, cols].astype(BF16))


def _proj_residual(x, xs, w_all, layer, res, res_s, *, name):
    k = x.shape[1]
    n = w_all.shape[2]
    nj = n // TN
    return pl.pallas_call(
        _proj_residual_body,
        grid=(MP // TM, nj),
        in_specs=[pl.BlockSpec((TM, k), lambda i, j: (i, 0)),
                  pl.BlockSpec((MS, k), lambda i, j: (0, 0)),
                  pl.BlockSpec((None, k, n), lambda i, j: (layer, 0, 0), pipeline_mode=pl.Buffered(1)),
                  pl.BlockSpec((TM, TN), lambda i, j: (i, j)),
                  pl.BlockSpec((MS, TN), lambda i, j: (0, j))],
        out_specs=[pl.BlockSpec((TM, TN), lambda i, j: (i, j)),
                   pl.BlockSpec((MS, TN), _hold_after_first_tile(lambda j: j, nj - 1))],
        out_shape=[jax.ShapeDtypeStruct((MP, n), F32),
                   jax.ShapeDtypeStruct((MS, n), F32)],
        scratch_shapes=[pltpu.VMEM((TM + MS, k), BF16)],
        compiler_params=_params(("arbitrary", "arbitrary")),
        name=name,
    )(x, xs, w_all, res, res_s)


TN_QKV = 1024
QKV_TILES = B_WIDTH // TN_QKV
QKV_COLS = B_GROUPS * 3 * B_WIDTH
KV_W = 2 * B_WIDTH
QKV_HEAD_COLS = QKV_COLS - KV_W
MXU_COLS = 256
QKV_NJ = QKV_COLS // TN_QKV
QKV_LAST_KV = QKV_HEAD_COLS // TN_QKV


def _qkv_body(x_ref, xs_ref, g_ref, w_ref, gain_ref, o_ref, olast_ref, os_ref, h_ref):
    i = pl.program_id(0)
    j = pl.program_id(1)
    tm = x_ref.shape[0]
    _fill_lhs(h_ref, x_ref, xs_ref, g_ref, i, j)
    is_v = (j // QKV_TILES) % 3 == 2
    in_last = j >= QKV_LAST_KV

    def store_heads(acc, out_ref, c0):
        for hh in range(MXU_COLS // B_HD):
            a = acc[:, hh * B_HD:(hh + 1) * B_HD]
            cs = slice(c0 + hh * B_HD, c0 + (hh + 1) * B_HD)
            r = lax.rsqrt(jnp.mean(a * a, axis=-1, keepdims=True) + EPS)
            r = jnp.where(is_v, 1.0, r)
            out_ref[:, cs] = (a * r) * gain_ref[:, cs]

    def run(out_ref, with_sample):
        h = h_ref[...] if with_sample else h_ref[0:tm, :]
        for c0 in range(0, TN_QKV, MXU_COLS):
            acc = _dot(h, w_ref[:, c0:c0 + MXU_COLS].astype(BF16))
            if with_sample:
                store_heads(acc[:tm], out_ref, c0)
                store_heads(acc[tm:], os_ref, c0)
            else:
                store_heads(acc, out_ref, c0)

    for with_sample, tile_cond in ((True, i == 0), (False, i != 0)):
        for out_ref, dest_cond in ((o_ref, jnp.logical_not(in_last)), (olast_ref, in_last)):
            @pl.when(tile_cond & dest_cond)
            def _(out_ref=out_ref, with_sample=with_sample):
                run(out_ref, with_sample)


def _qkv(x, xs, g_all, g_layer, w_all, layer, gain):
    def out_map(i, j):
        return i, jnp.minimum(j, QKV_LAST_KV - 1)

    def out_last_map(i, j):
        return i, jnp.maximum(j - QKV_LAST_KV, 0)

    return pl.pallas_call(
        _qkv_body,
        grid=(MP // TM, QKV_NJ),
        in_specs=[
            pl.BlockSpec((TM, D_MODEL), lambda i, j: (i, 0), pipeline_mode=pl.Buffered(1)),
            pl.BlockSpec((MS, D_MODEL), lambda i, j: (0, 0)),
            pl.BlockSpec((None, 1, D_MODEL), lambda i, j: (g_layer, 0, 0)),
            pl.BlockSpec((None, D_MODEL, TN_QKV), lambda i, j: (layer, 0, j)),
            pl.BlockSpec((1, TN_QKV), lambda i, j: (0, j)),
        ],
        out_specs=[pl.BlockSpec((TM, TN_QKV), out_map),
                   pl.BlockSpec((TM, TN_QKV), out_last_map),
                   pl.BlockSpec((MS, TN_QKV), _hold_after_first_tile(lambda j: j, QKV_NJ - 1))],
        out_shape=[jax.ShapeDtypeStruct((MP, QKV_HEAD_COLS), F32),
                   jax.ShapeDtypeStruct((MP, KV_W), F32),
                   jax.ShapeDtypeStruct((MS, QKV_COLS), F32)],
        scratch_shapes=[pltpu.VMEM((TM + MS, D_MODEL), BF16)],
        compiler_params=_params(("arbitrary", "arbitrary")),
        name="qkv",
    )(x, xs, _layer_vec(g_all), w_all, gain)


def _mixing_weights(wsm_ref, ws_ref, t_len, block):
    r = lax.broadcasted_iota(jnp.int32, (t_len, t_len), 0)
    c = lax.broadcasted_iota(jnp.int32, (t_len, t_len), 1)
    keep = c <= r
    if block < t_len:
        keep = keep & ((r // block) == (c // block)) & ((c % block) < DEC_SEQ)
    for h in range(A_HEADS):
        wsm_ref[h] = jnp.where(keep, ws_ref[h], 0.0).astype(BF16)


TN_A_IN = 1024
TN_A_OUT = 256
A_UV = A_WIDTH // TN_A_IN
A_P1 = 2 * A_UV
A_P2 = D_MODEL // TN_A_OUT


def _amix_prompt_body(x_ref, xs_ref, g_ref, win_ref, lng_ref, lnb_ref, ws_ref, bias_ref, wout_ref,
                      o_ref, zs_ref, h_ref, u_ref, v_ref, wsm_ref):
    i = pl.program_id(0)
    j = pl.program_id(1)
    tm = x_ref.shape[0]
    _fill_lhs(h_ref, x_ref, xs_ref, g_ref, i, j)

    def phase1(with_sample, dst_ref, slot):
        h = h_ref[...] if with_sample else h_ref[0:tm, :]
        for c0 in range(0, TN_A_IN, MXU_COLS):
            cs = slice(c0, c0 + MXU_COLS)
            z = jax.nn.gelu(_dot(h, win_ref[:, cs].astype(BF16)), approximate=True)
            if with_sample:
                dst_ref[slot, :, cs] = z[:tm].astype(dst_ref.dtype)
                zs_ref[:, cs] = z[tm:]
            else:
                dst_ref[slot, :, cs] = z.astype(dst_ref.dtype)

    for with_sample, tile_cond in ((True, i == 0), (False, i != 0)):
        for dst_ref, half_cond, slot in ((u_ref, j < A_UV, j),
                                         (v_ref, (j >= A_UV) & (j < A_P1), j - A_UV)):
            @pl.when(tile_cond & half_cond)
            def _(with_sample=with_sample, dst_ref=dst_ref, slot=slot):
                phase1(with_sample, dst_ref, slot)

    @pl.when(j == A_P1)
    def _():
        _mixing_weights(wsm_ref, ws_ref, A_CHUNK, A_CHUNK)

        def chunk(rows):
            vs = [v_ref[k, rows, :] for k in range(A_UV)]
            mu = sum(jnp.sum(vk, axis=-1, keepdims=True) for vk in vs) / A_WIDTH
            ds = [vk - mu for vk in vs]
            var = sum(jnp.sum(dk * dk, axis=-1, keepdims=True) for dk in ds) / A_WIDTH
            r = lax.rsqrt(var + EPS)
            for h in range(A_HEADS):
                k, c = divmod(h * A_HD, TN_A_IN)
                cs = slice(h * A_HD, (h + 1) * A_HD)
                vln = (ds[k][:, c:c + A_HD] * r) * lng_ref[:, cs] + lnb_ref[:, cs]
                mixed = _dot(wsm_ref[h], vln.astype(BF16)) + bias_ref[:, cs]
                h_ref[rows, cs] = (u_ref[k, rows, c:c + A_HD] * mixed).astype(BF16)
        _for_row_chunks(tm, A_CHUNK, chunk)

    @pl.when(j >= A_P1)
    def _():
        cols = pl.ds(pl.multiple_of((j - A_P1) * TN_A_OUT, TN_A_OUT), TN_A_OUT)
        o_ref[...] = x_ref[:, cols] + _dot(h_ref[0:tm, :], wout_ref[...].astype(BF16))


def _amix_prompt(x, xs, g_all, g_layer, win_all, lng_all, lnb_all, ws_all, bias, wout_all, layer):
    def out_col(j):
        return jnp.maximum(j - A_P1, 0)

    def z_col(j):
        return jnp.minimum(j, A_P1 - 1)

    return pl.pallas_call(
        _amix_prompt_body,
        grid=(MP // TM, A_P1 + A_P2),
        in_specs=[
            pl.BlockSpec((TM, D_MODEL), lambda i, j: (i, 0), pipeline_mode=pl.Buffered(1)),
            pl.BlockSpec((MS, D_MODEL), lambda i, j: (0, 0)),
            pl.BlockSpec((None, 1, D_MODEL), lambda i, j: (g_layer, 0, 0)),
            pl.BlockSpec((None, D_MODEL, TN_A_IN), lambda i, j: (layer, 0, z_col(j))),
            pl.BlockSpec((None, 1, A_WIDTH), lambda i, j: (layer, 0, 0)),
            pl.BlockSpec((None, 1, A_WIDTH), lambda i, j: (layer, 0, 0)),
            pl.BlockSpec((None, A_HEADS, A_CHUNK, A_CHUNK), lambda i, j: (layer, 0, 0, 0)),
            pl.BlockSpec((A_CHUNK, A_WIDTH), lambda i, j: (0, 0)),
            pl.BlockSpec((None, A_WIDTH, TN_A_OUT), lambda i, j: (layer, 0, out_col(j))),
        ],
        out_specs=[pl.BlockSpec((TM, TN_A_OUT), lambda i, j: (i, out_col(j))),
                   pl.BlockSpec((MS, TN_A_IN), _hold_after_first_tile(z_col, A_P1 - 1))],
        out_shape=[jax.ShapeDtypeStruct((MP, D_MODEL), F32),
                   jax.ShapeDtypeStruct((MS, 2 * A_WIDTH), F32)],
        scratch_shapes=[pltpu.VMEM((TM + MS, D_MODEL), BF16),
                        pltpu.VMEM((A_UV, TM, TN_A_IN), BF16),
                        pltpu.VMEM((A_UV, TM, TN_A_IN), F32),
                        pltpu.VMEM((A_HEADS, A_CHUNK, A_CHUNK), BF16)],
        compiler_params=_params(("arbitrary", "arbitrary")),
        name="amix_prompt",
    )(x, xs, _layer_vec(g_all), win_all, _layer_vec(lng_all), _layer_vec(lnb_all), ws_all, bias,
      wout_all)


def _amix_sample_body(x_ref, u_ref, v_ref, lng_ref, lnb_ref, ws_ref, bias_ref, wout_ref,
                      o_ref, vout_ref, gated_ref, wsm_ref):
    j = pl.program_id(0)

    @pl.when(j == 0)
    def _():
        _mixing_weights(wsm_ref, ws_ref, MS, SAMPLE_PAD)
        v = v_ref[...]
        mu = jnp.mean(v, axis=-1, keepdims=True)
        d = v - mu
        var = jnp.mean(d * d, axis=-1, keepdims=True)
        vln = (d * lax.rsqrt(var + EPS)) * lng_ref[...] + lnb_ref[...]
        vout_ref[...] = vln
        vb = vln.astype(BF16)
        for h in range(A_HEADS):
            cs = slice(h * A_HD, (h + 1) * A_HD)
            mixed = _dot(wsm_ref[h], vb[:, cs]) + bias_ref[:, cs]
            gated_ref[:, cs] = (u_ref[:, cs] * mixed).astype(BF16)

    o_ref[...] = x_ref[...] + _dot(gated_ref[...], wout_ref[...].astype(BF16))


def _amix_sample(xs, zs, lng_all, lnb_all, ws_s, bias_s, wout_all, layer):
    return pl.pallas_call(
        _amix_sample_body,
        grid=(D_MODEL // TN,),
        in_specs=[
            pl.BlockSpec((MS, TN), lambda j: (0, j)),
            pl.BlockSpec((MS, A_WIDTH), lambda j: (0, 0)),
            pl.BlockSpec((MS, A_WIDTH), lambda j: (0, 1)),
            pl.BlockSpec((None, 1, A_WIDTH), lambda j: (layer, 0, 0)),
            pl.BlockSpec((None, 1, A_WIDTH), lambda j: (layer, 0, 0)),
            pl.BlockSpec((A_HEADS, MS, MS), lambda j: (0, 0, 0)),
            pl.BlockSpec((MS, A_WIDTH), lambda j: (0, 0)),
            pl.BlockSpec((None, A_WIDTH, TN), lambda j: (layer, 0, j)),
        ],
        out_specs=[pl.BlockSpec((MS, TN), lambda j: (0, j)),
                   pl.BlockSpec((MS, A_WIDTH), lambda j: (0, 0))],
        out_shape=[jax.ShapeDtypeStruct((MS, D_MODEL), F32),
                   jax.ShapeDtypeStruct((MS, A_WIDTH), F32)],
        scratch_shapes=[pltpu.VMEM((MS, A_WIDTH), BF16),
                        pltpu.VMEM((A_HEADS, MS, MS), BF16)],
        compiler_params=_params(("arbitrary",)),
        name="amix_sample",
    )(xs, zs, zs, _layer_vec(lng_all), _layer_vec(lnb_all), ws_s, bias_s, wout_all)


ATTN_HEADS = 1


def _attn_prompt_body(q0, k0, v0, q1, k1, v1, q2, k2, v2, o_ref, acc_ref, m_ref, l_ref):
    groups = ((q0, k0, v0), (q1, k1, v1), (q2, k2, v2))
    n = B_BAND
    qi = lax.broadcasted_iota(jnp.int32, (n, 2 * n), 0)
    kj = lax.broadcasted_iota(jnp.int32, (n, 2 * n), 1)
    dist = n + qi - kj
    band_mask = (dist >= 0) & (dist <= n)
    qi1 = lax.broadcasted_iota(jnp.int32, (n, n), 0)
    kj1 = lax.broadcasted_iota(jnp.int32, (n, n), 1)
    causal_mask = kj1 <= qi1

    for hh in range(ATTN_HEADS):
        cs = slice(hh * B_HD, (hh + 1) * B_HD)
        for step, g in enumerate(range(B_GROUPS)):
            q_ref, k_ref, v_ref = groups[g]
            dil = B_DILATIONS[g]
            n_blocks = SEQ // dil // n
            for r in range(dil):
                for c in range(n_blocks):
                    start = r + c * n * dil
                    if dil == 1:
                        rows_q = pl.ds(start, n)
                    else:
                        rows_q = pl.ds(start, n, stride=dil)
                    if c == 0:
                        rows_k, mask = rows_q, causal_mask
                    elif dil == 1:
                        rows_k, mask = pl.ds(start - n, 2 * n), band_mask
                    else:
                        rows_k, mask = pl.ds(start - n * dil, 2 * n, stride=dil), band_mask
                    q = (q_ref[0, rows_q, cs] * B_SCALE).astype(BF16)
                    k = k_ref[0, rows_k, cs].astype(BF16)
                    v = v_ref[0, rows_k, cs].astype(BF16)
                    s = jnp.where(mask, _dot_nt(q, k), NEG_BIG)
                    m = jnp.max(s, axis=-1, keepdims=True)
                    p = jnp.exp(s - m)
                    pv = _dot(p.astype(BF16), jnp.concatenate([v, jnp.ones_like(v)], axis=1))
                    acc = pv[:, :B_HD]
                    l = pv[:, B_HD:]
                    if step == 0:
                        acc_ref[rows_q, :] = acc
                        m_ref[rows_q, :] = jnp.broadcast_to(m, (n, LANES))
                        l_ref[rows_q, :] = l
                    else:
                        m_old = m_ref[rows_q, :]
                        m_new = jnp.maximum(m_old, m)
                        a_old = jnp.exp(m_old - m_new)
                        a_new = jnp.exp(m - m_new)
                        acc_new = acc_ref[rows_q, :] * a_old + acc * a_new
                        l_new = l_ref[rows_q, :] * a_old + l * a_new
                        if step == B_GROUPS - 1:
                            acc_ref[rows_q, :] = acc_new / l_new
                        else:
                            acc_ref[rows_q, :] = acc_new
                            l_ref[rows_q, :] = l_new
                            m_ref[rows_q, :] = m_new
        o_ref[0, :, cs] = acc_ref[...].astype(o_ref.dtype)


def _attn_prompt(qkv_head, kv_last):
    hw = ATTN_HEADS * B_HD

    def spec(first_col):
        return pl.BlockSpec((1, SEQ, hw), lambda b, h: (b, 0, first_col // hw + h))

    in_specs, args = [], []
    for g in range(B_GROUPS):
        c0 = g * 3 * B_WIDTH
        if g < B_GROUPS - 1:
            in_specs += [spec(c0), spec(c0 + B_WIDTH), spec(c0 + 2 * B_WIDTH)]
            args += [qkv_head] * 3
        else:
            in_specs += [spec(c0), spec(0), spec(B_WIDTH)]
            args += [qkv_head, kv_last, kv_last]
    return pl.pallas_call(
        _attn_prompt_body,
        grid=(BATCH, B_HEADS // ATTN_HEADS),
        in_specs=in_specs,
        out_specs=pl.BlockSpec((1, SEQ, hw), lambda b, h: (b, 0, h)),
        out_shape=jax.ShapeDtypeStruct((BATCH, SEQ, B_WIDTH), BF16),
        scratch_shapes=[pltpu.VMEM((SEQ, B_HD), F32),
                        pltpu.VMEM((SEQ, LANES), F32),
                        pltpu.VMEM((SEQ, LANES), F32)],
        compiler_params=_params(("parallel", "parallel")),
        name="attn_prompt",
    )(*args)


assert all(B_WINDOWS[g] == B_BAND * B_DILATIONS[g] for g in range(B_GROUPS))
assert PAST_LEN >= max(B_WINDOWS) and B_DILATIONS[0] == 1 and DEC_SEQ <= min(B_DILATIONS[1:])


def _attn_sample_body(q_ref, kvn_ref, c0_ref, c1_ref, c2_ref, o_ref):
    k_heads = slice(0, B_HEADS)
    v_heads = slice(B_HEADS, 2 * B_HEADS)

    def piece(q, k, v, valid=None):
        s = jnp.sum(k * q[None], axis=-1, keepdims=True) * B_SCALE
        if valid is not None:
            s = jnp.where(valid, s, NEG_BIG)
        m = jnp.max(s, axis=0)
        p = jnp.exp(s - m[None])
        return m, jnp.sum(p, axis=0), jnp.sum(p * v, axis=0)

    row0 = lax.broadcasted_iota(jnp.int32, (B_WINDOWS[0], B_HEADS, 1), 0)
    strided = (None, c1_ref, c2_ref)
    for t in range(DEC_SEQ):
        terms = []
        for g in range(B_GROUPS):
            q = q_ref[g, 0, t]
            if g == 0:
                valid = B_WINDOWS[0] + t - row0 <= B_BAND
                terms.append(piece(q, c0_ref[0, :, k_heads, :], c0_ref[0, :, v_heads, :], valid))
                new = slice(0, t + 1)
            else:
                c_ref = strided[g]
                terms.append(piece(q, c_ref[0, :, t, k_heads, :], c_ref[0, :, t, v_heads, :]))
                new = slice(t, t + 1)
            terms.append(piece(q, kvn_ref[g, 0, new, k_heads, :], kvn_ref[g, 0, new, v_heads, :]))
        m_all = terms[0][0]
        for m, _, _ in terms[1:]:
            m_all = jnp.maximum(m_all, m)
        l_all = jnp.zeros((B_HEADS, 1), F32)
        acc_all = jnp.zeros((B_HEADS, B_HD), F32)
        for m, l, acc in terms:
            w = jnp.exp(m - m_all)
            l_all = l_all + l * w
            acc_all = acc_all + acc * w
        o_ref[0, t] = acc_all / l_all
    o_ref[0, DEC_SEQ:] = jnp.zeros((SAMPLE_PAD - DEC_SEQ, B_HEADS, B_HD), F32)


def _attn_sample(q_s, kv_new, caches):
    kv_rows = 2 * B_HEADS

    def by_residue(g):
        dil = B_DILATIONS[g]
        view = caches[g].reshape(DEC_BATCH, B_BAND, dil, kv_rows, B_HD)
        return view, pl.BlockSpec((1, B_BAND, DEC_SEQ, kv_rows, B_HD), lambda b: (b, 0, 0, 0, 0))

    c1, c1_spec = by_residue(1)
    c2, c2_spec = by_residue(2)
    return pl.pallas_call(
        _attn_sample_body,
        grid=(DEC_BATCH,),
        in_specs=[
            pl.BlockSpec((B_GROUPS, 1, SAMPLE_PAD, B_HEADS, B_HD), lambda b: (0, b, 0, 0, 0)),
            pl.BlockSpec((B_GROUPS, 1, SAMPLE_PAD, kv_rows, B_HD), lambda b: (0, b, 0, 0, 0)),
            pl.BlockSpec((1, B_WINDOWS[0], kv_rows, B_HD), lambda b: (b, 0, 0, 0)),
            c1_spec, c2_spec,
        ],
        out_specs=pl.BlockSpec((1, SAMPLE_PAD, B_HEADS, B_HD), lambda b: (b, 0, 0, 0)),
        out_shape=jax.ShapeDtypeStruct((DEC_BATCH, SAMPLE_PAD, B_HEADS, B_HD), F32),
        compiler_params=_params(("parallel",)),
        name="attn_sample",
    )(q_s, kv_new, caches[0], c1, c2)


POOL_ROWS = 256
assert all(w & (w - 1) == 0 for w in POOL_WINDOWS)


def _pool_prompt_body(x_ref, g_ref, w_ref, scale_ref, o_ref, tail_ref, r_ref, sum_ref, z_ref):
    grp = pl.program_id(1)
    n_chunks = SEQ // POOL_ROWS

    @pl.when(grp == 0)
    def _():
        def chunk(rows):
            x = x_ref[0, rows, :]
            r_ref[rows, :] = lax.rsqrt(jnp.mean(x * x, axis=-1, keepdims=True) + EPS)
        _for_row_chunks(SEQ, NORM_ROWS, chunk)

    for gi, w in enumerate(POOL_WINDOWS):
        @pl.when(grp == gi)
        def _(gi=gi, w=w):
            cs = slice(gi * C_GW, (gi + 1) * C_GW)

            def normed(c):
                rows = slice(c * POOL_ROWS, (c + 1) * POOL_ROWS)
                return (x_ref[0, rows, cs] * r_ref[rows, :]) * g_ref[:, cs]

            sum_ref[0:POOL_PAD, :] = jnp.zeros((POOL_PAD, C_GW), F32)
            for c in range(n_chunks):
                h = normed(c)
                sum_ref[POOL_PAD + c * POOL_ROWS:POOL_PAD + (c + 1) * POOL_ROWS, :] = h
                if c == n_chunks - 1:
                    tail_ref[0, :, cs] = h[POOL_ROWS - POOL_PAD:]
            k = 1
            while k < w:
                for c in reversed(range(n_chunks)):
                    r0 = POOL_PAD + c * POOL_ROWS
                    sum_ref[r0:r0 + POOL_ROWS, :] = (sum_ref[r0:r0 + POOL_ROWS, :]
                                                    + sum_ref[r0 - k:r0 - k + POOL_ROWS, :])
                k *= 2
            for c in range(n_chunks):
                r0 = POOL_PAD + c * POOL_ROWS
                pos = c * POOL_ROWS + lax.broadcasted_iota(jnp.int32, (POOL_ROWS, 1), 0)
                cnt = jnp.minimum(w, pos + 1).astype(F32)
                z_ref[c * POOL_ROWS:(c + 1) * POOL_ROWS, :] = (
                    sum_ref[r0:r0 + POOL_ROWS, :] / cnt - normed(c)).astype(BF16)
            y = _dot(z_ref[...], w_ref[...].astype(BF16))
            o_ref[0] = x_ref[0, :, cs] + y * scale_ref[...]


def _pool_prompt(x, g_all, g_layer, w_all, scale_all, layer):
    return pl.pallas_call(
        _pool_prompt_body,
        grid=(BATCH, C_GROUPS),
        in_specs=[pl.BlockSpec((1, SEQ, D_MODEL), lambda b, g: (b, 0, 0)),
                  pl.BlockSpec((None, 1, D_MODEL), lambda b, g: (g_layer, 0, 0)),
                  pl.BlockSpec((None, None, C_GW, C_GW), lambda b, g: (layer, g, 0, 0)),
                  pl.BlockSpec((None, 1, C_GW), lambda b, g: (layer, 0, g))],
        out_specs=[pl.BlockSpec((1, SEQ, C_GW), lambda b, g: (b, 0, g)),
                   pl.BlockSpec((1, POOL_PAD, D_MODEL), lambda b, g: (b, 0, 0))],
        out_shape=[jax.ShapeDtypeStruct((BATCH, SEQ, D_MODEL), F32),
                   jax.ShapeDtypeStruct((BATCH, POOL_PAD, D_MODEL), F32)],
        scratch_shapes=[pltpu.VMEM((SEQ, 1), F32),
                        pltpu.VMEM((POOL_PAD + SEQ, C_GW), F32),
                        pltpu.VMEM((SEQ, C_GW), BF16)],
        compiler_params=_params(("arbitrary", "arbitrary")),
        name="pool_prompt",
    )(x, _layer_vec(g_all), w_all, _layer_vec(scale_all))


def _pool_sample_body(x_ref, g_ref, state_ref, w_ref, scale_ref, o_ref, seq_ref, z_ref):
    for b in range(DEC_BATCH):
        rows = slice(b * SAMPLE_PAD, (b + 1) * SAMPLE_PAD)
        seq_ref[b, 0:POOL_PAD, :] = state_ref[b]
        seq_ref[b, POOL_PAD:, :] = _rms_rows(x_ref[rows, :], g_ref[...])
    for b in range(DEC_BATCH):
        rows = slice(b * SAMPLE_PAD, (b + 1) * SAMPLE_PAD)
        for gi, w in enumerate(POOL_WINDOWS):
            cs = slice(gi * C_GW, (gi + 1) * C_GW)
            cur = seq_ref[b, POOL_PAD:POOL_PAD + SAMPLE_PAD, cs]
            tot = cur
            for k in range(1, w):
                tot = tot + seq_ref[b, POOL_PAD - k:POOL_PAD - k + SAMPLE_PAD, cs]
            pos = PAST_LEN + lax.broadcasted_iota(jnp.int32, (SAMPLE_PAD, 1), 0)
            cnt = jnp.minimum(w, pos + 1).astype(F32)
            z_ref[rows, cs] = (tot / cnt - cur).astype(BF16)
    for gi in range(C_GROUPS):
        cs = slice(gi * C_GW, (gi + 1) * C_GW)
        y = _dot(z_ref[:, cs], w_ref[gi].astype(BF16))
        o_ref[:, cs] = x_ref[:, cs] + y * scale_ref[:, cs]


def _pool_sample(xs, g_all, g_layer, state_pad, w_all, scale_all, layer):
    seq_rows = POOL_PAD + SAMPLE_PAD
    return pl.pallas_call(
        _pool_sample_body,
        grid=(1,),
        in_specs=[
            pl.BlockSpec((MS, D_MODEL), lambda i: (0, 0)),
            pl.BlockSpec((None, 1, D_MODEL), lambda i: (g_layer, 0, 0)),
            pl.BlockSpec((DEC_BATCH, POOL_PAD, D_MODEL), lambda i: (0, 0, 0)),
            pl.BlockSpec((None, C_GROUPS, C_GW, C_GW), lambda i: (layer, 0, 0, 0)),
            pl.BlockSpec((None, 1, D_MODEL), lambda i: (layer, 0, 0)),
        ],
        out_specs=[pl.BlockSpec((MS, D_MODEL), lambda i: (0, 0)),
                   pl.BlockSpec((DEC_BATCH, seq_rows, D_MODEL), lambda i: (0, 0, 0))],
        out_shape=[jax.ShapeDtypeStruct((MS, D_MODEL), F32),
                   jax.ShapeDtypeStruct((DEC_BATCH, seq_rows, D_MODEL), F32)],
        scratch_shapes=[pltpu.VMEM((MS, D_MODEL), BF16)],
        compiler_params=_params(("arbitrary",)),
        name="pool_sample",
    )(xs, _layer_vec(g_all), state_pad, w_all, _layer_vec(scale_all))


def _mixer_a(xp, xs, norm_g, layer, ia, a_w_in, a_ln_g, a_ln_b, a_w_s, a_b_s, a_w_out):
    b_s = a_b_s[ia]
    bias_p = jnp.repeat(jnp.transpose(b_s), A_HD, axis=1)
    yp, zs = _amix_prompt(xp, xs, norm_g, layer, a_w_in, a_ln_g, a_ln_b, a_w_s, bias_p, a_w_out, ia)
    ws_s = jnp.tile(a_w_s[ia][:, :SAMPLE_PAD, :SAMPLE_PAD], (1, DEC_BATCH, DEC_BATCH))
    bias_s = jnp.tile(jnp.repeat(jnp.transpose(b_s[:, :SAMPLE_PAD]), A_HD, axis=1), (DEC_BATCH, 1))
    ys, v_s = _amix_sample(xs, zs, a_ln_g, a_ln_b, ws_s, bias_s, a_w_out, ia)
    return yp, ys, v_s


def _mixer_b(xp, xs, norm_g, layer, ib, caches, b_w_qkv, b_q_g, b_k_g, b_w_out):
    ones = jnp.ones((B_GROUPS, B_WIDTH), F32)
    gain = jnp.stack([jnp.tile(b_q_g[ib], (1, B_HEADS)), jnp.tile(b_k_g[ib], (1, B_HEADS)), ones], axis=1)
    gain = gain.reshape(1, QKV_COLS)
    qkv_head, kv_last, qkv_s = _qkv(xp, xs, norm_g, layer, b_w_qkv, ib, gain)
    qkv_head = qkv_head.reshape(BATCH, SEQ, QKV_HEAD_COLS)
    kv_last = kv_last.reshape(BATCH, SEQ, KV_W)
    op = _attn_prompt(qkv_head, kv_last)
    qkv_s = jnp.transpose(qkv_s.reshape(DEC_BATCH, SAMPLE_PAD, B_GROUPS, 3 * B_HEADS, B_HD), (2, 0, 1, 3, 4))
    q_s = qkv_s[:, :, :, :B_HEADS]
    kv_new = qkv_s[:, :, :, B_HEADS:]
    c = [cc[ib].reshape(DEC_BATCH, cc.shape[2], 2 * B_HEADS, B_HD) for cc in caches]
    os_ = _attn_sample(q_s, kv_new, c)
    yp, ys = _proj_residual(op.reshape(MP, B_WIDTH), os_.reshape(MS, B_WIDTH), b_w_out, ib, xp, xs,
                            name="b_out")
    new_p = []
    for g in range(B_GROUPS):
        keep = min(B_WINDOWS[g], SEQ)
        if g == B_GROUPS - 1:
            kv = kv_last[:, SEQ - keep:]
        else:
            kv = qkv_head[:, SEQ - keep:, (3 * g + 1) * B_WIDTH:(3 * g + 3) * B_WIDTH]
        new_p.append(kv.reshape(BATCH, keep, 2, B_HEADS, B_HD))
    new_s = [kv_new[g, :, :DEC_SEQ].reshape(DEC_BATCH, DEC_SEQ, 2, B_HEADS, B_HD)
             for g in range(B_GROUPS)]
    return yp, ys, new_p, new_s


def _mixer_c(xp, xs, norm_g, layer, ic, state, c_w, c_scale):
    yp, tail = _pool_prompt(xp.reshape(BATCH, SEQ, D_MODEL), norm_g, layer, c_w, c_scale, ic)
    yp = yp.reshape(MP, D_MODEL)
    state_pad = jnp.pad(state[ic], ((0, 0), (POOL_PAD - POOL_STATE, 0), (0, 0)))
    ys, seq = _pool_sample(xs, norm_g, layer, state_pad, c_w, c_scale, ic)
    pool_p = tail[:, POOL_PAD - POOL_STATE:]
    first = POOL_PAD + DEC_SEQ - POOL_STATE
    pool_s = seq[:, first:first + POOL_STATE]
    return yp, ys, pool_p, pool_s


def kernel(x_prompt, x_sample, cache_b_kv0, cache_b_kv1, cache_b_kv2, state_c_pool, norm_mix_g, norm_ffn_g, a_w_in, a_ln_g, a_ln_b, a_w_s, a_b_s, a_w_out, b_w_qkv, b_q_g, b_k_g, b_w_out, c_w, c_scale, ffn_w1, ffn_w3, ffn_w2):
    xp = x_prompt.reshape(MP, D_MODEL)
    xs = jnp.pad(x_sample, ((0, 0), (0, SAMPLE_PAD - DEC_SEQ), (0, 0))).reshape(MS, D_MODEL)
    a_v_s, pool_p, pool_s = [], [], []
    kv_p = [[] for _ in range(B_GROUPS)]
    kv_s = [[] for _ in range(B_GROUPS)]
    ia = ib = ic = 0
    for layer in range(DEPTH):
        kind = layer % N_MIXERS
        if kind == 0:
            xp, xs, v_s = _mixer_a(xp, xs, norm_mix_g, layer, ia, a_w_in, a_ln_g, a_ln_b, a_w_s, a_b_s, a_w_out)
            a_v_s.append(v_s.reshape(DEC_BATCH, SAMPLE_PAD, A_WIDTH)[:, :DEC_SEQ])
            ia += 1
        elif kind == 1:
            caches = (cache_b_kv0, cache_b_kv1, cache_b_kv2)
            xp, xs, kvp, kvs = _mixer_b(xp, xs, norm_mix_g, layer, ib, caches, b_w_qkv, b_q_g, b_k_g, b_w_out)
            for g in range(B_GROUPS):
                kv_p[g].append(kvp[g])
                kv_s[g].append(kvs[g])
            ib += 1
        else:
            xp, xs, pp, ps = _mixer_c(xp, xs, norm_mix_g, layer, ic, state_c_pool, c_w, c_scale)
            pool_p.append(pp)
            pool_s.append(ps)
            ic += 1
        xp, xs = _ffn(xp, xs, norm_ffn_g, ffn_w1, ffn_w3, ffn_w2, layer)
    y_prompt = xp.reshape(BATCH, SEQ, D_MODEL)
    y_sample = xs.reshape(DEC_BATCH, SAMPLE_PAD, D_MODEL)[:, :DEC_SEQ]
    return (y_prompt, y_sample, jnp.stack(a_v_s),
            jnp.stack(kv_p[0]), jnp.stack(kv_p[1]), jnp.stack(kv_p[2]),
            jnp.stack(kv_s[0]), jnp.stack(kv_s[1]), jnp.stack(kv_s[2]),
            jnp.stack(pool_p), jnp.stack(pool_s))
```

```python
import jax
import jax.numpy as jnp
from jax import lax
from jax.experimental import pallas as pl
from jax.experimental.pallas import tpu as pltpu

F32 = jnp.float32
BF16 = jnp.bfloat16

D_MODEL = 2048
BATCH = 4
SEQ = 2048
DEPTH = 4
DEC_BATCH = 8
DEC_SEQ = 4
PAST_LEN = 16384
N_MIXERS = 3
A_CHUNK = 128
A_WIDTH = D_MODEL
A_HEADS = 16
A_HD = A_WIDTH // A_HEADS
B_WINDOWS = (128, 512, 2048)
B_DILATIONS = (1, 4, 16)
B_GROUPS = 3
B_HD = 128
B_HEADS = D_MODEL // B_HD
B_WIDTH = B_HEADS * B_HD
B_SCALE = B_HD ** -0.5
B_BAND = 128
POOL_WINDOWS = (2, 4, 8, 16)
C_GROUPS = 4
C_GW = D_MODEL // C_GROUPS
POOL_STATE = max(POOL_WINDOWS) - 1
POOL_PAD = POOL_STATE + 1
D_FF = ((8 * D_MODEL + 3 * 256 - 1) // (3 * 256)) * 256
EPS = 1e-6

SUBLANES = 8
LANES = 128
VMEM_LIMIT_BYTES = 56 * 1024 * 1024

SAMPLE_PAD = SUBLANES
MP = BATCH * SEQ
MS = DEC_BATCH * SAMPLE_PAD
NEG_BIG = -1e30

TM = 1024
TN = 512
TF_FFN = 256
NORM_ROWS = 256


def _params(semantics):
    return pltpu.CompilerParams(dimension_semantics=semantics,
                                vmem_limit_bytes=VMEM_LIMIT_BYTES)


def _dot(a, b):
    return jnp.dot(a, b, preferred_element_type=F32)


def _dot_nt(a, b):
    return lax.dot_general(a, b, (((1,), (1,)), ((), ())), preferred_element_type=F32)


def _rms_rows(x, g):
    r = lax.rsqrt(jnp.mean(x * x, axis=-1, keepdims=True) + EPS)
    return (x * r) * g


def _for_row_chunks(rows, chunk, fn):
    chunk = min(chunk, rows)
    n = rows // chunk
    if n == 1:
        fn(pl.ds(0, chunk))
        return

    def body(c, carry):
        fn(pl.ds(pl.multiple_of(c * chunk, chunk), chunk))
        return carry

    lax.fori_loop(0, n, body, 0)


def _layer_vec(stacked):
    return stacked.reshape(stacked.shape[0], 1, stacked.shape[1])


def _hold_after_first_tile(col_fn, last):
    return lambda i, j: (0, jnp.where(i == 0, col_fn(j), last))


def _ffn_body(x_ref, xs_ref, g_ref, w1_ref, w3_ref, w2_ref, o_ref, os_ref, hn_ref):
    i = pl.program_id(0)
    j = pl.program_id(1)
    tm = x_ref.shape[0]

    def gate(h, w1, w3):
        return (jax.nn.silu(_dot(h, w1)) * _dot(h, w3)).astype(BF16)

    def first_step(with_sample):
        w1 = w1_ref[...].astype(BF16)
        w3 = w3_ref[...].astype(BF16)
        w2 = w2_ref[...].astype(BF16)
        for c in range(tm // NORM_ROWS):
            rows = slice(c * NORM_ROWS, (c + 1) * NORM_ROWS)
            x = x_ref[rows, :]
            h = _rms_rows(x, g_ref[...]).astype(BF16)
            hn_ref[rows, :] = h
            o_ref[rows, :] = x + _dot(gate(h, w1, w3), w2)
        if with_sample:
            xs = xs_ref[...]
            h = _rms_rows(xs, g_ref[...]).astype(BF16)
            hn_ref[tm:, :] = h
            os_ref[...] = xs + _dot(gate(h, w1, w3), w2)

    def later_step(with_sample):
        w1 = w1_ref[...].astype(BF16)
        w3 = w3_ref[...].astype(BF16)
        w2 = w2_ref[...].astype(BF16)
        if with_sample:
            gt = gate(hn_ref[...], w1, w3)
            o_ref[...] += _dot(gt[:tm], w2)
            os_ref[...] += _dot(gt[tm:], w2)
        else:
            o_ref[...] += _dot(gate(hn_ref[0:tm, :], w1, w3), w2)

    for with_sample, tile_cond in ((True, i == 0), (False, i != 0)):
        for step_fn, col_cond in ((first_step, j == 0), (later_step, j != 0)):
            @pl.when(tile_cond & col_cond)
            def _(step_fn=step_fn, with_sample=with_sample):
                step_fn(with_sample)


def _ffn(x, xs, g_all, w1_all, w3_all, w2_all, layer):
    return pl.pallas_call(
        _ffn_body,
        grid=(MP // TM, D_FF // TF_FFN),
        in_specs=[
            pl.BlockSpec((TM, D_MODEL), lambda i, j: (i, 0)),
            pl.BlockSpec((MS, D_MODEL), lambda i, j: (0, 0)),
            pl.BlockSpec((None, 1, D_MODEL), lambda i, j: (layer, 0, 0)),
            pl.BlockSpec((None, D_MODEL, TF_FFN), lambda i, j: (layer, 0, j)),
            pl.BlockSpec((None, D_MODEL, TF_FFN), lambda i, j: (layer, 0, j)),
            pl.BlockSpec((None, TF_FFN, D_MODEL), lambda i, j: (layer, j, 0)),
        ],
        out_specs=[pl.BlockSpec((TM, D_MODEL), lambda i, j: (i, 0)),
                   pl.BlockSpec((MS, D_MODEL), lambda i, j: (0, 0))],
        out_shape=[jax.ShapeDtypeStruct((MP, D_MODEL), F32),
                   jax.ShapeDtypeStruct((MS, D_MODEL), F32)],
        scratch_shapes=[pltpu.VMEM((TM + MS, D_MODEL), BF16)],
        compiler_params=_params(("arbitrary", "arbitrary")),
        name="ffn",
    )(x, xs, _layer_vec(g_all), w1_all, w3_all, w2_all)


def _fill_lhs(h_ref, x_ref, xs_ref, g_ref, i, j):
    tm = x_ref.shape[0]

    def prep(x):
        if g_ref is not None:
            x = _rms_rows(x, g_ref[...])
        return x.astype(BF16)

    @pl.when(j == 0)
    def _():
        def chunk(rows):
            h_ref[rows, :] = prep(x_ref[rows, :])
        _for_row_chunks(tm, NORM_ROWS, chunk)

    @pl.when((i == 0) & (j == 0))
    def _():
        h_ref[tm:, :] = prep(xs_ref[...])


def _proj_residual_body(x_ref, xs_ref, w_ref, res_ref, res_s_ref, o_ref, os_ref, h_ref):
    i = pl.program_id(0)
    j = pl.program_id(1)
    tm = x_ref.shape[0]
    _fill_lhs(h_ref, x_ref, xs_ref, None, i, j)
    cols = pl.ds(pl.multiple_of(j * TN, TN), TN)

    @pl.when(i == 0)
    def _():
        acc = _dot(h_ref[...], w_ref[:, cols].astype(BF16))
        o_ref[...] = res_ref[...] + acc[:tm]
        os_ref[...] = res_s_ref[...] + acc[tm:]

    @pl.when(i != 0)
    def _():
        o_ref[...] = res_ref[...] + _dot(h_ref[0:tm, :], w_ref[:, cols].astype(BF16))


def _proj_residual(x, xs, w_all, layer, res, res_s, *, name):
    k = x.shape[1]
    n = w_all.shape[2]
    nj = n // TN
    return pl.pallas_call(
        _proj_residual_body,
        grid=(MP // TM, nj),
        in_specs=[pl.BlockSpec((TM, k), lambda i, j: (i, 0)),
                  pl.BlockSpec((MS, k), lambda i, j: (0, 0)),
                  pl.BlockSpec((None, k, n), lambda i, j: (layer, 0, 0), pipeline_mode=pl.Buffered(1)),
                  pl.BlockSpec((TM, TN), lambda i, j: (i, j)),
                  pl.BlockSpec((MS, TN), lambda i, j: (0, j))],
        out_specs=[pl.BlockSpec((TM, TN), lambda i, j: (i, j)),
                   pl.BlockSpec((MS, TN), _hold_after_first_tile(lambda j: j, nj - 1))],
        out_shape=[jax.ShapeDtypeStruct((MP, n), F32),
                   jax.ShapeDtypeStruct((MS, n), F32)],
        scratch_shapes=[pltpu.VMEM((TM + MS, k), BF16)],
        compiler_params=_params(("arbitrary", "arbitrary")),
        name=name,
    )(x, xs, w_all, res, res_s)


TN_QKV = 1024
QKV_TILES = B_WIDTH // TN_QKV
QKV_COLS = B_GROUPS * 3 * B_WIDTH
KV_W = 2 * B_WIDTH
QKV_HEAD_COLS = QKV_COLS - KV_W
MXU_COLS = 256
QKV_NJ = QKV_COLS // TN_QKV
QKV_LAST_KV = QKV_HEAD_COLS // TN_QKV


def _kv_tail_rows(g):
    keep = min(B_WINDOWS[g], SEQ)
    assert keep <= TM and SEQ % TM == 0
    return keep, (3 * g + 1) * QKV_TILES


def _qkv_body(x_ref, xs_ref, g_ref, w_ref, gain_ref, o_ref, olast_ref, os_ref, *rest):
    tail_refs, h_ref = rest[:-1], rest[-1]
    i = pl.program_id(0)
    j = pl.program_id(1)
    tm = x_ref.shape[0]
    _fill_lhs(h_ref, x_ref, xs_ref, g_ref, i, j)
    is_v = (j // QKV_TILES) % 3 == 2
    in_last = j >= QKV_LAST_KV

    def store_heads(acc, out_ref, c0):
        for hh in range(MXU_COLS // B_HD):
            a = acc[:, hh * B_HD:(hh + 1) * B_HD]
            cs = slice(c0 + hh * B_HD, c0 + (hh + 1) * B_HD)
            r = lax.rsqrt(jnp.mean(a * a, axis=-1, keepdims=True) + EPS)
            r = jnp.where(is_v, 1.0, r)
            out_ref[:, cs] = (a * r) * gain_ref[:, cs]

    def run(out_ref, with_sample):
        h = h_ref[...] if with_sample else h_ref[0:tm, :]
        for c0 in range(0, TN_QKV, MXU_COLS):
            acc = _dot(h, w_ref[:, c0:c0 + MXU_COLS].astype(BF16))
            if with_sample:
                store_heads(acc[:tm], out_ref, c0)
                store_heads(acc[tm:], os_ref, c0)
            else:
                store_heads(acc, out_ref, c0)

    for with_sample, tile_cond in ((True, i == 0), (False, i != 0)):
        for out_ref, dest_cond in ((o_ref, jnp.logical_not(in_last)), (olast_ref, in_last)):
            @pl.when(tile_cond & dest_cond)
            def _(out_ref=out_ref, with_sample=with_sample):
                run(out_ref, with_sample)

    last_tile_of_batch = (i + 1) % (SEQ // TM) == 0
    for g, tail_ref in enumerate(tail_refs):
        keep, first = _kv_tail_rows(g)

        @pl.when(last_tile_of_batch & (j >= first) & (j < first + 2 * QKV_TILES))
        def _(tail_ref=tail_ref, keep=keep):
            tail_ref[0] = o_ref[tm - keep:tm, :]


def _qkv(x, xs, g_all, g_layer, w_all, layer, gain):
    def out_map(i, j):
        return i, jnp.minimum(j, QKV_LAST_KV - 1)

    def out_last_map(i, j):
        return i, jnp.maximum(j - QKV_LAST_KV, 0)

    tiles_per_batch = SEQ // TM

    def tail_spec(g):
        keep, first = _kv_tail_rows(g)

        def index(i, j):
            on_last = (i + 1) % tiles_per_batch == 0
            return i // tiles_per_batch, 0, jnp.where(on_last, jnp.clip(j - first, 0, 2 * QKV_TILES - 1), 0)

        return pl.BlockSpec((1, keep, TN_QKV), index, pipeline_mode=pl.Buffered(1))

    tail_groups = range(B_GROUPS - 1)

    return pl.pallas_call(
        _qkv_body,
        grid=(MP // TM, QKV_NJ),
        in_specs=[
            pl.BlockSpec((TM, D_MODEL), lambda i, j: (i, 0), pipeline_mode=pl.Buffered(1)),
            pl.BlockSpec((MS, D_MODEL), lambda i, j: (0, 0)),
            pl.BlockSpec((None, 1, D_MODEL), lambda i, j: (g_layer, 0, 0)),
            pl.BlockSpec((None, D_MODEL, TN_QKV), lambda i, j: (layer, 0, j)),
            pl.BlockSpec((1, TN_QKV), lambda i, j: (0, j)),
        ],
        out_specs=[pl.BlockSpec((TM, TN_QKV), out_map),
                   pl.BlockSpec((TM, TN_QKV), out_last_map),
                   pl.BlockSpec((MS, TN_QKV), _hold_after_first_tile(lambda j: j, QKV_NJ - 1)),
                   *[tail_spec(g) for g in tail_groups]],
        out_shape=[jax.ShapeDtypeStruct((MP, QKV_HEAD_COLS), F32),
                   jax.ShapeDtypeStruct((MP, KV_W), F32),
                   jax.ShapeDtypeStruct((MS, QKV_COLS), F32),
                   *[jax.ShapeDtypeStruct((BATCH, _kv_tail_rows(g)[0], KV_W), F32) for g in tail_groups]],
        scratch_shapes=[pltpu.VMEM((TM + MS, D_MODEL), BF16)],
        compiler_params=_params(("arbitrary", "arbitrary")),
        name="qkv",
    )(x, xs, _layer_vec(g_all), w_all, gain)


def _mixing_weights(wsm_ref, ws_ref, t_len, block):
    r = lax.broadcasted_iota(jnp.int32, (t_len, t_len), 0)
    c = lax.broadcasted_iota(jnp.int32, (t_len, t_len), 1)
    keep = c <= r
    if block < t_len:
        keep = keep & ((r // block) == (c // block)) & ((c % block) < DEC_SEQ)
    for h in range(A_HEADS):
        wsm_ref[h] = jnp.where(keep, ws_ref[h], 0.0).astype(BF16)


TN_A_IN = 1024
TN_A_OUT = 256
A_UV = A_WIDTH // TN_A_IN
A_P1 = 2 * A_UV
A_P2 = D_MODEL // TN_A_OUT


def _amix_prompt_body(x_ref, xs_ref, g_ref, win_ref, lng_ref, lnb_ref, ws_ref, bias_ref, wout_ref,
                      o_ref, zs_ref, h_ref, u_ref, v_ref, wsm_ref):
    i = pl.program_id(0)
    j = pl.program_id(1)
    tm = x_ref.shape[0]
    _fill_lhs(h_ref, x_ref, xs_ref, g_ref, i, j)

    def phase1(with_sample, dst_ref, slot):
        h = h_ref[...] if with_sample else h_ref[0:tm, :]
        for c0 in range(0, TN_A_IN, MXU_COLS):
            cs = slice(c0, c0 + MXU_COLS)
            z = jax.nn.gelu(_dot(h, win_ref[:, cs].astype(BF16)), approximate=True)
            if with_sample:
                dst_ref[slot, :, cs] = z[:tm].astype(dst_ref.dtype)
                zs_ref[:, cs] = z[tm:]
            else:
                dst_ref[slot, :, cs] = z.astype(dst_ref.dtype)

    for with_sample, tile_cond in ((True, i == 0), (False, i != 0)):
        for dst_ref, half_cond, slot in ((u_ref, j < A_UV, j),
                                         (v_ref, (j >= A_UV) & (j < A_P1), j - A_UV)):
            @pl.when(tile_cond & half_cond)
            def _(with_sample=with_sample, dst_ref=dst_ref, slot=slot):
                phase1(with_sample, dst_ref, slot)

    @pl.when(j == A_P1)
    def _():
        _mixing_weights(wsm_ref, ws_ref, A_CHUNK, A_CHUNK)

        def chunk(rows):
            vs = [v_ref[k, rows, :] for k in range(A_UV)]
            mu = sum(jnp.sum(vk, axis=-1, keepdims=True) for vk in vs) / A_WIDTH
            ds = [vk - mu for vk in vs]
            var = sum(jnp.sum(dk * dk, axis=-1, keepdims=True) for dk in ds) / A_WIDTH
            r = lax.rsqrt(var + EPS)
            for h in range(A_HEADS):
                k, c = divmod(h * A_HD, TN_A_IN)
                cs = slice(h * A_HD, (h + 1) * A_HD)
                vln = (ds[k][:, c:c + A_HD] * r) * lng_ref[:, cs] + lnb_ref[:, cs]
                mixed = _dot(wsm_ref[h], vln.astype(BF16)) + bias_ref[:, cs]
                h_ref[rows, cs] = (u_ref[k, rows, c:c + A_HD] * mixed).astype(BF16)
        _for_row_chunks(tm, A_CHUNK, chunk)

    @pl.when(j >= A_P1)
    def _():
        cols = pl.ds(pl.multiple_of((j - A_P1) * TN_A_OUT, TN_A_OUT), TN_A_OUT)
        o_ref[...] = x_ref[:, cols] + _dot(h_ref[0:tm, :], wout_ref[...].astype(BF16))


def _amix_prompt(x, xs, g_all, g_layer, win_all, lng_all, lnb_all, ws_all, bias, wout_all, layer):
    def out_col(j):
        return jnp.maximum(j - A_P1, 0)

    def z_col(j):
        return jnp.minimum(j, A_P1 - 1)

    return pl.pallas_call(
        _amix_prompt_body,
        grid=(MP // TM, A_P1 + A_P2),
        in_specs=[
            pl.BlockSpec((TM, D_MODEL), lambda i, j: (i, 0), pipeline_mode=pl.Buffered(1)),
            pl.BlockSpec((MS, D_MODEL), lambda i, j: (0, 0)),
            pl.BlockSpec((None, 1, D_MODEL), lambda i, j: (g_layer, 0, 0)),
            pl.BlockSpec((None, D_MODEL, TN_A_IN), lambda i, j: (layer, 0, z_col(j))),
            pl.BlockSpec((None, 1, A_WIDTH), lambda i, j: (layer, 0, 0)),
            pl.BlockSpec((None, 1, A_WIDTH), lambda i, j: (layer, 0, 0)),
            pl.BlockSpec((None, A_HEADS, A_CHUNK, A_CHUNK), lambda i, j: (layer, 0, 0, 0)),
            pl.BlockSpec((A_CHUNK, A_WIDTH), lambda i, j: (0, 0)),
            pl.BlockSpec((None, A_WIDTH, TN_A_OUT), lambda i, j: (layer, 0, out_col(j))),
        ],
        out_specs=[pl.BlockSpec((TM, TN_A_OUT), lambda i, j: (i, out_col(j))),
                   pl.BlockSpec((MS, TN_A_IN), _hold_after_first_tile(z_col, A_P1 - 1))],
        out_shape=[jax.ShapeDtypeStruct((MP, D_MODEL), F32),
                   jax.ShapeDtypeStruct((MS, 2 * A_WIDTH), F32)],
        scratch_shapes=[pltpu.VMEM((TM + MS, D_MODEL), BF16),
                        pltpu.VMEM((A_UV, TM, TN_A_IN), BF16),
                        pltpu.VMEM((A_UV, TM, TN_A_IN), F32),
                        pltpu.VMEM((A_HEADS, A_CHUNK, A_CHUNK), BF16)],
        compiler_params=_params(("arbitrary", "arbitrary")),
        name="amix_prompt",
    )(x, xs, _layer_vec(g_all), win_all, _layer_vec(lng_all), _layer_vec(lnb_all), ws_all, bias,
      wout_all)


def _amix_sample_body(x_ref, u_ref, v_ref, lng_ref, lnb_ref, ws_ref, bias_ref, wout_ref,
                      o_ref, vout_ref, gated_ref, wsm_ref):
    j = pl.program_id(0)

    @pl.when(j == 0)
    def _():
        _mixing_weights(wsm_ref, ws_ref, MS, SAMPLE_PAD)
        v = v_ref[...]
        mu = jnp.mean(v, axis=-1, keepdims=True)
        d = v - mu
        var = jnp.mean(d * d, axis=-1, keepdims=True)
        vln = (d * lax.rsqrt(var + EPS)) * lng_ref[...] + lnb_ref[...]
        vout_ref[...] = vln
        vb = vln.astype(BF16)
        for h in range(A_HEADS):
            cs = slice(h * A_HD, (h + 1) * A_HD)
            mixed = _dot(wsm_ref[h], vb[:, cs]) + bias_ref[:, cs]
            gated_ref[:, cs] = (u_ref[:, cs] * mixed).astype(BF16)

    o_ref[...] = x_ref[...] + _dot(gated_ref[...], wout_ref[...].astype(BF16))


def _amix_sample(xs, zs, lng_all, lnb_all, ws_s, bias_s, wout_all, layer):
    return pl.pallas_call(
        _amix_sample_body,
        grid=(D_MODEL // TN,),
        in_specs=[
            pl.BlockSpec((MS, TN), lambda j: (0, j)),
            pl.BlockSpec((MS, A_WIDTH), lambda j: (0, 0)),
            pl.BlockSpec((MS, A_WIDTH), lambda j: (0, 1)),
            pl.BlockSpec((None, 1, A_WIDTH), lambda j: (layer, 0, 0)),
            pl.BlockSpec((None, 1, A_WIDTH), lambda j: (layer, 0, 0)),
            pl.BlockSpec((A_HEADS, MS, MS), lambda j: (0, 0, 0)),
            pl.BlockSpec((MS, A_WIDTH), lambda j: (0, 0)),
            pl.BlockSpec((None, A_WIDTH, TN), lambda j: (layer, 0, j)),
        ],
        out_specs=[pl.BlockSpec((MS, TN), lambda j: (0, j)),
                   pl.BlockSpec((MS, A_WIDTH), lambda j: (0, 0))],
        out_shape=[jax.ShapeDtypeStruct((MS, D_MODEL), F32),
                   jax.ShapeDtypeStruct((MS, A_WIDTH), F32)],
        scratch_shapes=[pltpu.VMEM((MS, A_WIDTH), BF16),
                        pltpu.VMEM((A_HEADS, MS, MS), BF16)],
        compiler_params=_params(("arbitrary",)),
        name="amix_sample",
    )(xs, zs, zs, _layer_vec(lng_all), _layer_vec(lnb_all), ws_s, bias_s, wout_all)


ATTN_HEADS = 1


def _attn_prompt_body(q0, k0, v0, q1, k1, v1, q2, k2, v2, o_ref, acc_ref, m_ref, l_ref):
    groups = ((q0, k0, v0), (q1, k1, v1), (q2, k2, v2))
    n = B_BAND
    qi = lax.broadcasted_iota(jnp.int32, (n, 2 * n), 0)
    kj = lax.broadcasted_iota(jnp.int32, (n, 2 * n), 1)
    dist = n + qi - kj
    band_mask = (dist >= 0) & (dist <= n)
    qi1 = lax.broadcasted_iota(jnp.int32, (n, n), 0)
    kj1 = lax.broadcasted_iota(jnp.int32, (n, n), 1)
    causal_mask = kj1 <= qi1

    for hh in range(ATTN_HEADS):
        cs = slice(hh * B_HD, (hh + 1) * B_HD)
        for step, g in enumerate(range(B_GROUPS)):
            q_ref, k_ref, v_ref = groups[g]
            dil = B_DILATIONS[g]
            n_blocks = SEQ // dil // n
            for r in range(dil):
                for c in range(n_blocks):
                    start = r + c * n * dil
                    if dil == 1:
                        rows_q = pl.ds(start, n)
                    else:
                        rows_q = pl.ds(start, n, stride=dil)
                    if c == 0:
                        rows_k, mask = rows_q, causal_mask
                    elif dil == 1:
                        rows_k, mask = pl.ds(start - n, 2 * n), band_mask
                    else:
                        rows_k, mask = pl.ds(start - n * dil, 2 * n, stride=dil), band_mask
                    q = (q_ref[0, rows_q, cs] * B_SCALE).astype(BF16)
                    k = k_ref[0, rows_k, cs].astype(BF16)
                    v = v_ref[0, rows_k, cs].astype(BF16)
                    s = jnp.where(mask, _dot_nt(q, k), NEG_BIG)
                    m = jnp.max(s, axis=-1, keepdims=True)
                    p = jnp.exp(s - m)
                    pv = _dot(p.astype(BF16), jnp.concatenate([v, jnp.ones_like(v)], axis=1))
                    acc = pv[:, :B_HD]
                    l = pv[:, B_HD:]
                    if step == 0:
                        acc_ref[rows_q, :] = acc
                        m_ref[rows_q, :] = jnp.broadcast_to(m, (n, LANES))
                        l_ref[rows_q, :] = l
                    else:
                        m_old = m_ref[rows_q, :]
                        m_new = jnp.maximum(m_old, m)
                        a_old = jnp.exp(m_old - m_new)
                        a_new = jnp.exp(m - m_new)
                        acc_new = acc_ref[rows_q, :] * a_old + acc * a_new
                        l_new = l_ref[rows_q, :] * a_old + l * a_new
                        if step == B_GROUPS - 1:
                            acc_ref[rows_q, :] = acc_new / l_new
                        else:
                            acc_ref[rows_q, :] = acc_new
                            l_ref[rows_q, :] = l_new
                            m_ref[rows_q, :] = m_new
        o_ref[0, :, cs] = acc_ref[...].astype(o_ref.dtype)


def _attn_prompt(qkv_head, kv_last):
    hw = ATTN_HEADS * B_HD

    def spec(first_col):
        return pl.BlockSpec((1, SEQ, hw), lambda b, h: (b, 0, first_col // hw + h))

    in_specs, args = [], []
    for g in range(B_GROUPS):
        c0 = g * 3 * B_WIDTH
        if g < B_GROUPS - 1:
            in_specs += [spec(c0), spec(c0 + B_WIDTH), spec(c0 + 2 * B_WIDTH)]
            args += [qkv_head] * 3
        else:
            in_specs += [spec(c0), spec(0), spec(B_WIDTH)]
            args += [qkv_head, kv_last, kv_last]
    return pl.pallas_call(
        _attn_prompt_body,
        grid=(BATCH, B_HEADS // ATTN_HEADS),
        in_specs=in_specs,
        out_specs=pl.BlockSpec((1, SEQ, hw), lambda b, h: (b, 0, h)),
        out_shape=jax.ShapeDtypeStruct((BATCH, SEQ, B_WIDTH), BF16),
        scratch_shapes=[pltpu.VMEM((SEQ, B_HD), F32),
                        pltpu.VMEM((SEQ, LANES), F32),
                        pltpu.VMEM((SEQ, LANES), F32)],
        compiler_params=_params(("parallel", "parallel")),
        name="attn_prompt",
    )(*args)


assert all(B_WINDOWS[g] == B_BAND * B_DILATIONS[g] for g in range(B_GROUPS))
assert PAST_LEN >= max(B_WINDOWS) and B_DILATIONS[0] == 1 and DEC_SEQ <= min(B_DILATIONS[1:])


def _attn_sample_body(q_ref, kvn_ref, c0_ref, c1_ref, c2_ref, o_ref):
    k_heads = slice(0, B_HEADS)
    v_heads = slice(B_HEADS, 2 * B_HEADS)

    def piece(q, k, v, valid=None):
        s = jnp.sum(k * q[None], axis=-1, keepdims=True) * B_SCALE
        if valid is not None:
            s = jnp.where(valid, s, NEG_BIG)
        m = jnp.max(s, axis=0)
        p = jnp.exp(s - m[None])
        return m, jnp.sum(p, axis=0), jnp.sum(p * v, axis=0)

    row0 = lax.broadcasted_iota(jnp.int32, (B_WINDOWS[0], B_HEADS, 1), 0)
    strided = (None, c1_ref, c2_ref)
    for t in range(DEC_SEQ):
        terms = []
        for g in range(B_GROUPS):
            q = q_ref[g, 0, t]
            if g == 0:
                valid = B_WINDOWS[0] + t - row0 <= B_BAND
                terms.append(piece(q, c0_ref[0, :, k_heads, :], c0_ref[0, :, v_heads, :], valid))
                new = slice(0, t + 1)
            else:
                c_ref = strided[g]
                terms.append(piece(q, c_ref[0, :, t, k_heads, :], c_ref[0, :, t, v_heads, :]))
                new = slice(t, t + 1)
            terms.append(piece(q, kvn_ref[g, 0, new, k_heads, :], kvn_ref[g, 0, new, v_heads, :]))
        m_all = terms[0][0]
        for m, _, _ in terms[1:]:
            m_all = jnp.maximum(m_all, m)
        l_all = jnp.zeros((B_HEADS, 1), F32)
        acc_all = jnp.zeros((B_HEADS, B_HD), F32)
        for m, l, acc in terms:
            w = jnp.exp(m - m_all)
            l_all = l_all + l * w
            acc_all = acc_all + acc * w
        o_ref[0, t] = acc_all / l_all
    o_ref[0, DEC_SEQ:] = jnp.zeros((SAMPLE_PAD - DEC_SEQ, B_HEADS, B_HD), F32)


def _attn_sample(q_s, kv_new, caches):
    kv_rows = 2 * B_HEADS

    def by_residue(g):
        dil = B_DILATIONS[g]
        view = caches[g].reshape(DEC_BATCH, B_BAND, dil, kv_rows, B_HD)
        return view, pl.BlockSpec((1, B_BAND, DEC_SEQ, kv_rows, B_HD), lambda b: (b, 0, 0, 0, 0))

    c1, c1_spec = by_residue(1)
    c2, c2_spec = by_residue(2)
    return pl.pallas_call(
        _attn_sample_body,
        grid=(DEC_BATCH,),
        in_specs=[
            pl.BlockSpec((B_GROUPS, 1, SAMPLE_PAD, B_HEADS, B_HD), lambda b: (0, b, 0, 0, 0)),
            pl.BlockSpec((B_GROUPS, 1, SAMPLE_PAD, kv_rows, B_HD), lambda b: (0, b, 0, 0, 0)),
            pl.BlockSpec((1, B_WINDOWS[0], kv_rows, B_HD), lambda b: (b, 0, 0, 0)),
            c1_spec, c2_spec,
        ],
        out_specs=pl.BlockSpec((1, SAMPLE_PAD, B_HEADS, B_HD), lambda b: (b, 0, 0, 0)),
        out_shape=jax.ShapeDtypeStruct((DEC_BATCH, SAMPLE_PAD, B_HEADS, B_HD), F32),
        compiler_params=_params(("parallel",)),
        name="attn_sample",
    )(q_s, kv_new, caches[0], c1, c2)


POOL_ROWS = 256
assert all(w & (w - 1) == 0 for w in POOL_WINDOWS)


def _pool_prompt_body(x_ref, g_ref, w_ref, scale_ref, o_ref, tail_ref, r_ref, sum_ref, z_ref):
    grp = pl.program_id(1)
    n_chunks = SEQ // POOL_ROWS

    @pl.when(grp == 0)
    def _():
        def chunk(rows):
            x = x_ref[0, rows, :]
            r_ref[rows, :] = lax.rsqrt(jnp.mean(x * x, axis=-1, keepdims=True) + EPS)
        _for_row_chunks(SEQ, NORM_ROWS, chunk)

    for gi, w in enumerate(POOL_WINDOWS):
        @pl.when(grp == gi)
        def _(gi=gi, w=w):
            cs = slice(gi * C_GW, (gi + 1) * C_GW)

            def normed(c):
                rows = slice(c * POOL_ROWS, (c + 1) * POOL_ROWS)
                return (x_ref[0, rows, cs] * r_ref[rows, :]) * g_ref[:, cs]

            sum_ref[0:POOL_PAD, :] = jnp.zeros((POOL_PAD, C_GW), F32)
            for c in range(n_chunks):
                h = normed(c)
                sum_ref[POOL_PAD + c * POOL_ROWS:POOL_PAD + (c + 1) * POOL_ROWS, :] = h
                if c == n_chunks - 1:
                    tail_ref[0, :, cs] = h[POOL_ROWS - POOL_PAD:]
            k = 1
            while k < w:
                for c in reversed(range(n_chunks)):
                    r0 = POOL_PAD + c * POOL_ROWS
                    sum_ref[r0:r0 + POOL_ROWS, :] = (sum_ref[r0:r0 + POOL_ROWS, :]
                                                    + sum_ref[r0 - k:r0 - k + POOL_ROWS, :])
                k *= 2
            for c in range(n_chunks):
                r0 = POOL_PAD + c * POOL_ROWS
                pos = c * POOL_ROWS + lax.broadcasted_iota(jnp.int32, (POOL_ROWS, 1), 0)
                cnt = jnp.minimum(w, pos + 1).astype(F32)
                z_ref[c * POOL_ROWS:(c + 1) * POOL_ROWS, :] = (
                    sum_ref[r0:r0 + POOL_ROWS, :] / cnt - normed(c)).astype(BF16)
            y = _dot(z_ref[...], w_ref[...].astype(BF16))
            o_ref[0] = x_ref[0, :, cs] + y * scale_ref[...]


def _pool_prompt(x, g_all, g_layer, w_all, scale_all, layer):
    return pl.pallas_call(
        _pool_prompt_body,
        grid=(BATCH, C_GROUPS),
        in_specs=[pl.BlockSpec((1, SEQ, D_MODEL), lambda b, g: (b, 0, 0)),
                  pl.BlockSpec((None, 1, D_MODEL), lambda b, g: (g_layer, 0, 0)),
                  pl.BlockSpec((None, None, C_GW, C_GW), lambda b, g: (layer, g, 0, 0)),
                  pl.BlockSpec((None, 1, C_GW), lambda b, g: (layer, 0, g))],
        out_specs=[pl.BlockSpec((1, SEQ, C_GW), lambda b, g: (b, 0, g)),
                   pl.BlockSpec((1, POOL_PAD, D_MODEL), lambda b, g: (b, 0, 0))],
        out_shape=[jax.ShapeDtypeStruct((BATCH, SEQ, D_MODEL), F32),
                   jax.ShapeDtypeStruct((BATCH, POOL_PAD, D_MODEL), F32)],
        scratch_shapes=[pltpu.VMEM((SEQ, 1), F32),
                        pltpu.VMEM((POOL_PAD + SEQ, C_GW), F32),
                        pltpu.VMEM((SEQ, C_GW), BF16)],
        compiler_params=_params(("arbitrary", "arbitrary")),
        name="pool_prompt",
    )(x, _layer_vec(g_all), w_all, _layer_vec(scale_all))


def _pool_sample_body(x_ref, g_ref, state_ref, w_ref, scale_ref, o_ref, seq_ref, z_ref):
    for b in range(DEC_BATCH):
        rows = slice(b * SAMPLE_PAD, (b + 1) * SAMPLE_PAD)
        seq_ref[b, 0:POOL_PAD, :] = state_ref[b]
        seq_ref[b, POOL_PAD:, :] = _rms_rows(x_ref[rows, :], g_ref[...])
    for b in range(DEC_BATCH):
        rows = slice(b * SAMPLE_PAD, (b + 1) * SAMPLE_PAD)
        for gi, w in enumerate(POOL_WINDOWS):
            cs = slice(gi * C_GW, (gi + 1) * C_GW)
            cur = seq_ref[b, POOL_PAD:POOL_PAD + SAMPLE_PAD, cs]
            tot = cur
            for k in range(1, w):
                tot = tot + seq_ref[b, POOL_PAD - k:POOL_PAD - k + SAMPLE_PAD, cs]
            pos = PAST_LEN + lax.broadcasted_iota(jnp.int32, (SAMPLE_PAD, 1), 0)
            cnt = jnp.minimum(w, pos + 1).astype(F32)
            z_ref[rows, cs] = (tot / cnt - cur).astype(BF16)
    for gi in range(C_GROUPS):
        cs = slice(gi * C_GW, (gi + 1) * C_GW)
        y = _dot(z_ref[:, cs], w_ref[gi].astype(BF16))
        o_ref[:, cs] = x_ref[:, cs] + y * scale_ref[:, cs]


def _pool_sample(xs, g_all, g_layer, state_pad, w_all, scale_all, layer):
    seq_rows = POOL_PAD + SAMPLE_PAD
    return pl.pallas_call(
        _pool_sample_body,
        grid=(1,),
        in_specs=[
            pl.BlockSpec((MS, D_MODEL), lambda i: (0, 0)),
            pl.BlockSpec((None, 1, D_MODEL), lambda i: (g_layer, 0, 0)),
            pl.BlockSpec((DEC_BATCH, POOL_PAD, D_MODEL), lambda i: (0, 0, 0)),
            pl.BlockSpec((None, C_GROUPS, C_GW, C_GW), lambda i: (layer, 0, 0, 0)),
            pl.BlockSpec((None, 1, D_MODEL), lambda i: (layer, 0, 0)),
        ],
        out_specs=[pl.BlockSpec((MS, D_MODEL), lambda i: (0, 0)),
                   pl.BlockSpec((DEC_BATCH, seq_rows, D_MODEL), lambda i: (0, 0, 0))],
        out_shape=[jax.ShapeDtypeStruct((MS, D_MODEL), F32),
                   jax.ShapeDtypeStruct((DEC_BATCH, seq_rows, D_MODEL), F32)],
        scratch_shapes=[pltpu.VMEM((MS, D_MODEL), BF16)],
        compiler_params=_params(("arbitrary",)),
        name="pool_sample",
    )(xs, _layer_vec(g_all), state_pad, w_all, _layer_vec(scale_all))


def _mixer_a(xp, xs, norm_g, layer, ia, a_w_in, a_ln_g, a_ln_b, a_w_s, a_b_s, a_w_out):
    b_s = a_b_s[ia]
    bias_p = jnp.repeat(jnp.transpose(b_s), A_HD, axis=1)
    yp, zs = _amix_prompt(xp, xs, norm_g, layer, a_w_in, a_ln_g, a_ln_b, a_w_s, bias_p, a_w_out, ia)
    ws_s = jnp.tile(a_w_s[ia][:, :SAMPLE_PAD, :SAMPLE_PAD], (1, DEC_BATCH, DEC_BATCH))
    bias_s = jnp.tile(jnp.repeat(jnp.transpose(b_s[:, :SAMPLE_PAD]), A_HD, axis=1), (DEC_BATCH, 1))
    ys, v_s = _amix_sample(xs, zs, a_ln_g, a_ln_b, ws_s, bias_s, a_w_out, ia)
    return yp, ys, v_s


def _mixer_b(xp, xs, norm_g, layer, ib, caches, b_w_qkv, b_q_g, b_k_g, b_w_out):
    ones = jnp.ones((B_GROUPS, B_WIDTH), F32)
    gain = jnp.stack([jnp.tile(b_q_g[ib], (1, B_HEADS)), jnp.tile(b_k_g[ib], (1, B_HEADS)), ones], axis=1)
    gain = gain.reshape(1, QKV_COLS)
    qkv_head, kv_last, qkv_s, *kv_tails = _qkv(xp, xs, norm_g, layer, b_w_qkv, ib, gain)
    qkv_head = qkv_head.reshape(BATCH, SEQ, QKV_HEAD_COLS)
    kv_last = kv_last.reshape(BATCH, SEQ, KV_W)
    op = _attn_prompt(qkv_head, kv_last)
    qkv_s = jnp.transpose(qkv_s.reshape(DEC_BATCH, SAMPLE_PAD, B_GROUPS, 3 * B_HEADS, B_HD), (2, 0, 1, 3, 4))
    q_s = qkv_s[:, :, :, :B_HEADS]
    kv_new = qkv_s[:, :, :, B_HEADS:]
    c = [cc[ib].reshape(DEC_BATCH, cc.shape[2], 2 * B_HEADS, B_HD) for cc in caches]
    os_ = _attn_sample(q_s, kv_new, c)
    yp, ys = _proj_residual(op.reshape(MP, B_WIDTH), os_.reshape(MS, B_WIDTH), b_w_out, ib, xp, xs,
                            name="b_out")
    new_p = []
    for g in range(B_GROUPS):
        keep = min(B_WINDOWS[g], SEQ)
        kv = kv_last[:, SEQ - keep:] if g == B_GROUPS - 1 else kv_tails[g]
        new_p.append(kv.reshape(BATCH, keep, 2, B_HEADS, B_HD))
    new_s = [kv_new[g, :, :DEC_SEQ].reshape(DEC_BATCH, DEC_SEQ, 2, B_HEADS, B_HD)
             for g in range(B_GROUPS)]
    return yp, ys, new_p, new_s


def _mixer_c(xp, xs, norm_g, layer, ic, state, c_w, c_scale):
    yp, tail = _pool_prompt(xp.reshape(BATCH, SEQ, D_MODEL), norm_g, layer, c_w, c_scale, ic)
    yp = yp.reshape(MP, D_MODEL)
    state_pad = jnp.pad(state[ic], ((0, 0), (POOL_PAD - POOL_STATE, 0), (0, 0)))
    ys, seq = _pool_sample(xs, norm_g, layer, state_pad, c_w, c_scale, ic)
    pool_p = tail[:, POOL_PAD - POOL_STATE:]
    first = POOL_PAD + DEC_SEQ - POOL_STATE
    pool_s = seq[:, first:first + POOL_STATE]
    return yp, ys, pool_p, pool_s


def kernel(x_prompt, x_sample, cache_b_kv0, cache_b_kv1, cache_b_kv2, state_c_pool, norm_mix_g, norm_ffn_g, a_w_in, a_ln_g, a_ln_b, a_w_s, a_b_s, a_w_out, b_w_qkv, b_q_g, b_k_g, b_w_out, c_w, c_scale, ffn_w1, ffn_w3, ffn_w2):
    xp = x_prompt.reshape(MP, D_MODEL)
    xs = jnp.pad(x_sample, ((0, 0), (0, SAMPLE_PAD - DEC_SEQ), (0, 0))).reshape(MS, D_MODEL)
    a_v_s, pool_p, pool_s = [], [], []
    kv_p = [[] for _ in range(B_GROUPS)]
    kv_s = [[] for _ in range(B_GROUPS)]
    ia = ib = ic = 0
    for layer in range(DEPTH):
        kind = layer % N_MIXERS
        if kind == 0:
            xp, xs, v_s = _mixer_a(xp, xs, norm_mix_g, layer, ia, a_w_in, a_ln_g, a_ln_b, a_w_s, a_b_s, a_w_out)
            a_v_s.append(v_s.reshape(DEC_BATCH, SAMPLE_PAD, A_WIDTH)[:, :DEC_SEQ])
            ia += 1
        elif kind == 1:
            caches = (cache_b_kv0, cache_b_kv1, cache_b_kv2)
            xp, xs, kvp, kvs = _mixer_b(xp, xs, norm_mix_g, layer, ib, caches, b_w_qkv, b_q_g, b_k_g, b_w_out)
            for g in range(B_GROUPS):
                kv_p[g].append(kvp[g])
                kv_s[g].append(kvs[g])
            ib += 1
        else:
            xp, xs, pp, ps = _mixer_c(xp, xs, norm_mix_g, layer, ic, state_c_pool, c_w, c_scale)
            pool_p.append(pp)
            pool_s.append(ps)
            ic += 1
        xp, xs = _ffn(xp, xs, norm_ffn_g, ffn_w1, ffn_w3, ffn_w2, layer)
    y_prompt = xp.reshape(BATCH, SEQ, D_MODEL)
    y_sample = xs.reshape(DEC_BATCH, SAMPLE_PAD, D_MODEL)[:, :DEC_SEQ]
    return (y_prompt, y_sample, jnp.stack(a_v_s),
            jnp.stack(kv_p[0]), jnp.stack(kv_p[1]), jnp.stack(kv_p[2]),
            jnp.stack(kv_s[0]), jnp.stack(kv_s[1]), jnp.stack(kv_s[2]),
            jnp.stack(pool_p), jnp.stack(pool_s))
```

```python
import jax
import jax.numpy as jnp
from jax import lax
from jax.experimental import pallas as pl
from jax.experimental.pallas import tpu as pltpu

F32 = jnp.float32
BF16 = jnp.bfloat16

D_MODEL = 2048
BATCH = 4
SEQ = 2048
DEPTH = 4
DEC_BATCH = 8
DEC_SEQ = 4
PAST_LEN = 16384
N_MIXERS = 3
A_CHUNK = 128
A_WIDTH = D_MODEL
A_HEADS = 16
A_HD = A_WIDTH // A_HEADS
B_WINDOWS = (128, 512, 2048)
B_DILATIONS = (1, 4, 16)
B_GROUPS = 3
B_HD = 128
B_HEADS = D_MODEL // B_HD
B_WIDTH = B_HEADS * B_HD
B_SCALE = B_HD ** -0.5
B_BAND = 128
POOL_WINDOWS = (2, 4, 8, 16)
C_GROUPS = 4
C_GW = D_MODEL // C_GROUPS
POOL_STATE = max(POOL_WINDOWS) - 1
POOL_PAD = POOL_STATE + 1
D_FF = ((8 * D_MODEL + 3 * 256 - 1) // (3 * 256)) * 256
EPS = 1e-6

SUBLANES = 8
LANES = 128
VMEM_LIMIT_BYTES = 56 * 1024 * 1024

SAMPLE_PAD = SUBLANES
MP = BATCH * SEQ
MS = DEC_BATCH * SAMPLE_PAD
NEG_BIG = -1e30

TM = 1024
TN = 512
TF_FFN = 256
NORM_ROWS = 256


def _params(semantics):
    return pltpu.CompilerParams(dimension_semantics=semantics,
                                vmem_limit_bytes=VMEM_LIMIT_BYTES)


def _dot(a, b):
    return jnp.dot(a, b, preferred_element_type=F32)


def _dot_nt(a, b):
    return lax.dot_general(a, b, (((1,), (1,)), ((), ())), preferred_element_type=F32)


def _rms_rows(x, g):
    r = lax.rsqrt(jnp.mean(x * x, axis=-1, keepdims=True) + EPS)
    return (x * r) * g


def _for_row_chunks(rows, chunk, fn):
    chunk = min(chunk, rows)
    n = rows // chunk
    if n == 1:
        fn(pl.ds(0, chunk))
        return

    def body(c, carry):
        fn(pl.ds(pl.multiple_of(c * chunk, chunk), chunk))
        return carry

    lax.fori_loop(0, n, body, 0)


def _layer_vec(stacked):
    return stacked.reshape(stacked.shape[0], 1, stacked.shape[1])


def _hold_after_first_tile(col_fn, last):
    return lambda i, j: (0, jnp.where(i == 0, col_fn(j), last))


def _ffn_body(x_ref, xs_ref, g_ref, w1_ref, w3_ref, w2_ref, o_ref, os_ref, hn_ref):
    i = pl.program_id(0)
    j = pl.program_id(1)
    tm = x_ref.shape[0]

    def gate(h, w1, w3):
        return (jax.nn.silu(_dot(h, w1)) * _dot(h, w3)).astype(BF16)

    def first_step(with_sample):
        w1 = w1_ref[...].astype(BF16)
        w3 = w3_ref[...].astype(BF16)
        w2 = w2_ref[...].astype(BF16)
        for c in range(tm // NORM_ROWS):
            rows = slice(c * NORM_ROWS, (c + 1) * NORM_ROWS)
            x = x_ref[rows, :]
            h = _rms_rows(x, g_ref[...]).astype(BF16)
            hn_ref[rows, :] = h
            o_ref[rows, :] = x + _dot(gate(h, w1, w3), w2)
        if with_sample:
            xs = xs_ref[...]
            h = _rms_rows(xs, g_ref[...]).astype(BF16)
            hn_ref[tm:, :] = h
            os_ref[...] = xs + _dot(gate(h, w1, w3), w2)

    def later_step(with_sample):
        w1 = w1_ref[...].astype(BF16)
        w3 = w3_ref[...].astype(BF16)
        w2 = w2_ref[...].astype(BF16)
        if with_sample:
            gt = gate(hn_ref[...], w1, w3)
            o_ref[...] += _dot(gt[:tm], w2)
            os_ref[...] += _dot(gt[tm:], w2)
        else:
            o_ref[...] += _dot(gate(hn_ref[0:tm, :], w1, w3), w2)

    for with_sample, tile_cond in ((True, i == 0), (False, i != 0)):
        for step_fn, col_cond in ((first_step, j == 0), (later_step, j != 0)):
            @pl.when(tile_cond & col_cond)
            def _(step_fn=step_fn, with_sample=with_sample):
                step_fn(with_sample)


def _ffn(x, xs, g_all, w1_all, w3_all, w2_all, layer):
    return pl.pallas_call(
        _ffn_body,
        grid=(MP // TM, D_FF // TF_FFN),
        in_specs=[
            pl.BlockSpec((TM, D_MODEL), lambda i, j: (i, 0)),
            pl.BlockSpec((MS, D_MODEL), lambda i, j: (0, 0)),
            pl.BlockSpec((None, 1, D_MODEL), lambda i, j: (layer, 0, 0)),
            pl.BlockSpec((None, D_MODEL, TF_FFN), lambda i, j: (layer, 0, j)),
            pl.BlockSpec((None, D_MODEL, TF_FFN), lambda i, j: (layer, 0, j)),
            pl.BlockSpec((None, TF_FFN, D_MODEL), lambda i, j: (layer, j, 0)),
        ],
        out_specs=[pl.BlockSpec((TM, D_MODEL), lambda i, j: (i, 0)),
                   pl.BlockSpec((MS, D_MODEL), lambda i, j: (0, 0))],
        out_shape=[jax.ShapeDtypeStruct((MP, D_MODEL), F32),
                   jax.ShapeDtypeStruct((MS, D_MODEL), F32)],
        scratch_shapes=[pltpu.VMEM((TM + MS, D_MODEL), BF16)],
        compiler_params=_params(("arbitrary", "arbitrary")),
        name="ffn",
    )(x, xs, _layer_vec(g_all), w1_all, w3_all, w2_all)


def _fill_lhs(h_ref, x_ref, xs_ref, g_ref, i, j):
    tm = x_ref.shape[0]

    def prep(x):
        if g_ref is not None:
            x = _rms_rows(x, g_ref[...])
        return x.astype(BF16)

    @pl.when(j == 0)
    def _():
        def chunk(rows):
            h_ref[rows, :] = prep(x_ref[rows, :])
        _for_row_chunks(tm, NORM_ROWS, chunk)

    @pl.when((i == 0) & (j == 0))
    def _():
        h_ref[tm:, :] = prep(xs_ref[...])


def _proj_residual_body(x_ref, xs_ref, w_ref, res_ref, res_s_ref, o_ref, os_ref, h_ref):
    i = pl.program_id(0)
    j = pl.program_id(1)
    tm = x_ref.shape[0]
    _fill_lhs(h_ref, x_ref, xs_ref, None, i, j)
    cols = pl.ds(pl.multiple_of(j * TN, TN), TN)

    @pl.when(i == 0)
    def _():
        acc = _dot(h_ref[...], w_ref[:, cols].astype(BF16))
        o_ref[...] = res_ref[...] + acc[:tm]
        os_ref[...] = res_s_ref[...] + acc[tm:]

    @pl.when(i != 0)
    def _():
        o_ref[...] = res_ref[...] + _dot(h_ref[0:tm, :], w_ref[:, cols].astype(BF16))


def _proj_residual(x, xs, w_all, layer, res, res_s, *, name):
    k = x.shape[1]
    n = w_all.shape[2]
    nj = n // TN
    return pl.pallas_call(
        _proj_residual_body,
        grid=(MP // TM, nj),
        in_specs=[pl.BlockSpec((TM, k), lambda i, j: (i, 0)),
                  pl.BlockSpec((MS, k), lambda i, j: (0, 0)),
                  pl.BlockSpec((None, k, n), lambda i, j: (layer, 0, 0), pipeline_mode=pl.Buffered(1)),
                  pl.BlockSpec((TM, TN), lambda i, j: (i, j)),
                  pl.BlockSpec((MS, TN), lambda i, j: (0, j))],
        out_specs=[pl.BlockSpec((TM, TN), lambda i, j: (i, j)),
                   pl.BlockSpec((MS, TN), _hold_after_first_tile(lambda j: j, nj - 1))],
        out_shape=[jax.ShapeDtypeStruct((MP, n), F32),
                   jax.ShapeDtypeStruct((MS, n), F32)],
        scratch_shapes=[pltpu.VMEM((TM + MS, k), BF16)],
        compiler_params=_params(("arbitrary", "arbitrary")),
        name=name,
    )(x, xs, w_all, res, res_s)


TN_QKV = 1024
QKV_TILES = B_WIDTH // TN_QKV
QKV_COLS = B_GROUPS * 3 * B_WIDTH
KV_W = 2 * B_WIDTH
QKV_HEAD_COLS = QKV_COLS - KV_W
MXU_COLS = 256
QKV_NJ = QKV_COLS // TN_QKV
QKV_LAST_KV = QKV_HEAD_COLS // TN_QKV


def _qkv_body(x_ref, xs_ref, g_ref, w_ref, gain_ref, o_ref, olast_ref, os_ref, h_ref):
    i = pl.program_id(0)
    j = pl.program_id(1)
    tm = x_ref.shape[0]
    _fill_lhs(h_ref, x_ref, xs_ref, g_ref, i, j)
    is_v = (j // QKV_TILES) % 3 == 2
    in_last = j >= QKV_LAST_KV

    def store_heads(acc, out_ref, c0):
        for hh in range(MXU_COLS // B_HD):
            a = acc[:, hh * B_HD:(hh + 1) * B_HD]
            cs = slice(c0 + hh * B_HD, c0 + (hh + 1) * B_HD)
            r = lax.rsqrt(jnp.mean(a * a, axis=-1, keepdims=True) + EPS)
            r = jnp.where(is_v, 1.0, r)
            out_ref[:, cs] = (a * r) * gain_ref[:, cs]

    def run(out_ref, with_sample):
        h = h_ref[...] if with_sample else h_ref[0:tm, :]
        for c0 in range(0, TN_QKV, MXU_COLS):
            acc = _dot(h, w_ref[:, c0:c0 + MXU_COLS].astype(BF16))
            if with_sample:
                store_heads(acc[:tm], out_ref, c0)
                store_heads(acc[tm:], os_ref, c0)
            else:
                store_heads(acc, out_ref, c0)

    for with_sample, tile_cond in ((True, i == 0), (False, i != 0)):
        for out_ref, dest_cond in ((o_ref, jnp.logical_not(in_last)), (olast_ref, in_last)):
            @pl.when(tile_cond & dest_cond)
            def _(out_ref=out_ref, with_sample=with_sample):
                run(out_ref, with_sample)


def _qkv(x, xs, g_all, g_layer, w_all, layer, gain):
    def out_map(i, j):
        return i, jnp.minimum(j, QKV_LAST_KV - 1)

    def out_last_map(i, j):
        return i, jnp.maximum(j - QKV_LAST_KV, 0)

    return pl.pallas_call(
        _qkv_body,
        grid=(MP // TM, QKV_NJ),
        in_specs=[
            pl.BlockSpec((TM, D_MODEL), lambda i, j: (i, 0), pipeline_mode=pl.Buffered(1)),
            pl.BlockSpec((MS, D_MODEL), lambda i, j: (0, 0)),
            pl.BlockSpec((None, 1, D_MODEL), lambda i, j: (g_layer, 0, 0)),
            pl.BlockSpec((None, D_MODEL, TN_QKV), lambda i, j: (layer, 0, j)),
            pl.BlockSpec((1, TN_QKV), lambda i, j: (0, j)),
        ],
        out_specs=[pl.BlockSpec((TM, TN_QKV), out_map),
                   pl.BlockSpec((TM, TN_QKV), out_last_map),
                   pl.BlockSpec((MS, TN_QKV), _hold_after_first_tile(lambda j: j, QKV_NJ - 1))],
        out_shape=[jax.ShapeDtypeStruct((MP, QKV_HEAD_COLS), F32),
                   jax.ShapeDtypeStruct((MP, KV_W), F32),
                   jax.ShapeDtypeStruct((MS, QKV_COLS), F32)],
        scratch_shapes=[pltpu.VMEM((TM + MS, D_MODEL), BF16)],
        compiler_params=_params(("arbitrary", "arbitrary")),
        name="qkv",
    )(x, xs, _layer_vec(g_all), w_all, gain)


def _mixing_weights(wsm_ref, ws_ref, t_len, block):
    r = lax.broadcasted_iota(jnp.int32, (t_len, t_len), 0)
    c = lax.broadcasted_iota(jnp.int32, (t_len, t_len), 1)
    keep = c <= r
    if block < t_len:
        keep = keep & ((r // block) == (c // block)) & ((c % block) < DEC_SEQ)
    for h in range(A_HEADS):
        wsm_ref[h] = jnp.where(keep, ws_ref[h], 0.0).astype(BF16)


TN_A_IN = 1024
TN_A_OUT = 256
A_UV = A_WIDTH // TN_A_IN
A_P1 = 2 * A_UV
A_P2 = D_MODEL // TN_A_OUT


def _amix_prompt_body(x_ref, xs_ref, g_ref, win_ref, lng_ref, lnb_ref, ws_ref, bias_ref, wout_ref,
                      o_ref, zs_ref, h_ref, u_ref, v_ref, wsm_ref):
    i = pl.program_id(0)
    j = pl.program_id(1)
    tm = x_ref.shape[0]
    _fill_lhs(h_ref, x_ref, xs_ref, g_ref, i, j)

    def phase1(with_sample, dst_ref, slot):
        h = h_ref[...] if with_sample else h_ref[0:tm, :]
        for c0 in range(0, TN_A_IN, MXU_COLS):
            cs = slice(c0, c0 + MXU_COLS)
            z = jax.nn.gelu(_dot(h, win_ref[:, cs].astype(BF16)), approximate=True)
            if with_sample:
                dst_ref[slot, :, cs] = z[:tm].astype(dst_ref.dtype)
                zs_ref[:, cs] = z[tm:]
            else:
                dst_ref[slot, :, cs] = z.astype(dst_ref.dtype)

    for with_sample, tile_cond in ((True, i == 0), (False, i != 0)):
        for dst_ref, half_cond, slot in ((u_ref, j < A_UV, j),
                                         (v_ref, (j >= A_UV) & (j < A_P1), j - A_UV)):
            @pl.when(tile_cond & half_cond)
            def _(with_sample=with_sample, dst_ref=dst_ref, slot=slot):
                phase1(with_sample, dst_ref, slot)

    @pl.when(j == A_P1)
    def _():
        _mixing_weights(wsm_ref, ws_ref, A_CHUNK, A_CHUNK)

        def chunk(rows):
            vs = [v_ref[k, rows, :] for k in range(A_UV)]
            mu = sum(jnp.sum(vk, axis=-1, keepdims=True) for vk in vs) / A_WIDTH
            ds = [vk - mu for vk in vs]
            var = sum(jnp.sum(dk * dk, axis=-1, keepdims=True) for dk in ds) / A_WIDTH
            r = lax.rsqrt(var + EPS)
            for h in range(A_HEADS):
                k, c = divmod(h * A_HD, TN_A_IN)
                cs = slice(h * A_HD, (h + 1) * A_HD)
                vln = (ds[k][:, c:c + A_HD] * r) * lng_ref[:, cs] + lnb_ref[:, cs]
                mixed = _dot(wsm_ref[h], vln.astype(BF16)) + bias_ref[:, cs]
                h_ref[rows, cs] = (u_ref[k, rows, c:c + A_HD] * mixed).astype(BF16)
        _for_row_chunks(tm, A_CHUNK, chunk)

    @pl.when(j >= A_P1)
    def _():
        cols = pl.ds(pl.multiple_of((j - A_P1) * TN_A_OUT, TN_A_OUT), TN_A_OUT)
        o_ref[...] = x_ref[:, cols] + _dot(h_ref[0:tm, :], wout_ref[...].astype(BF16))


def _amix_prompt(x, xs, g_all, g_layer, win_all, lng_all, lnb_all, ws_all, bias, wout_all, layer):
    def out_col(j):
        return jnp.maximum(j - A_P1, 0)

    def z_col(j):
        return jnp.minimum(j, A_P1 - 1)

    return pl.pallas_call(
        _amix_prompt_body,
        grid=(MP // TM, A_P1 + A_P2),
        in_specs=[
            pl.BlockSpec((TM, D_MODEL), lambda i, j: (i, 0), pipeline_mode=pl.Buffered(1)),
            pl.BlockSpec((MS, D_MODEL), lambda i, j: (0, 0)),
            pl.BlockSpec((None, 1, D_MODEL), lambda i, j: (g_layer, 0, 0)),
            pl.BlockSpec((None, D_MODEL, TN_A_IN), lambda i, j: (layer, 0, z_col(j))),
            pl.BlockSpec((None, 1, A_WIDTH), lambda i, j: (layer, 0, 0)),
            pl.BlockSpec((None, 1, A_WIDTH), lambda i, j: (layer, 0, 0)),
            pl.BlockSpec((None, A_HEADS, A_CHUNK, A_CHUNK), lambda i, j: (layer, 0, 0, 0)),
            pl.BlockSpec((A_CHUNK, A_WIDTH), lambda i, j: (0, 0)),
            pl.BlockSpec((None, A_WIDTH, TN_A_OUT), lambda i, j: (layer, 0, out_col(j))),
        ],
        out_specs=[pl.BlockSpec((TM, TN_A_OUT), lambda i, j: (i, out_col(j))),
                   pl.BlockSpec((MS, TN_A_IN), _hold_after_first_tile(z_col, A_P1 - 1))],
        out_shape=[jax.ShapeDtypeStruct((MP, D_MODEL), F32),
                   jax.ShapeDtypeStruct((MS, 2 * A_WIDTH), F32)],
        scratch_shapes=[pltpu.VMEM((TM + MS, D_MODEL), BF16),
                        pltpu.VMEM((A_UV, TM, TN_A_IN), BF16),
                        pltpu.VMEM((A_UV, TM, TN_A_IN), F32),
                        pltpu.VMEM((A_HEADS, A_CHUNK, A_CHUNK), BF16)],
        compiler_params=_params(("arbitrary", "arbitrary")),
        name="amix_prompt",
    )(x, xs, _layer_vec(g_all), win_all, _layer_vec(lng_all), _layer_vec(lnb_all), ws_all, bias,
      wout_all)


def _amix_sample_body(x_ref, u_ref, v_ref, lng_ref, lnb_ref, ws_ref, bias_ref, wout_ref,
                      o_ref, vout_ref, gated_ref, wsm_ref):
    j = pl.program_id(0)

    @pl.when(j == 0)
    def _():
        _mixing_weights(wsm_ref, ws_ref, MS, SAMPLE_PAD)
        v = v_ref[...]
        mu = jnp.mean(v, axis=-1, keepdims=True)
        d = v - mu
        var = jnp.mean(d * d, axis=-1, keepdims=True)
        vln = (d * lax.rsqrt(var + EPS)) * lng_ref[...] + lnb_ref[...]
        vout_ref[...] = vln
        vb = vln.astype(BF16)
        for h in range(A_HEADS):
            cs = slice(h * A_HD, (h + 1) * A_HD)
            mixed = _dot(wsm_ref[h], vb[:, cs]) + bias_ref[:, cs]
            gated_ref[:, cs] = (u_ref[:, cs] * mixed).astype(BF16)

    o_ref[...] = x_ref[...] + _dot(gated_ref[...], wout_ref[...].astype(BF16))


def _amix_sample(xs, zs, lng_all, lnb_all, ws_s, bias_s, wout_all, layer):
    return pl.pallas_call(
        _amix_sample_body,
        grid=(D_MODEL // TN,),
        in_specs=[
            pl.BlockSpec((MS, TN), lambda j: (0, j)),
            pl.BlockSpec((MS, A_WIDTH), lambda j: (0, 0)),
            pl.BlockSpec((MS, A_WIDTH), lambda j: (0, 1)),
            pl.BlockSpec((None, 1, A_WIDTH), lambda j: (layer, 0, 0)),
            pl.BlockSpec((None, 1, A_WIDTH), lambda j: (layer, 0, 0)),
            pl.BlockSpec((A_HEADS, MS, MS), lambda j: (0, 0, 0)),
            pl.BlockSpec((MS, A_WIDTH), lambda j: (0, 0)),
            pl.BlockSpec((None, A_WIDTH, TN), lambda j: (layer, 0, j)),
        ],
        out_specs=[pl.BlockSpec((MS, TN), lambda j: (0, j)),
                   pl.BlockSpec((MS, A_WIDTH), lambda j: (0, 0))],
        out_shape=[jax.ShapeDtypeStruct((MS, D_MODEL), F32),
                   jax.ShapeDtypeStruct((MS, A_WIDTH), F32)],
        scratch_shapes=[pltpu.VMEM((MS, A_WIDTH), BF16),
                        pltpu.VMEM((A_HEADS, MS, MS), BF16)],
        compiler_params=_params(("arbitrary",)),
        name="amix_sample",
    )(xs, zs, zs, _layer_vec(lng_all), _layer_vec(lnb_all), ws_s, bias_s, wout_all)


ATTN_HEADS = 1


ATTN_STEP = B_DILATIONS[1]
assert B_DILATIONS == (1, ATTN_STEP, ATTN_STEP * ATTN_STEP)


def _attn_prompt_body(q0, k0, v0, q1, k1, v1, q2, k2, v2, o_ref,
                      acc_ref, m_ref, l_ref, acc4_ref, m4_ref, l4_ref, q4_ref, k4_ref, v4_ref):
    n = B_BAND
    step_d = ATTN_STEP
    seq4 = SEQ // step_d
    qi = lax.broadcasted_iota(jnp.int32, (n, 2 * n), 0)
    kj = lax.broadcasted_iota(jnp.int32, (n, 2 * n), 1)
    dist = n + qi - kj
    band_mask = (dist >= 0) & (dist <= n)
    qi1 = lax.broadcasted_iota(jnp.int32, (n, n), 0)
    kj1 = lax.broadcasted_iota(jnp.int32, (n, n), 1)
    causal_mask = kj1 <= qi1

    def attend(q, k, v, mask):
        s = jnp.where(mask, _dot_nt((q * B_SCALE).astype(BF16), k.astype(BF16)), NEG_BIG)
        m = jnp.max(s, axis=-1, keepdims=True)
        p = jnp.exp(s - m)
        vb = v.astype(BF16)
        pv = _dot(p.astype(BF16), jnp.concatenate([vb, jnp.ones_like(vb)], axis=1))
        return pv[:, :B_HD], m, pv[:, B_HD:]

    def merge(old, new):
        acc_old, m_old, l_old = old
        acc, m, l = new
        m_new = jnp.maximum(m_old, m)
        a_old = jnp.exp(m_old - m_new)
        a_new = jnp.exp(m - m_new)
        return acc_old * a_old + acc * a_new, m_new, l_old * a_old + l * a_new

    for hh in range(ATTN_HEADS):
        cs = slice(hh * B_HD, (hh + 1) * B_HD)

        for c in range(SEQ // n):
            rows = pl.ds(c * n, n)
            rows_k, mask = (rows, causal_mask) if c == 0 else (pl.ds((c - 1) * n, 2 * n), band_mask)
            acc, m, l = attend(q0[0, rows, cs], k0[0, rows_k, cs], v0[0, rows_k, cs], mask)
            acc_ref[rows, :] = acc
            m_ref[rows, :] = jnp.broadcast_to(m, (n, LANES))
            l_ref[rows, :] = l

        for r in range(step_d):
            for c in range(seq4 // n):
                rows = pl.ds(r + c * n * step_d, n, stride=step_d)
                if c == 0:
                    rows_k, mask = rows, causal_mask
                else:
                    rows_k, mask = pl.ds(r + (c - 1) * n * step_d, 2 * n, stride=step_d), band_mask
                new = attend(q1[0, rows, cs], k1[0, rows_k, cs], v1[0, rows_k, cs], mask)
                acc, m, l = merge((acc_ref[rows, :], m_ref[rows, :], l_ref[rows, :]), new)
                dst = pl.ds(c * n, n)
                acc4_ref[r, dst, :] = acc
                m4_ref[r, dst, :] = m
                l4_ref[r, dst, :] = l

        assert seq4 // step_d == n
        for r in range(step_d):
            rows_r = pl.ds(r, seq4, stride=step_d)
            q4_ref[r] = q2[0, rows_r, cs]
            k4_ref[r] = k2[0, rows_r, cs]
            v4_ref[r] = v2[0, rows_r, cs]
        for r in range(step_d):
            for r2 in range(step_d):
                rows = pl.ds(r2, n, stride=step_d)
                new = attend(q4_ref[r, rows, :], k4_ref[r, rows, :], v4_ref[r, rows, :], causal_mask)
                acc, _, l = merge((acc4_ref[r, rows, :], m4_ref[r, rows, :], l4_ref[r, rows, :]), new)
                acc4_ref[r, rows, :] = acc / l

        for r in range(step_d):
            acc_ref[pl.ds(r, seq4, stride=step_d), :] = acc4_ref[r]
        o_ref[0, :, cs] = acc_ref[...].astype(o_ref.dtype)


def _attn_prompt(qkv_head, kv_last):
    hw = ATTN_HEADS * B_HD

    def spec(first_col):
        return pl.BlockSpec((1, SEQ, hw), lambda b, h: (b, 0, first_col // hw + h))

    in_specs, args = [], []
    for g in range(B_GROUPS):
        c0 = g * 3 * B_WIDTH
        if g < B_GROUPS - 1:
            in_specs += [spec(c0), spec(c0 + B_WIDTH), spec(c0 + 2 * B_WIDTH)]
            args += [qkv_head] * 3
        else:
            in_specs += [spec(c0), spec(0), spec(B_WIDTH)]
            args += [qkv_head, kv_last, kv_last]
    return pl.pallas_call(
        _attn_prompt_body,
        grid=(BATCH, B_HEADS // ATTN_HEADS),
        in_specs=in_specs,
        out_specs=pl.BlockSpec((1, SEQ, hw), lambda b, h: (b, 0, h)),
        out_shape=jax.ShapeDtypeStruct((BATCH, SEQ, B_WIDTH), BF16),
        scratch_shapes=[pltpu.VMEM((SEQ, B_HD), F32)] * 3
                       + [pltpu.VMEM((ATTN_STEP, SEQ // ATTN_STEP, B_HD), F32)] * 6,
        compiler_params=_params(("parallel", "parallel")),
        name="attn_prompt",
    )(*args)


assert all(B_WINDOWS[g] == B_BAND * B_DILATIONS[g] for g in range(B_GROUPS))
assert PAST_LEN >= max(B_WINDOWS) and B_DILATIONS[0] == 1 and DEC_SEQ <= min(B_DILATIONS[1:])


def _attn_sample_body(q_ref, kvn_ref, c0_ref, c1_ref, c2_ref, o_ref):
    k_heads = slice(0, B_HEADS)
    v_heads = slice(B_HEADS, 2 * B_HEADS)

    def piece(q, k, v, valid=None):
        s = jnp.sum(k * q[None], axis=-1, keepdims=True) * B_SCALE
        if valid is not None:
            s = jnp.where(valid, s, NEG_BIG)
        m = jnp.max(s, axis=0)
        p = jnp.exp(s - m[None])
        return m, jnp.sum(p, axis=0), jnp.sum(p * v, axis=0)

    row0 = lax.broadcasted_iota(jnp.int32, (B_WINDOWS[0], B_HEADS, 1), 0)
    strided = (None, c1_ref, c2_ref)
    for t in range(DEC_SEQ):
        terms = []
        for g in range(B_GROUPS):
            q = q_ref[g, 0, t]
            if g == 0:
                valid = B_WINDOWS[0] + t - row0 <= B_BAND
                terms.append(piece(q, c0_ref[0, :, k_heads, :], c0_ref[0, :, v_heads, :], valid))
                new = slice(0, t + 1)
            else:
                c_ref = strided[g]
                terms.append(piece(q, c_ref[0, :, t, k_heads, :], c_ref[0, :, t, v_heads, :]))
                new = slice(t, t + 1)
            terms.append(piece(q, kvn_ref[g, 0, new, k_heads, :], kvn_ref[g, 0, new, v_heads, :]))
        m_all = terms[0][0]
        for m, _, _ in terms[1:]:
            m_all = jnp.maximum(m_all, m)
        l_all = jnp.zeros((B_HEADS, 1), F32)
        acc_all = jnp.zeros((B_HEADS, B_HD), F32)
        for m, l, acc in terms:
            w = jnp.exp(m - m_all)
            l_all = l_all + l * w
            acc_all = acc_all + acc * w
        o_ref[0, t] = acc_all / l_all
    o_ref[0, DEC_SEQ:] = jnp.zeros((SAMPLE_PAD - DEC_SEQ, B_HEADS, B_HD), F32)


def _attn_sample(q_s, kv_new, caches):
    kv_rows = 2 * B_HEADS

    def by_residue(g):
        dil = B_DILATIONS[g]
        view = caches[g].reshape(DEC_BATCH, B_BAND, dil, kv_rows, B_HD)
        return view, pl.BlockSpec((1, B_BAND, DEC_SEQ, kv_rows, B_HD), lambda b: (b, 0, 0, 0, 0))

    c1, c1_spec = by_residue(1)
    c2, c2_spec = by_residue(2)
    return pl.pallas_call(
        _attn_sample_body,
        grid=(DEC_BATCH,),
        in_specs=[
            pl.BlockSpec((B_GROUPS, 1, SAMPLE_PAD, B_HEADS, B_HD), lambda b: (0, b, 0, 0, 0)),
            pl.BlockSpec((B_GROUPS, 1, SAMPLE_PAD, kv_rows, B_HD), lambda b: (0, b, 0, 0, 0)),
            pl.BlockSpec((1, B_WINDOWS[0], kv_rows, B_HD), lambda b: (b, 0, 0, 0)),
            c1_spec, c2_spec,
        ],
        out_specs=pl.BlockSpec((1, SAMPLE_PAD, B_HEADS, B_HD), lambda b: (b, 0, 0, 0)),
        out_shape=jax.ShapeDtypeStruct((DEC_BATCH, SAMPLE_PAD, B_HEADS, B_HD), F32),
        compiler_params=_params(("parallel",)),
        name="attn_sample",
    )(q_s, kv_new, caches[0], c1, c2)


POOL_ROWS = 256
assert all(w & (w - 1) == 0 for w in POOL_WINDOWS)


def _pool_prompt_body(x_ref, g_ref, w_ref, scale_ref, o_ref, tail_ref, r_ref, sum_ref, z_ref):
    grp = pl.program_id(1)
    n_chunks = SEQ // POOL_ROWS

    @pl.when(grp == 0)
    def _():
        def chunk(rows):
            x = x_ref[0, rows, :]
            r_ref[rows, :] = lax.rsqrt(jnp.mean(x * x, axis=-1, keepdims=True) + EPS)
        _for_row_chunks(SEQ, NORM_ROWS, chunk)

    for gi, w in enumerate(POOL_WINDOWS):
        @pl.when(grp == gi)
        def _(gi=gi, w=w):
            cs = slice(gi * C_GW, (gi + 1) * C_GW)

            def normed(c):
                rows = slice(c * POOL_ROWS, (c + 1) * POOL_ROWS)
                return (x_ref[0, rows, cs] * r_ref[rows, :]) * g_ref[:, cs]

            sum_ref[0:POOL_PAD, :] = jnp.zeros((POOL_PAD, C_GW), F32)
            for c in range(n_chunks):
                h = normed(c)
                sum_ref[POOL_PAD + c * POOL_ROWS:POOL_PAD + (c + 1) * POOL_ROWS, :] = h
                if c == n_chunks - 1:
                    tail_ref[0, :, cs] = h[POOL_ROWS - POOL_PAD:]
            k = 1
            while k < w:
                for c in reversed(range(n_chunks)):
                    r0 = POOL_PAD + c * POOL_ROWS
                    sum_ref[r0:r0 + POOL_ROWS, :] = (sum_ref[r0:r0 + POOL_ROWS, :]
                                                    + sum_ref[r0 - k:r0 - k + POOL_ROWS, :])
                k *= 2
            for c in range(n_chunks):
                r0 = POOL_PAD + c * POOL_ROWS
                pos = c * POOL_ROWS + lax.broadcasted_iota(jnp.int32, (POOL_ROWS, 1), 0)
                cnt = jnp.minimum(w, pos + 1).astype(F32)
                z_ref[c * POOL_ROWS:(c + 1) * POOL_ROWS, :] = (
                    sum_ref[r0:r0 + POOL_ROWS, :] / cnt - normed(c)).astype(BF16)
            y = _dot(z_ref[...], w_ref[...].astype(BF16))
            o_ref[0] = x_ref[0, :, cs] + y * scale_ref[...]


def _pool_prompt(x, g_all, g_layer, w_all, scale_all, layer):
    return pl.pallas_call(
        _pool_prompt_body,
        grid=(BATCH, C_GROUPS),
        in_specs=[pl.BlockSpec((1, SEQ, D_MODEL), lambda b, g: (b, 0, 0)),
                  pl.BlockSpec((None, 1, D_MODEL), lambda b, g: (g_layer, 0, 0)),
                  pl.BlockSpec((None, None, C_GW, C_GW), lambda b, g: (layer, g, 0, 0)),
                  pl.BlockSpec((None, 1, C_GW), lambda b, g: (layer, 0, g))],
        out_specs=[pl.BlockSpec((1, SEQ, C_GW), lambda b, g: (b, 0, g)),
                   pl.BlockSpec((1, POOL_PAD, D_MODEL), lambda b, g: (b, 0, 0))],
        out_shape=[jax.ShapeDtypeStruct((BATCH, SEQ, D_MODEL), F32),
                   jax.ShapeDtypeStruct((BATCH, POOL_PAD, D_MODEL), F32)],
        scratch_shapes=[pltpu.VMEM((SEQ, 1), F32),
                        pltpu.VMEM((POOL_PAD + SEQ, C_GW), F32),
                        pltpu.VMEM((SEQ, C_GW), BF16)],
        compiler_params=_params(("arbitrary", "arbitrary")),
        name="pool_prompt",
    )(x, _layer_vec(g_all), w_all, _layer_vec(scale_all))


def _pool_sample_body(x_ref, g_ref, state_ref, w_ref, scale_ref, o_ref, seq_ref, z_ref):
    for b in range(DEC_BATCH):
        rows = slice(b * SAMPLE_PAD, (b + 1) * SAMPLE_PAD)
        seq_ref[b, 0:POOL_PAD, :] = state_ref[b]
        seq_ref[b, POOL_PAD:, :] = _rms_rows(x_ref[rows, :], g_ref[...])
    for b in range(DEC_BATCH):
        rows = slice(b * SAMPLE_PAD, (b + 1) * SAMPLE_PAD)
        for gi, w in enumerate(POOL_WINDOWS):
            cs = slice(gi * C_GW, (gi + 1) * C_GW)
            cur = seq_ref[b, POOL_PAD:POOL_PAD + SAMPLE_PAD, cs]
            tot = cur
            for k in range(1, w):
                tot = tot + seq_ref[b, POOL_PAD - k:POOL_PAD - k + SAMPLE_PAD, cs]
            pos = PAST_LEN + lax.broadcasted_iota(jnp.int32, (SAMPLE_PAD, 1), 0)
            cnt = jnp.minimum(w, pos + 1).astype(F32)
            z_ref[rows, cs] = (tot / cnt - cur).astype(BF16)
    for gi in range(C_GROUPS):
        cs = slice(gi * C_GW, (gi + 1) * C_GW)
        y = _dot(z_ref[:, cs], w_ref[gi].astype(BF16))
        o_ref[:, cs] = x_ref[:, cs] + y * scale_ref[:, cs]


def _pool_sample(xs, g_all, g_layer, state_pad, w_all, scale_all, layer):
    seq_rows = POOL_PAD + SAMPLE_PAD
    return pl.pallas_call(
        _pool_sample_body,
        grid=(1,),
        in_specs=[
            pl.BlockSpec((MS, D_MODEL), lambda i: (0, 0)),
            pl.BlockSpec((None, 1, D_MODEL), lambda i: (g_layer, 0, 0)),
            pl.BlockSpec((DEC_BATCH, POOL_PAD, D_MODEL), lambda i: (0, 0, 0)),
            pl.BlockSpec((None, C_GROUPS, C_GW, C_GW), lambda i: (layer, 0, 0, 0)),
            pl.BlockSpec((None, 1, D_MODEL), lambda i: (layer, 0, 0)),
        ],
        out_specs=[pl.BlockSpec((MS, D_MODEL), lambda i: (0, 0)),
                   pl.BlockSpec((DEC_BATCH, seq_rows, D_MODEL), lambda i: (0, 0, 0))],
        out_shape=[jax.ShapeDtypeStruct((MS, D_MODEL), F32),
                   jax.ShapeDtypeStruct((DEC_BATCH, seq_rows, D_MODEL), F32)],
        scratch_shapes=[pltpu.VMEM((MS, D_MODEL), BF16)],
        compiler_params=_params(("arbitrary",)),
        name="pool_sample",
    )(xs, _layer_vec(g_all), state_pad, w_all, _layer_vec(scale_all))


def _mixer_a(xp, xs, norm_g, layer, ia, a_w_in, a_ln_g, a_ln_b, a_w_s, a_b_s, a_w_out):
    b_s = a_b_s[ia]
    bias_p = jnp.repeat(jnp.transpose(b_s), A_HD, axis=1)
    yp, zs = _amix_prompt(xp, xs, norm_g, layer, a_w_in, a_ln_g, a_ln_b, a_w_s, bias_p, a_w_out, ia)
    ws_s = jnp.tile(a_w_s[ia][:, :SAMPLE_PAD, :SAMPLE_PAD], (1, DEC_BATCH, DEC_BATCH))
    bias_s = jnp.tile(jnp.repeat(jnp.transpose(b_s[:, :SAMPLE_PAD]), A_HD, axis=1), (DEC_BATCH, 1))
    ys, v_s = _amix_sample(xs, zs, a_ln_g, a_ln_b, ws_s, bias_s, a_w_out, ia)
    return yp, ys, v_s


def _mixer_b(xp, xs, norm_g, layer, ib, caches, b_w_qkv, b_q_g, b_k_g, b_w_out):
    ones = jnp.ones((B_GROUPS, B_WIDTH), F32)
    gain = jnp.stack([jnp.tile(b_q_g[ib], (1, B_HEADS)), jnp.tile(b_k_g[ib], (1, B_HEADS)), ones], axis=1)
    gain = gain.reshape(1, QKV_COLS)
    qkv_head, kv_last, qkv_s = _qkv(xp, xs, norm_g, layer, b_w_qkv, ib, gain)
    qkv_head = qkv_head.reshape(BATCH, SEQ, QKV_HEAD_COLS)
    kv_last = kv_last.reshape(BATCH, SEQ, KV_W)
    op = _attn_prompt(qkv_head, kv_last)
    qkv_s = jnp.transpose(qkv_s.reshape(DEC_BATCH, SAMPLE_PAD, B_GROUPS, 3 * B_HEADS, B_HD), (2, 0, 1, 3, 4))
    q_s = qkv_s[:, :, :, :B_HEADS]
    kv_new = qkv_s[:, :, :, B_HEADS:]
    c = [cc[ib].reshape(DEC_BATCH, cc.shape[2], 2 * B_HEADS, B_HD) for cc in caches]
    os_ = _attn_sample(q_s, kv_new, c)
    yp, ys = _proj_residual(op.reshape(MP, B_WIDTH), os_.reshape(MS, B_WIDTH), b_w_out, ib, xp, xs,
                            name="b_out")
    new_p = []
    for g in range(B_GROUPS):
        keep = min(B_WINDOWS[g], SEQ)
        if g == B_GROUPS - 1:
            kv = kv_last[:, SEQ - keep:]
        else:
            kv = qkv_head[:, SEQ - keep:, (3 * g + 1) * B_WIDTH:(3 * g + 3) * B_WIDTH]
        new_p.append(kv.reshape(BATCH, keep, 2, B_HEADS, B_HD))
    new_s = [kv_new[g, :, :DEC_SEQ].reshape(DEC_BATCH, DEC_SEQ, 2, B_HEADS, B_HD)
             for g in range(B_GROUPS)]
    return yp, ys, new_p, new_s


def _mixer_c(xp, xs, norm_g, layer, ic, state, c_w, c_scale):
    yp, tail = _pool_prompt(xp.reshape(BATCH, SEQ, D_MODEL), norm_g, layer, c_w, c_scale, ic)
    yp = yp.reshape(MP, D_MODEL)
    state_pad = jnp.pad(state[ic], ((0, 0), (POOL_PAD - POOL_STATE, 0), (0, 0)))
    ys, seq = _pool_sample(xs, norm_g, layer, state_pad, c_w, c_scale, ic)
    pool_p = tail[:, POOL_PAD - POOL_STATE:]
    first = POOL_PAD + DEC_SEQ - POOL_STATE
    pool_s = seq[:, first:first + POOL_STATE]
    return yp, ys, pool_p, pool_s


def kernel(x_prompt, x_sample, cache_b_kv0, cache_b_kv1, cache_b_kv2, state_c_pool, norm_mix_g, norm_ffn_g, a_w_in, a_ln_g, a_ln_b, a_w_s, a_b_s, a_w_out, b_w_qkv, b_q_g, b_k_g, b_w_out, c_w, c_scale, ffn_w1, ffn_w3, ffn_w2):
    xp = x_prompt.reshape(MP, D_MODEL)
    xs = jnp.pad(x_sample, ((0, 0), (0, SAMPLE_PAD - DEC_SEQ), (0, 0))).reshape(MS, D_MODEL)
    a_v_s, pool_p, pool_s = [], [], []
    kv_p = [[] for _ in range(B_GROUPS)]
    kv_s = [[] for _ in range(B_GROUPS)]
    ia = ib = ic = 0
    for layer in range(DEPTH):
        kind = layer % N_MIXERS
        if kind == 0:
            xp, xs, v_s = _mixer_a(xp, xs, norm_mix_g, layer, ia, a_w_in, a_ln_g, a_ln_b, a_w_s, a_b_s, a_w_out)
            a_v_s.append(v_s.reshape(DEC_BATCH, SAMPLE_PAD, A_WIDTH)[:, :DEC_SEQ])
            ia += 1
        elif kind == 1:
            caches = (cache_b_kv0, cache_b_kv1, cache_b_kv2)
            xp, xs, kvp, kvs = _mixer_b(xp, xs, norm_mix_g, layer, ib, caches, b_w_qkv, b_q_g, b_k_g, b_w_out)
            for g in range(B_GROUPS):
                kv_p[g].append(kvp[g])
                kv_s[g].append(kvs[g])
            ib += 1
        else:
            xp, xs, pp, ps = _mixer_c(xp, xs, norm_mix_g, layer, ic, state_c_pool, c_w, c_scale)
            pool_p.append(pp)
            pool_s.append(ps)
            ic += 1
        xp, xs = _ffn(xp, xs, norm_ffn_g, ffn_w1, ffn_w3, ffn_w2, layer)
    y_prompt = xp.reshape(BATCH, SEQ, D_MODEL)
    y_sample = xs.reshape(DEC_BATCH, SAMPLE_PAD, D_MODEL)[:, :DEC_SEQ]
    return (y_prompt, y_sample, jnp.stack(a_v_s),
            jnp.stack(kv_p[0]), jnp.stack(kv_p[1]), jnp.stack(kv_p[2]),
            jnp.stack(kv_s[0]), jnp.stack(kv_s[1]), jnp.stack(kv_s[2]),
            jnp.stack(pool_p), jnp.stack(pool_s))
```

```python
import jax
import jax.numpy as jnp
from jax import lax
from jax.experimental import pallas as pl
from jax.experimental.pallas import tpu as pltpu

F32 = jnp.float32
BF16 = jnp.bfloat16

D_MODEL = 2048
BATCH = 4
SEQ = 2048
DEPTH = 4
DEC_BATCH = 8
DEC_SEQ = 4
PAST_LEN = 16384
N_MIXERS = 3
A_CHUNK = 128
A_WIDTH = D_MODEL
A_HEADS = 16
A_HD = A_WIDTH // A_HEADS
B_WINDOWS = (128, 512, 2048)
B_DILATIONS = (1, 4, 16)
B_GROUPS = 3
B_HD = 128
B_HEADS = D_MODEL // B_HD
B_WIDTH = B_HEADS * B_HD
B_SCALE = B_HD ** -0.5
B_BAND = 128
POOL_WINDOWS = (2, 4, 8, 16)
C_GROUPS = 4
C_GW = D_MODEL // C_GROUPS
POOL_STATE = max(POOL_WINDOWS) - 1
POOL_PAD = POOL_STATE + 1
D_FF = ((8 * D_MODEL + 3 * 256 - 1) // (3 * 256)) * 256
EPS = 1e-6

SUBLANES = 8
LANES = 128
VMEM_LIMIT_BYTES = 56 * 1024 * 1024

SAMPLE_PAD = SUBLANES
MP = BATCH * SEQ
MS = DEC_BATCH * SAMPLE_PAD
NEG_BIG = -1e30

TM = 1024
TN = 512
TF_FFN = 256
NORM_ROWS = 256
TAIL_ROWS = 256


def _params(semantics):
    return pltpu.CompilerParams(dimension_semantics=semantics,
                                vmem_limit_bytes=VMEM_LIMIT_BYTES)


def _dot(a, b):
    return jnp.dot(a, b, preferred_element_type=F32)


def _dot_nt(a, b):
    return lax.dot_general(a, b, (((1,), (1,)), ((), ())), preferred_element_type=F32)


def _rms_rows(x, g):
    r = lax.rsqrt(jnp.mean(x * x, axis=-1, keepdims=True) + EPS)
    return (x * r) * g


def _for_row_chunks(rows, chunk, fn):
    chunk = min(chunk, rows)
    n = rows // chunk
    if n == 1:
        fn(pl.ds(0, chunk))
        return

    def body(c, carry):
        fn(pl.ds(pl.multiple_of(c * chunk, chunk), chunk))
        return carry

    lax.fori_loop(0, n, body, 0)


def _layer_vec(stacked):
    return stacked.reshape(stacked.shape[0], 1, stacked.shape[1])


def _hold_after_first_tile(col_fn, last):
    return lambda i, j: (0, jnp.where(i == 0, col_fn(j), last))


def _ffn_body(x_ref, xs_ref, g_ref, w1_ref, w3_ref, w2_ref, o_ref, os_ref, hn_ref):
    i = pl.program_id(0)
    j = pl.program_id(1)
    tm = x_ref.shape[0]

    def gate(h, w1, w3):
        return (jax.nn.silu(_dot(h, w1)) * _dot(h, w3)).astype(BF16)

    def first_step(with_sample):
        w1 = w1_ref[...].astype(BF16)
        w3 = w3_ref[...].astype(BF16)
        w2 = w2_ref[...].astype(BF16)
        for c in range(tm // NORM_ROWS):
            rows = slice(c * NORM_ROWS, (c + 1) * NORM_ROWS)
            x = x_ref[rows, :]
            h = _rms_rows(x, g_ref[...]).astype(BF16)
            hn_ref[rows, :] = h
            o_ref[rows, :] = x + _dot(gate(h, w1, w3), w2)
        if with_sample:
            xs = xs_ref[...]
            h = _rms_rows(xs, g_ref[...]).astype(BF16)
            hn_ref[tm:, :] = h
            os_ref[...] = xs + _dot(gate(h, w1, w3), w2)

    def later_step(with_sample):
        w1 = w1_ref[...].astype(BF16)
        w3 = w3_ref[...].astype(BF16)
        w2 = w2_ref[...].astype(BF16)
        if with_sample:
            gt = gate(hn_ref[...], w1, w3)
            o_ref[...] += _dot(gt[:tm], w2)
            os_ref[...] += _dot(gt[tm:], w2)
        else:
            o_ref[...] += _dot(gate(hn_ref[0:tm, :], w1, w3), w2)

    for with_sample, tile_cond in ((True, i == 0), (False, i != 0)):
        for step_fn, col_cond in ((first_step, j == 0), (later_step, j != 0)):
            @pl.when(tile_cond & col_cond)
            def _(step_fn=step_fn, with_sample=with_sample):
                step_fn(with_sample)


def _ffn(x, xs, g_all, w1_all, w3_all, w2_all, layer):
    return pl.pallas_call(
        _ffn_body,
        grid=(MP // TM, D_FF // TF_FFN),
        in_specs=[
            pl.BlockSpec((TM, D_MODEL), lambda i, j: (i, 0)),
            pl.BlockSpec((MS, D_MODEL), lambda i, j: (0, 0)),
            pl.BlockSpec((None, 1, D_MODEL), lambda i, j: (layer, 0, 0)),
            pl.BlockSpec((None, D_MODEL, TF_FFN), lambda i, j: (layer, 0, j)),
            pl.BlockSpec((None, D_MODEL, TF_FFN), lambda i, j: (layer, 0, j)),
            pl.BlockSpec((None, TF_FFN, D_MODEL), lambda i, j: (layer, j, 0)),
        ],
        out_specs=[pl.BlockSpec((TM, D_MODEL), lambda i, j: (i, 0)),
                   pl.BlockSpec((MS, D_MODEL), lambda i, j: (0, 0))],
        out_shape=[jax.ShapeDtypeStruct((MP, D_MODEL), F32),
                   jax.ShapeDtypeStruct((MS, D_MODEL), F32)],
        scratch_shapes=[pltpu.VMEM((TM + MS, D_MODEL), BF16)],
        compiler_params=_params(("arbitrary", "arbitrary")),
        name="ffn",
    )(x, xs, _layer_vec(g_all), w1_all, w3_all, w2_all)


def _fill_lhs(h_ref, x_ref, xs_ref, g_ref, i, j):
    tm = x_ref.shape[0]

    def prep(x):
        if g_ref is not None:
            x = _rms_rows(x, g_ref[...])
        return x.astype(BF16)

    @pl.when(j == 0)
    def _():
        def chunk(rows):
            h_ref[rows, :] = prep(x_ref[rows, :])
        _for_row_chunks(tm, NORM_ROWS, chunk)

    @pl.when((i == 0) & (j == 0))
    def _():
        h_ref[tm:, :] = prep(xs_ref[...])


def _proj_residual_body(x_ref, xs_ref, w_ref, res_ref, res_s_ref, o_ref, os_ref, h_ref):
    i = pl.program_id(0)
    j = pl.program_id(1)
    tm = x_ref.shape[0]
    _fill_lhs(h_ref, x_ref, xs_ref, None, i, j)
    cols = pl.ds(pl.multiple_of(j * TN, TN), TN)

    @pl.when(i == 0)
    def _():
        acc = _dot(h_ref[...], w_ref[:, cols].astype(BF16))
        o_ref[...] = res_ref[...] + acc[:tm]
        os_ref[...] = res_s_ref[...] + acc[tm:]

    @pl.when(i != 0)
    def _():
        o_ref[...] = res_ref[...] + _dot(h_ref[0:tm, :], w_ref[:, cols].astype(BF16))


def _proj_residual(x, xs, w_all, layer, res, res_s, *, name):
    k = x.shape[1]
    n = w_all.shape[2]
    nj = n // TN
    return pl.pallas_call(
        _proj_residual_body,
        grid=(MP // TM, nj),
        in_specs=[pl.BlockSpec((TM, k), lambda i, j: (i, 0)),
                  pl.BlockSpec((MS, k), lambda i, j: (0, 0)),
                  pl.BlockSpec((None, k, n), lambda i, j: (layer, 0, 0), pipeline_mode=pl.Buffered(1)),
                  pl.BlockSpec((TM, TN), lambda i, j: (i, j)),
                  pl.BlockSpec((MS, TN), lambda i, j: (0, j))],
        out_specs=[pl.BlockSpec((TM, TN), lambda i, j: (i, j)),
                   pl.BlockSpec((MS, TN), _hold_after_first_tile(lambda j: j, nj - 1))],
        out_shape=[jax.ShapeDtypeStruct((MP, n), F32),
                   jax.ShapeDtypeStruct((MS, n), F32)],
        scratch_shapes=[pltpu.VMEM((TM + MS, k), BF16)],
        compiler_params=_params(("arbitrary", "arbitrary")),
        name=name,
    )(x, xs, w_all, res, res_s)


TN_QKV = 1024
QKV_TILES = B_WIDTH // TN_QKV
QKV_COLS = B_GROUPS * 3 * B_WIDTH
KV_W = 2 * B_WIDTH
QKV_HEAD_COLS = QKV_COLS - KV_W
MXU_COLS = 256
QKV_NJ = QKV_COLS // TN_QKV
QKV_LAST_KV = QKV_HEAD_COLS // TN_QKV


def _qkv_body(x_ref, xs_ref, g_ref, w_ref, gain_ref, o_ref, olast_ref, os_ref, h_ref):
    i = pl.program_id(0)
    j = pl.program_id(1)
    tm = x_ref.shape[0]
    _fill_lhs(h_ref, x_ref, xs_ref, g_ref, i, j)
    is_v = (j // QKV_TILES) % 3 == 2
    in_last = j >= QKV_LAST_KV

    def store_heads(acc, out_ref, c0, rows=slice(None)):
        for hh in range(MXU_COLS // B_HD):
            a = acc[:, hh * B_HD:(hh + 1) * B_HD]
            cs = slice(c0 + hh * B_HD, c0 + (hh + 1) * B_HD)
            r = lax.rsqrt(jnp.mean(a * a, axis=-1, keepdims=True) + EPS)
            r = jnp.where(is_v, 1.0, r)
            out_ref[rows, cs] = (a * r) * gain_ref[:, cs]

    def run(out_ref, with_sample):
        h = h_ref[...] if with_sample else h_ref[0:tm, :]
        for c0 in range(0, TN_QKV, MXU_COLS):
            w = w_ref[:, c0:c0 + MXU_COLS].astype(BF16)
            if with_sample:
                acc = _dot(h, w)
                store_heads(acc[:tm], out_ref, c0)
                store_heads(acc[tm:], os_ref, c0)
            elif c0 + MXU_COLS < TN_QKV:
                store_heads(_dot(h, w), out_ref, c0)
            else:
                for r0 in range(0, tm, TAIL_ROWS):
                    rows = slice(r0, r0 + TAIL_ROWS)
                    store_heads(_dot(h_ref[rows, :], w), out_ref, c0, rows)

    for with_sample, tile_cond in ((True, i == 0), (False, i != 0)):
        for out_ref, dest_cond in ((o_ref, jnp.logical_not(in_last)), (olast_ref, in_last)):
            @pl.when(tile_cond & dest_cond)
            def _(out_ref=out_ref, with_sample=with_sample):
                run(out_ref, with_sample)


def _qkv(x, xs, g_all, g_layer, w_all, layer, gain):
    def out_map(i, j):
        return i, jnp.minimum(j, QKV_LAST_KV - 1)

    def out_last_map(i, j):
        return i, jnp.maximum(j - QKV_LAST_KV, 0)

    return pl.pallas_call(
        _qkv_body,
        grid=(MP // TM, QKV_NJ),
        in_specs=[
            pl.BlockSpec((TM, D_MODEL), lambda i, j: (i, 0), pipeline_mode=pl.Buffered(1)),
            pl.BlockSpec((MS, D_MODEL), lambda i, j: (0, 0)),
            pl.BlockSpec((None, 1, D_MODEL), lambda i, j: (g_layer, 0, 0)),
            pl.BlockSpec((None, D_MODEL, TN_QKV), lambda i, j: (layer, 0, j)),
            pl.BlockSpec((1, TN_QKV), lambda i, j: (0, j)),
        ],
        out_specs=[pl.BlockSpec((TM, TN_QKV), out_map),
                   pl.BlockSpec((TM, TN_QKV), out_last_map),
                   pl.BlockSpec((MS, TN_QKV), _hold_after_first_tile(lambda j: j, QKV_NJ - 1))],
        out_shape=[jax.ShapeDtypeStruct((MP, QKV_HEAD_COLS), F32),
                   jax.ShapeDtypeStruct((MP, KV_W), F32),
                   jax.ShapeDtypeStruct((MS, QKV_COLS), F32)],
        scratch_shapes=[pltpu.VMEM((TM + MS, D_MODEL), BF16)],
        compiler_params=_params(("arbitrary", "arbitrary")),
        name="qkv",
    )(x, xs, _layer_vec(g_all), w_all, gain)


def _mixing_weights(wsm_ref, ws_ref, t_len, block):
    r = lax.broadcasted_iota(jnp.int32, (t_len, t_len), 0)
    c = lax.broadcasted_iota(jnp.int32, (t_len, t_len), 1)
    keep = c <= r
    if block < t_len:
        keep = keep & ((r // block) == (c // block)) & ((c % block) < DEC_SEQ)
    for h in range(A_HEADS):
        wsm_ref[h] = jnp.where(keep, ws_ref[h], 0.0).astype(BF16)


TN_A_IN = 1024
TN_A_OUT = 256
A_UV = A_WIDTH // TN_A_IN
A_P1 = 2 * A_UV
A_P2 = D_MODEL // TN_A_OUT


def _amix_prompt_body(x_ref, xs_ref, g_ref, win_ref, lng_ref, lnb_ref, ws_ref, bias_ref, wout_ref,
                      o_ref, zs_ref, h_ref, u_ref, v_ref, wsm_ref):
    i = pl.program_id(0)
    j = pl.program_id(1)
    tm = x_ref.shape[0]
    _fill_lhs(h_ref, x_ref, xs_ref, g_ref, i, j)

    def phase1(with_sample, dst_ref, slot):
        h = h_ref[...] if with_sample else h_ref[0:tm, :]
        for c0 in range(0, TN_A_IN, MXU_COLS):
            cs = slice(c0, c0 + MXU_COLS)
            w = win_ref[:, cs].astype(BF16)
            if with_sample:
                z = jax.nn.gelu(_dot(h, w), approximate=True)
                dst_ref[slot, :, cs] = z[:tm].astype(dst_ref.dtype)
                zs_ref[:, cs] = z[tm:]
            elif c0 + MXU_COLS < TN_A_IN:
                dst_ref[slot, :, cs] = jax.nn.gelu(_dot(h, w), approximate=True).astype(dst_ref.dtype)
            else:
                for r0 in range(0, tm, TAIL_ROWS):
                    rows = slice(r0, r0 + TAIL_ROWS)
                    z = jax.nn.gelu(_dot(h_ref[rows, :], w), approximate=True)
                    dst_ref[slot, rows, cs] = z.astype(dst_ref.dtype)

    for with_sample, tile_cond in ((True, i == 0), (False, i != 0)):
        for dst_ref, half_cond, slot in ((u_ref, j < A_UV, j),
                                         (v_ref, (j >= A_UV) & (j < A_P1), j - A_UV)):
            @pl.when(tile_cond & half_cond)
            def _(with_sample=with_sample, dst_ref=dst_ref, slot=slot):
                phase1(with_sample, dst_ref, slot)

    @pl.when(j == A_P1)
    def _():
        _mixing_weights(wsm_ref, ws_ref, A_CHUNK, A_CHUNK)

        def chunk(rows):
            vs = [v_ref[k, rows, :] for k in range(A_UV)]
            mu = sum(jnp.sum(vk, axis=-1, keepdims=True) for vk in vs) / A_WIDTH
            ds = [vk - mu for vk in vs]
            var = sum(jnp.sum(dk * dk, axis=-1, keepdims=True) for dk in ds) / A_WIDTH
            r = lax.rsqrt(var + EPS)
            for h in range(A_HEADS):
                k, c = divmod(h * A_HD, TN_A_IN)
                cs = slice(h * A_HD, (h + 1) * A_HD)
                vln = (ds[k][:, c:c + A_HD] * r) * lng_ref[:, cs] + lnb_ref[:, cs]
                mixed = _dot(wsm_ref[h], vln.astype(BF16)) + bias_ref[:, cs]
                h_ref[rows, cs] = (u_ref[k, rows, c:c + A_HD] * mixed).astype(BF16)
        _for_row_chunks(tm, A_CHUNK, chunk)

    @pl.when(j >= A_P1)
    def _():
        cols = pl.ds(pl.multiple_of((j - A_P1) * TN_A_OUT, TN_A_OUT), TN_A_OUT)
        o_ref[...] = x_ref[:, cols] + _dot(h_ref[0:tm, :], wout_ref[...].astype(BF16))


def _amix_prompt(x, xs, g_all, g_layer, win_all, lng_all, lnb_all, ws_all, bias, wout_all, layer):
    def out_col(j):
        return jnp.maximum(j - A_P1, 0)

    def z_col(j):
        return jnp.minimum(j, A_P1 - 1)

    return pl.pallas_call(
        _amix_prompt_body,
        grid=(MP // TM, A_P1 + A_P2),
        in_specs=[
            pl.BlockSpec((TM, D_MODEL), lambda i, j: (i, 0), pipeline_mode=pl.Buffered(1)),
            pl.BlockSpec((MS, D_MODEL), lambda i, j: (0, 0)),
            pl.BlockSpec((None, 1, D_MODEL), lambda i, j: (g_layer, 0, 0)),
            pl.BlockSpec((None, D_MODEL, TN_A_IN), lambda i, j: (layer, 0, z_col(j))),
            pl.BlockSpec((None, 1, A_WIDTH), lambda i, j: (layer, 0, 0)),
            pl.BlockSpec((None, 1, A_WIDTH), lambda i, j: (layer, 0, 0)),
            pl.BlockSpec((None, A_HEADS, A_CHUNK, A_CHUNK), lambda i, j: (layer, 0, 0, 0)),
            pl.BlockSpec((A_CHUNK, A_WIDTH), lambda i, j: (0, 0)),
            pl.BlockSpec((None, A_WIDTH, TN_A_OUT), lambda i, j: (layer, 0, out_col(j))),
        ],
        out_specs=[pl.BlockSpec((TM, TN_A_OUT), lambda i, j: (i, out_col(j))),
                   pl.BlockSpec((MS, TN_A_IN), _hold_after_first_tile(z_col, A_P1 - 1))],
        out_shape=[jax.ShapeDtypeStruct((MP, D_MODEL), F32),
                   jax.ShapeDtypeStruct((MS, 2 * A_WIDTH), F32)],
        scratch_shapes=[pltpu.VMEM((TM + MS, D_MODEL), BF16),
                        pltpu.VMEM((A_UV, TM, TN_A_IN), BF16),
                        pltpu.VMEM((A_UV, TM, TN_A_IN), F32),
                        pltpu.VMEM((A_HEADS, A_CHUNK, A_CHUNK), BF16)],
        compiler_params=_params(("arbitrary", "arbitrary")),
        name="amix_prompt",
    )(x, xs, _layer_vec(g_all), win_all, _layer_vec(lng_all), _layer_vec(lnb_all), ws_all, bias,
      wout_all)


def _amix_sample_body(x_ref, u_ref, v_ref, lng_ref, lnb_ref, ws_ref, bias_ref, wout_ref,
                      o_ref, vout_ref, gated_ref, wsm_ref):
    j = pl.program_id(0)

    @pl.when(j == 0)
    def _():
        _mixing_weights(wsm_ref, ws_ref, MS, SAMPLE_PAD)
        v = v_ref[...]
        mu = jnp.mean(v, axis=-1, keepdims=True)
        d = v - mu
        var = jnp.mean(d * d, axis=-1, keepdims=True)
        vln = (d * lax.rsqrt(var + EPS)) * lng_ref[...] + lnb_ref[...]
        vout_ref[...] = vln
        vb = vln.astype(BF16)
        for h in range(A_HEADS):
            cs = slice(h * A_HD, (h + 1) * A_HD)
            mixed = _dot(wsm_ref[h], vb[:, cs]) + bias_ref[:, cs]
            gated_ref[:, cs] = (u_ref[:, cs] * mixed).astype(BF16)

    o_ref[...] = x_ref[...] + _dot(gated_ref[...], wout_ref[...].astype(BF16))


def _amix_sample(xs, zs, lng_all, lnb_all, ws_s, bias_s, wout_all, layer):
    return pl.pallas_call(
        _amix_sample_body,
        grid=(D_MODEL // TN,),
        in_specs=[
            pl.BlockSpec((MS, TN), lambda j: (0, j)),
            pl.BlockSpec((MS, A_WIDTH), lambda j: (0, 0)),
            pl.BlockSpec((MS, A_WIDTH), lambda j: (0, 1)),
            pl.BlockSpec((None, 1, A_WIDTH), lambda j: (layer, 0, 0)),
            pl.BlockSpec((None, 1, A_WIDTH), lambda j: (layer, 0, 0)),
            pl.BlockSpec((A_HEADS, MS, MS), lambda j: (0, 0, 0)),
            pl.BlockSpec((MS, A_WIDTH), lambda j: (0, 0)),
            pl.BlockSpec((None, A_WIDTH, TN), lambda j: (layer, 0, j)),
        ],
        out_specs=[pl.BlockSpec((MS, TN), lambda j: (0, j)),
                   pl.BlockSpec((MS, A_WIDTH), lambda j: (0, 0))],
        out_shape=[jax.ShapeDtypeStruct((MS, D_MODEL), F32),
                   jax.ShapeDtypeStruct((MS, A_WIDTH), F32)],
        scratch_shapes=[pltpu.VMEM((MS, A_WIDTH), BF16),
                        pltpu.VMEM((A_HEADS, MS, MS), BF16)],
        compiler_params=_params(("arbitrary",)),
        name="amix_sample",
    )(xs, zs, zs, _layer_vec(lng_all), _layer_vec(lnb_all), ws_s, bias_s, wout_all)


ATTN_HEADS = 1


ATTN_STEP = B_DILATIONS[1]
assert B_DILATIONS == (1, ATTN_STEP, ATTN_STEP * ATTN_STEP)


def _attn_prompt_body(q0, k0, v0, q1, k1, v1, q2, k2, v2, o_ref,
                      acc_ref, m_ref, l_ref, acc4_ref, m4_ref, l4_ref, q4_ref, k4_ref, v4_ref):
    n = B_BAND
    step_d = ATTN_STEP
    seq4 = SEQ // step_d
    qi = lax.broadcasted_iota(jnp.int32, (n, 2 * n), 0)
    kj = lax.broadcasted_iota(jnp.int32, (n, 2 * n), 1)
    dist = n + qi - kj
    band_mask = (dist >= 0) & (dist <= n)
    qi1 = lax.broadcasted_iota(jnp.int32, (n, n), 0)
    kj1 = lax.broadcasted_iota(jnp.int32, (n, n), 1)
    causal_mask = kj1 <= qi1

    def attend(q, k, v, mask):
        s = jnp.where(mask, _dot_nt((q * B_SCALE).astype(BF16), k.astype(BF16)), NEG_BIG)
        m = jnp.max(s, axis=-1, keepdims=True)
        p = jnp.exp(s - m)
        vb = v.astype(BF16)
        pv = _dot(p.astype(BF16), jnp.concatenate([vb, jnp.ones_like(vb)], axis=1))
        return pv[:, :B_HD], m, pv[:, B_HD:]

    def merge(old, new):
        acc_old, m_old, l_old = old
        acc, m, l = new
        m_new = jnp.maximum(m_old, m)
        a_old = jnp.exp(m_old - m_new)
        a_new = jnp.exp(m - m_new)
        return acc_old * a_old + acc * a_new, m_new, l_old * a_old + l * a_new

    for hh in range(ATTN_HEADS):
        cs = slice(hh * B_HD, (hh + 1) * B_HD)

        for c in range(SEQ // n):
            rows = pl.ds(c * n, n)
            rows_k, mask = (rows, causal_mask) if c == 0 else (pl.ds((c - 1) * n, 2 * n), band_mask)
            acc, m, l = attend(q0[0, rows, cs], k0[0, rows_k, cs], v0[0, rows_k, cs], mask)
            acc_ref[rows, :] = acc
            m_ref[rows, :] = jnp.broadcast_to(m, (n, LANES))
            l_ref[rows, :] = l

        for r in range(step_d):
            for c in range(seq4 // n):
                rows = pl.ds(r + c * n * step_d, n, stride=step_d)
                if c == 0:
                    rows_k, mask = rows, causal_mask
                else:
                    rows_k, mask = pl.ds(r + (c - 1) * n * step_d, 2 * n, stride=step_d), band_mask
                new = attend(q1[0, rows, cs], k1[0, rows_k, cs], v1[0, rows_k, cs], mask)
                acc, m, l = merge((acc_ref[rows, :], m_ref[rows, :], l_ref[rows, :]), new)
                dst = pl.ds(c * n, n)
                acc4_ref[r, dst, :] = acc
                m4_ref[r, dst, :] = m
                l4_ref[r, dst, :] = l

        assert seq4 // step_d == n
        for r in range(step_d):
            rows_r = pl.ds(r, seq4, stride=step_d)
            q4_ref[r] = q2[0, rows_r, cs]
            k4_ref[r] = k2[0, rows_r, cs]
            v4_ref[r] = v2[0, rows_r, cs]
        for r in range(step_d):
            for r2 in range(step_d):
                rows = pl.ds(r2, n, stride=step_d)
                new = attend(q4_ref[r, rows, :], k4_ref[r, rows, :], v4_ref[r, rows, :], causal_mask)
                acc, _, l = merge((acc4_ref[r, rows, :], m4_ref[r, rows, :], l4_ref[r, rows, :]), new)
                acc4_ref[r, rows, :] = acc / l

        for r in range(step_d):
            acc_ref[pl.ds(r, seq4, stride=step_d), :] = acc4_ref[r]
        o_ref[0, :, cs] = acc_ref[...].astype(o_ref.dtype)


def _attn_prompt(qkv_head, kv_last):
    hw = ATTN_HEADS * B_HD

    def spec(first_col):
        return pl.BlockSpec((1, SEQ, hw), lambda b, h: (b, 0, first_col // hw + h))

    in_specs, args = [], []
    for g in range(B_GROUPS):
        c0 = g * 3 * B_WIDTH
        if g < B_GROUPS - 1:
            in_specs += [spec(c0), spec(c0 + B_WIDTH), spec(c0 + 2 * B_WIDTH)]
            args += [qkv_head] * 3
        else:
            in_specs += [spec(c0), spec(0), spec(B_WIDTH)]
            args += [qkv_head, kv_last, kv_last]
    return pl.pallas_call(
        _attn_prompt_body,
        grid=(BATCH, B_HEADS // ATTN_HEADS),
        in_specs=in_specs,
        out_specs=pl.BlockSpec((1, SEQ, hw), lambda b, h: (b, 0, h)),
        out_shape=jax.ShapeDtypeStruct((BATCH, SEQ, B_WIDTH), BF16),
        scratch_shapes=[pltpu.VMEM((SEQ, B_HD), F32)] * 3
                       + [pltpu.VMEM((ATTN_STEP, SEQ // ATTN_STEP, B_HD), F32)] * 6,
        compiler_params=_params(("parallel", "parallel")),
        name="attn_prompt",
    )(*args)


assert all(B_WINDOWS[g] == B_BAND * B_DILATIONS[g] for g in range(B_GROUPS))
assert PAST_LEN >= max(B_WINDOWS) and B_DILATIONS[0] == 1 and DEC_SEQ <= min(B_DILATIONS[1:])


def _attn_sample_body(q_ref, kvn_ref, c0_ref, c1_ref, c2_ref, o_ref):
    k_heads = slice(0, B_HEADS)
    v_heads = slice(B_HEADS, 2 * B_HEADS)

    def piece(q, k, v, valid=None):
        s = jnp.sum(k * q[None], axis=-1, keepdims=True) * B_SCALE
        if valid is not None:
            s = jnp.where(valid, s, NEG_BIG)
        m = jnp.max(s, axis=0)
        p = jnp.exp(s - m[None])
        return m, jnp.sum(p, axis=0), jnp.sum(p * v, axis=0)

    row0 = lax.broadcasted_iota(jnp.int32, (B_WINDOWS[0], B_HEADS, 1), 0)
    strided = (None, c1_ref, c2_ref)
    for t in range(DEC_SEQ):
        terms = []
        for g in range(B_GROUPS):
            q = q_ref[g, 0, t]
            if g == 0:
                valid = B_WINDOWS[0] + t - row0 <= B_BAND
                terms.append(piece(q, c0_ref[0, :, k_heads, :], c0_ref[0, :, v_heads, :], valid))
                new = slice(0, t + 1)
            else:
                c_ref = strided[g]
                terms.append(piece(q, c_ref[0, :, t, k_heads, :], c_ref[0, :, t, v_heads, :]))
                new = slice(t, t + 1)
            terms.append(piece(q, kvn_ref[g, 0, new, k_heads, :], kvn_ref[g, 0, new, v_heads, :]))
        m_all = terms[0][0]
        for m, _, _ in terms[1:]:
            m_all = jnp.maximum(m_all, m)
        l_all = jnp.zeros((B_HEADS, 1), F32)
        acc_all = jnp.zeros((B_HEADS, B_HD), F32)
        for m, l, acc in terms:
            w = jnp.exp(m - m_all)
            l_all = l_all + l * w
            acc_all = acc_all + acc * w
        o_ref[0, t] = acc_all / l_all
    o_ref[0, DEC_SEQ:] = jnp.zeros((SAMPLE_PAD - DEC_SEQ, B_HEADS, B_HD), F32)


def _attn_sample(q_s, kv_new, caches):
    kv_rows = 2 * B_HEADS

    def by_residue(g):
        dil = B_DILATIONS[g]
        view = caches[g].reshape(DEC_BATCH, B_BAND, dil, kv_rows, B_HD)
        return view, pl.BlockSpec((1, B_BAND, DEC_SEQ, kv_rows, B_HD), lambda b: (b, 0, 0, 0, 0))

    c1, c1_spec = by_residue(1)
    c2, c2_spec = by_residue(2)
    return pl.pallas_call(
        _attn_sample_body,
        grid=(DEC_BATCH,),
        in_specs=[
            pl.BlockSpec((B_GROUPS, 1, SAMPLE_PAD, B_HEADS, B_HD), lambda b: (0, b, 0, 0, 0)),
            pl.BlockSpec((B_GROUPS, 1, SAMPLE_PAD, kv_rows, B_HD), lambda b: (0, b, 0, 0, 0)),
            pl.BlockSpec((1, B_WINDOWS[0], kv_rows, B_HD), lambda b: (b, 0, 0, 0)),
            c1_spec, c2_spec,
        ],
        out_specs=pl.BlockSpec((1, SAMPLE_PAD, B_HEADS, B_HD), lambda b: (b, 0, 0, 0)),
        out_shape=jax.ShapeDtypeStruct((DEC_BATCH, SAMPLE_PAD, B_HEADS, B_HD), F32),
        compiler_params=_params(("parallel",)),
        name="attn_sample",
    )(q_s, kv_new, caches[0], c1, c2)


POOL_ROWS = 256
assert all(w & (w - 1) == 0 for w in POOL_WINDOWS)


def _pool_prompt_body(x_ref, g_ref, w_ref, scale_ref, o_ref, tail_ref, r_ref, sum_ref, z_ref):
    grp = pl.program_id(1)
    n_chunks = SEQ // POOL_ROWS

    @pl.when(grp == 0)
    def _():
        def chunk(rows):
            x = x_ref[0, rows, :]
            r_ref[rows, :] = lax.rsqrt(jnp.mean(x * x, axis=-1, keepdims=True) + EPS)
        _for_row_chunks(SEQ, NORM_ROWS, chunk)

    for gi, w in enumerate(POOL_WINDOWS):
        @pl.when(grp == gi)
        def _(gi=gi, w=w):
            cs = slice(gi * C_GW, (gi + 1) * C_GW)

            def normed(c):
                rows = slice(c * POOL_ROWS, (c + 1) * POOL_ROWS)
                return (x_ref[0, rows, cs] * r_ref[rows, :]) * g_ref[:, cs]

            sum_ref[0:POOL_PAD, :] = jnp.zeros((POOL_PAD, C_GW), F32)
            for c in range(n_chunks):
                h = normed(c)
                sum_ref[POOL_PAD + c * POOL_ROWS:POOL_PAD + (c + 1) * POOL_ROWS, :] = h
                if c == n_chunks - 1:
                    tail_ref[0, :, cs] = h[POOL_ROWS - POOL_PAD:]
            k = 1
            while k < w:
                for c in reversed(range(n_chunks)):
                    r0 = POOL_PAD + c * POOL_ROWS
                    sum_ref[r0:r0 + POOL_ROWS, :] = (sum_ref[r0:r0 + POOL_ROWS, :]
                                                    + sum_ref[r0 - k:r0 - k + POOL_ROWS, :])
                k *= 2
            for c in range(n_chunks):
                r0 = POOL_PAD + c * POOL_ROWS
                pos = c * POOL_ROWS + lax.broadcasted_iota(jnp.int32, (POOL_ROWS, 1), 0)
                cnt = jnp.minimum(w, pos + 1).astype(F32)
                z_ref[c * POOL_ROWS:(c + 1) * POOL_ROWS, :] = (
                    sum_ref[r0:r0 + POOL_ROWS, :] / cnt - normed(c)).astype(BF16)
            y = _dot(z_ref[...], w_ref[...].astype(BF16))
            o_ref[0] = x_ref[0, :, cs] + y * scale_ref[...]


def _pool_prompt(x, g_all, g_layer, w_all, scale_all, layer):
    return pl.pallas_call(
        _pool_prompt_body,
        grid=(BATCH, C_GROUPS),
        in_specs=[pl.BlockSpec((1, SEQ, D_MODEL), lambda b, g: (b, 0, 0)),
                  pl.BlockSpec((None, 1, D_MODEL), lambda b, g: (g_layer, 0, 0)),
                  pl.BlockSpec((None, None, C_GW, C_GW), lambda b, g: (layer, g, 0, 0)),
                  pl.BlockSpec((None, 1, C_GW), lambda b, g: (layer, 0, g))],
        out_specs=[pl.BlockSpec((1, SEQ, C_GW), lambda b, g: (b, 0, g)),
                   pl.BlockSpec((1, POOL_PAD, D_MODEL), lambda b, g: (b, 0, 0))],
        out_shape=[jax.ShapeDtypeStruct((BATCH, SEQ, D_MODEL), F32),
                   jax.ShapeDtypeStruct((BATCH, POOL_PAD, D_MODEL), F32)],
        scratch_shapes=[pltpu.VMEM((SEQ, 1), F32),
                        pltpu.VMEM((POOL_PAD + SEQ, C_GW), F32),
                        pltpu.VMEM((SEQ, C_GW), BF16)],
        compiler_params=_params(("arbitrary", "arbitrary")),
        name="pool_prompt",
    )(x, _layer_vec(g_all), w_all, _layer_vec(scale_all))


def _pool_sample_body(x_ref, g_ref, state_ref, w_ref, scale_ref, o_ref, seq_ref, z_ref):
    for b in range(DEC_BATCH):
        rows = slice(b * SAMPLE_PAD, (b + 1) * SAMPLE_PAD)
        seq_ref[b, 0:POOL_PAD, :] = state_ref[b]
        seq_ref[b, POOL_PAD:, :] = _rms_rows(x_ref[rows, :], g_ref[...])
    for b in range(DEC_BATCH):
        rows = slice(b * SAMPLE_PAD, (b + 1) * SAMPLE_PAD)
        for gi, w in enumerate(POOL_WINDOWS):
            cs = slice(gi * C_GW, (gi + 1) * C_GW)
            cur = seq_ref[b, POOL_PAD:POOL_PAD + SAMPLE_PAD, cs]
            tot = cur
            for k in range(1, w):
                tot = tot + seq_ref[b, POOL_PAD - k:POOL_PAD - k + SAMPLE_PAD, cs]
            pos = PAST_LEN + lax.broadcasted_iota(jnp.int32, (SAMPLE_PAD, 1), 0)
            cnt = jnp.minimum(w, pos + 1).astype(F32)
            z_ref[rows, cs] = (tot / cnt - cur).astype(BF16)
    for gi in range(C_GROUPS):
        cs = slice(gi * C_GW, (gi + 1) * C_GW)
        y = _dot(z_ref[:, cs], w_ref[gi].astype(BF16))
        o_ref[:, cs] = x_ref[:, cs] + y * scale_ref[:, cs]


def _pool_sample(xs, g_all, g_layer, state_pad, w_all, scale_all, layer):
    seq_rows = POOL_PAD + SAMPLE_PAD
    return pl.pallas_call(
        _pool_sample_body,
        grid=(1,),
        in_specs=[
            pl.BlockSpec((MS, D_MODEL), lambda i: (0, 0)),
            pl.BlockSpec((None, 1, D_MODEL), lambda i: (g_layer, 0, 0)),
            pl.BlockSpec((DEC_BATCH, POOL_PAD, D_MODEL), lambda i: (0, 0, 0)),
            pl.BlockSpec((None, C_GROUPS, C_GW, C_GW), lambda i: (layer, 0, 0, 0)),
            pl.BlockSpec((None, 1, D_MODEL), lambda i: (layer, 0, 0)),
        ],
        out_specs=[pl.BlockSpec((MS, D_MODEL), lambda i: (0, 0)),
                   pl.BlockSpec((DEC_BATCH, seq_rows, D_MODEL), lambda i: (0, 0, 0))],
        out_shape=[jax.ShapeDtypeStruct((MS, D_MODEL), F32),
                   jax.ShapeDtypeStruct((DEC_BATCH, seq_rows, D_MODEL), F32)],
        scratch_shapes=[pltpu.VMEM((MS, D_MODEL), BF16)],
        compiler_params=_params(("arbitrary",)),
        name="pool_sample",
    )(xs, _layer_vec(g_all), state_pad, w_all, _layer_vec(scale_all))


def _mixer_a(xp, xs, norm_g, layer, ia, a_w_in, a_ln_g, a_ln_b, a_w_s, a_b_s, a_w_out):
    b_s = a_b_s[ia]
    bias_p = jnp.repeat(jnp.transpose(b_s), A_HD, axis=1)
    yp, zs = _amix_prompt(xp, xs, norm_g, layer, a_w_in, a_ln_g, a_ln_b, a_w_s, bias_p, a_w_out, ia)
    ws_s = jnp.tile(a_w_s[ia][:, :SAMPLE_PAD, :SAMPLE_PAD], (1, DEC_BATCH, DEC_BATCH))
    bias_s = jnp.tile(jnp.repeat(jnp.transpose(b_s[:, :SAMPLE_PAD]), A_HD, axis=1), (DEC_BATCH, 1))
    ys, v_s = _amix_sample(xs, zs, a_ln_g, a_ln_b, ws_s, bias_s, a_w_out, ia)
    return yp, ys, v_s


def _mixer_b(xp, xs, norm_g, layer, ib, caches, b_w_qkv, b_q_g, b_k_g, b_w_out):
    ones = jnp.ones((B_GROUPS, B_WIDTH), F32)
    gain = jnp.stack([jnp.tile(b_q_g[ib], (1, B_HEADS)), jnp.tile(b_k_g[ib], (1, B_HEADS)), ones], axis=1)
    gain = gain.reshape(1, QKV_COLS)
    qkv_head, kv_last, qkv_s = _qkv(xp, xs, norm_g, layer, b_w_qkv, ib, gain)
    qkv_head = qkv_head.reshape(BATCH, SEQ, QKV_HEAD_COLS)
    kv_last = kv_last.reshape(BATCH, SEQ, KV_W)
    op = _attn_prompt(qkv_head, kv_last)
    qkv_s = jnp.transpose(qkv_s.reshape(DEC_BATCH, SAMPLE_PAD, B_GROUPS, 3 * B_HEADS, B_HD), (2, 0, 1, 3, 4))
    q_s = qkv_s[:, :, :, :B_HEADS]
    kv_new = qkv_s[:, :, :, B_HEADS:]
    c = [cc[ib].reshape(DEC_BATCH, cc.shape[2], 2 * B_HEADS, B_HD) for cc in caches]
    os_ = _attn_sample(q_s, kv_new, c)
    yp, ys = _proj_residual(op.reshape(MP, B_WIDTH), os_.reshape(MS, B_WIDTH), b_w_out, ib, xp, xs,
                            name="b_out")
    new_p = []
    for g in range(B_GROUPS):
        keep = min(B_WINDOWS[g], SEQ)
        if g == B_GROUPS - 1:
            kv = kv_last[:, SEQ - keep:]
        else:
            kv = qkv_head[:, SEQ - keep:, (3 * g + 1) * B_WIDTH:(3 * g + 3) * B_WIDTH]
        new_p.append(kv.reshape(BATCH, keep, 2, B_HEADS, B_HD))
    new_s = [kv_new[g, :, :DEC_SEQ].reshape(DEC_BATCH, DEC_SEQ, 2, B_HEADS, B_HD)
             for g in range(B_GROUPS)]
    return yp, ys, new_p, new_s


def _mixer_c(xp, xs, norm_g, layer, ic, state, c_w, c_scale):
    yp, tail = _pool_prompt(xp.reshape(BATCH, SEQ, D_MODEL), norm_g, layer, c_w, c_scale, ic)
    yp = yp.reshape(MP, D_MODEL)
    state_pad = jnp.pad(state[ic], ((0, 0), (POOL_PAD - POOL_STATE, 0), (0, 0)))
    ys, seq = _pool_sample(xs, norm_g, layer, state_pad, c_w, c_scale, ic)
    pool_p = tail[:, POOL_PAD - POOL_STATE:]
    first = POOL_PAD + DEC_SEQ - POOL_STATE
    pool_s = seq[:, first:first + POOL_STATE]
    return yp, ys, pool_p, pool_s


def kernel(x_prompt, x_sample, cache_b_kv0, cache_b_kv1, cache_b_kv2, state_c_pool, norm_mix_g, norm_ffn_g, a_w_in, a_ln_g, a_ln_b, a_w_s, a_b_s, a_w_out, b_w_qkv, b_q_g, b_k_g, b_w_out, c_w, c_scale, ffn_w1, ffn_w3, ffn_w2):
    xp = x_prompt.reshape(MP, D_MODEL)
    xs = jnp.pad(x_sample, ((0, 0), (0, SAMPLE_PAD - DEC_SEQ), (0, 0))).reshape(MS, D_MODEL)
    a_v_s, pool_p, pool_s = [], [], []
    kv_p = [[] for _ in range(B_GROUPS)]
    kv_s = [[] for _ in range(B_GROUPS)]
    ia = ib = ic = 0
    for layer in range(DEPTH):
        kind = layer % N_MIXERS
        if kind == 0:
            xp, xs, v_s = _mixer_a(xp, xs, norm_mix_g, layer, ia, a_w_in, a_ln_g, a_ln_b, a_w_s, a_b_s, a_w_out)
            a_v_s.append(v_s.reshape(DEC_BATCH, SAMPLE_PAD, A_WIDTH)[:, :DEC_SEQ])
            ia += 1
        elif kind == 1:
            caches = (cache_b_kv0, cache_b_kv1, cache_b_kv2)
            xp, xs, kvp, kvs = _mixer_b(xp, xs, norm_mix_g, layer, ib, caches, b_w_qkv, b_q_g, b_k_g, b_w_out)
            for g in range(B_GROUPS):
                kv_p[g].append(kvp[g])
                kv_s[g].append(kvs[g])
            ib += 1
        else:
            xp, xs, pp, ps = _mixer_c(xp, xs, norm_mix_g, layer, ic, state_c_pool, c_w, c_scale)
            pool_p.append(pp)
            pool_s.append(ps)
            ic += 1
        xp, xs = _ffn(xp, xs, norm_ffn_g, ffn_w1, ffn_w3, ffn_w2, layer)
    y_prompt = xp.reshape(BATCH, SEQ, D_MODEL)
    y_sample = xs.reshape(DEC_BATCH, SAMPLE_PAD, D_MODEL)[:, :DEC_SEQ]
    return (y_prompt, y_sample, jnp.stack(a_v_s),
            jnp.stack(kv_p[0]), jnp.stack(kv_p[1]), jnp.stack(kv_p[2]),
            jnp.stack(kv_s[0]), jnp.stack(kv_s[1]), jnp.stack(kv_s[2]),
            jnp.stack(pool_p), jnp.stack(pool_s))
```

```python
import jax
import jax.numpy as jnp
from jax import lax
from jax.experimental import pallas as pl
from jax.experimental.pallas import tpu as pltpu

F32 = jnp.float32
BF16 = jnp.bfloat16

D_MODEL = 2048
BATCH = 4
SEQ = 2048
DEPTH = 4
DEC_BATCH = 8
DEC_SEQ = 4
PAST_LEN = 16384
N_MIXERS = 3
A_CHUNK = 128
A_WIDTH = D_MODEL
A_HEADS = 16
A_HD = A_WIDTH // A_HEADS
B_WINDOWS = (128, 512, 2048)
B_DILATIONS = (1, 4, 16)
B_GROUPS = 3
B_HD = 128
B_HEADS = D_MODEL // B_HD
B_WIDTH = B_HEADS * B_HD
B_SCALE = B_HD ** -0.5
B_BAND = 128
POOL_WINDOWS = (2, 4, 8, 16)
C_GROUPS = 4
C_GW = D_MODEL // C_GROUPS
POOL_STATE = max(POOL_WINDOWS) - 1
POOL_PAD = POOL_STATE + 1
D_FF = ((8 * D_MODEL + 3 * 256 - 1) // (3 * 256)) * 256
EPS = 1e-6

SUBLANES = 8
LANES = 128
VMEM_LIMIT_BYTES = 56 * 1024 * 1024

SAMPLE_PAD = SUBLANES
MP = BATCH * SEQ
MS = DEC_BATCH * SAMPLE_PAD
NEG_BIG = -1e30

TM = 1024
TN = 512
TF_FFN = 256
NORM_ROWS = 256
TAIL_ROWS = 256


def _params(semantics):
    return pltpu.CompilerParams(dimension_semantics=semantics,
                                vmem_limit_bytes=VMEM_LIMIT_BYTES)


def _dot(a, b):
    return jnp.dot(a, b, preferred_element_type=F32)


def _dot_nt(a, b):
    return lax.dot_general(a, b, (((1,), (1,)), ((), ())), preferred_element_type=F32)


def _rms_rows(x, g):
    r = lax.rsqrt(jnp.mean(x * x, axis=-1, keepdims=True) + EPS)
    return (x * r) * g


def _for_row_chunks(rows, chunk, fn):
    chunk = min(chunk, rows)
    n = rows // chunk
    if n == 1:
        fn(pl.ds(0, chunk))
        return

    def body(c, carry):
        fn(pl.ds(pl.multiple_of(c * chunk, chunk), chunk))
        return carry

    lax.fori_loop(0, n, body, 0)


def _layer_vec(stacked):
    return stacked.reshape(stacked.shape[0], 1, stacked.shape[1])


def _hold_after_first_tile(col_fn, last):
    return lambda i, j: (0, jnp.where(i == 0, col_fn(j), last))


def _ffn_body(x_ref, xs_ref, g_ref, w1_ref, w3_ref, w2_ref, o_ref, os_ref, hn_ref):
    i = pl.program_id(0)
    j = pl.program_id(1)
    tm = x_ref.shape[0]

    def gate(h, w1, w3):
        return (jax.nn.silu(_dot(h, w1)) * _dot(h, w3)).astype(BF16)

    def first_step(with_sample):
        w1 = w1_ref[...].astype(BF16)
        w3 = w3_ref[...].astype(BF16)
        w2 = w2_ref[...].astype(BF16)
        for c in range(tm // NORM_ROWS):
            rows = slice(c * NORM_ROWS, (c + 1) * NORM_ROWS)
            x = x_ref[rows, :]
            h = _rms_rows(x, g_ref[...]).astype(BF16)
            hn_ref[rows, :] = h
            o_ref[rows, :] = x + _dot(gate(h, w1, w3), w2)
        if with_sample:
            xs = xs_ref[...]
            h = _rms_rows(xs, g_ref[...]).astype(BF16)
            hn_ref[tm:, :] = h
            os_ref[...] = xs + _dot(gate(h, w1, w3), w2)

    def later_step(with_sample):
        w1 = w1_ref[...].astype(BF16)
        w3 = w3_ref[...].astype(BF16)
        w2 = w2_ref[...].astype(BF16)
        if with_sample:
            gt = gate(hn_ref[...], w1, w3)
            o_ref[...] += _dot(gt[:tm], w2)
            os_ref[...] += _dot(gt[tm:], w2)
        else:
            o_ref[...] += _dot(gate(hn_ref[0:tm, :], w1, w3), w2)

    for with_sample, tile_cond in ((True, i == 0), (False, i != 0)):
        for step_fn, col_cond in ((first_step, j == 0), (later_step, j != 0)):
            @pl.when(tile_cond & col_cond)
            def _(step_fn=step_fn, with_sample=with_sample):
                step_fn(with_sample)


def _ffn(x, xs, g_all, w1_all, w3_all, w2_all, layer):
    return pl.pallas_call(
        _ffn_body,
        grid=(MP // TM, D_FF // TF_FFN),
        in_specs=[
            pl.BlockSpec((TM, D_MODEL), lambda i, j: (i, 0)),
            pl.BlockSpec((MS, D_MODEL), lambda i, j: (0, 0)),
            pl.BlockSpec((None, 1, D_MODEL), lambda i, j: (layer, 0, 0)),
            pl.BlockSpec((None, D_MODEL, TF_FFN), lambda i, j: (layer, 0, j)),
            pl.BlockSpec((None, D_MODEL, TF_FFN), lambda i, j: (layer, 0, j)),
            pl.BlockSpec((None, TF_FFN, D_MODEL), lambda i, j: (layer, j, 0)),
        ],
        out_specs=[pl.BlockSpec((TM, D_MODEL), lambda i, j: (i, 0)),
                   pl.BlockSpec((MS, D_MODEL), lambda i, j: (0, 0))],
        out_shape=[jax.ShapeDtypeStruct((MP, D_MODEL), F32),
                   jax.ShapeDtypeStruct((MS, D_MODEL), F32)],
        scratch_shapes=[pltpu.VMEM((TM + MS, D_MODEL), BF16)],
        compiler_params=_params(("arbitrary", "arbitrary")),
        name="ffn",
    )(x, xs, _layer_vec(g_all), w1_all, w3_all, w2_all)


def _fill_lhs(h_ref, x_ref, xs_ref, g_ref, i, j):
    tm = x_ref.shape[0]

    def prep(x):
        if g_ref is not None:
            x = _rms_rows(x, g_ref[...])
        return x.astype(BF16)

    @pl.when(j == 0)
    def _():
        def chunk(rows):
            h_ref[rows, :] = prep(x_ref[rows, :])
        _for_row_chunks(tm, NORM_ROWS, chunk)

    @pl.when((i == 0) & (j == 0))
    def _():
        h_ref[tm:, :] = prep(xs_ref[...])


def _proj_residual_body(x_ref, xs_ref, w_ref, res_ref, res_s_ref, o_ref, os_ref, h_ref):
    i = pl.program_id(0)
    j = pl.program_id(1)
    tm = x_ref.shape[0]
    _fill_lhs(h_ref, x_ref, xs_ref, None, i, j)
    cols = pl.ds(pl.multiple_of(j * TN, TN), TN)

    @pl.when(i == 0)
    def _():
        acc = _dot(h_ref[...], w_ref[:, cols].astype(BF16))
        o_ref[...] = res_ref[...] + acc[:tm]
        os_ref[...] = res_s_ref[...] + acc[tm:]

    @pl.when(i != 0)
    def _():
        o_ref[...] = res_ref[...] + _dot(h_ref[0:tm, :], w_ref[:, cols].astype(BF16))


def _proj_residual(x, xs, w_all, layer, res, res_s, *, name):
    k = x.shape[1]
    n = w_all.shape[2]
    nj = n // TN
    return pl.pallas_call(
        _proj_residual_body,
        grid=(MP // TM, nj),
        in_specs=[pl.BlockSpec((TM, k), lambda i, j: (i, 0)),
                  pl.BlockSpec((MS, k), lambda i, j: (0, 0)),
                  pl.BlockSpec((None, k, n), lambda i, j: (layer, 0, 0), pipeline_mode=pl.Buffered(1)),
                  pl.BlockSpec((TM, TN), lambda i, j: (i, j)),
                  pl.BlockSpec((MS, TN), lambda i, j: (0, j))],
        out_specs=[pl.BlockSpec((TM, TN), lambda i, j: (i, j)),
                   pl.BlockSpec((MS, TN), _hold_after_first_tile(lambda j: j, nj - 1))],
        out_shape=[jax.ShapeDtypeStruct((MP, n), F32),
                   jax.ShapeDtypeStruct((MS, n), F32)],
        scratch_shapes=[pltpu.VMEM((TM + MS, k), BF16)],
        compiler_params=_params(("arbitrary", "arbitrary")),
        name=name,
    )(x, xs, w_all, res, res_s)


TN_QKV = 1024
QKV_TILES = B_WIDTH // TN_QKV
QKV_COLS = B_GROUPS * 3 * B_WIDTH
KV_W = 2 * B_WIDTH
QKV_HEAD_COLS = QKV_COLS - KV_W
MXU_COLS = 256
QKV_NJ = QKV_COLS // TN_QKV
QKV_LAST_KV = QKV_HEAD_COLS // TN_QKV


def _qkv_body(x_hbm, xs_ref, g_ref, w_ref, gain_ref, o_ref, olast_ref, os_ref, h_ref, x_ref, x_sem):
    i = pl.program_id(0)
    j = pl.program_id(1)
    tm = x_ref.shape[0]

    def x_copy(tile):
        return pltpu.make_async_copy(x_hbm.at[pl.ds(tile * tm, tm), :], x_ref, x_sem)

    @pl.when((i == 0) & (j == 0))
    def _():
        x_copy(0).start()

    @pl.when(j == 0)
    def _():
        x_copy(i).wait()

    _fill_lhs(h_ref, x_ref, xs_ref, g_ref, i, j)

    @pl.when((j == 1) & (i + 1 < pl.num_programs(0)))
    def _():
        x_copy(i + 1).start()

    is_v = (j // QKV_TILES) % 3 == 2
    in_last = j >= QKV_LAST_KV

    def store_heads(acc, out_ref, c0, rows=slice(None)):
        for hh in range(MXU_COLS // B_HD):
            a = acc[:, hh * B_HD:(hh + 1) * B_HD]
            cs = slice(c0 + hh * B_HD, c0 + (hh + 1) * B_HD)
            r = lax.rsqrt(jnp.mean(a * a, axis=-1, keepdims=True) + EPS)
            r = jnp.where(is_v, 1.0, r)
            out_ref[rows, cs] = (a * r) * gain_ref[:, cs]

    def run(out_ref, with_sample):
        h = h_ref[...] if with_sample else h_ref[0:tm, :]
        for c0 in range(0, TN_QKV, MXU_COLS):
            w = w_ref[:, c0:c0 + MXU_COLS].astype(BF16)
            if with_sample:
                acc = _dot(h, w)
                store_heads(acc[:tm], out_ref, c0)
                store_heads(acc[tm:], os_ref, c0)
            elif c0 + MXU_COLS < TN_QKV:
                store_heads(_dot(h, w), out_ref, c0)
            else:
                for r0 in range(0, tm, TAIL_ROWS):
                    rows = slice(r0, r0 + TAIL_ROWS)
                    store_heads(_dot(h_ref[rows, :], w), out_ref, c0, rows)

    for with_sample, tile_cond in ((True, i == 0), (False, i != 0)):
        for out_ref, dest_cond in ((o_ref, jnp.logical_not(in_last)), (olast_ref, in_last)):
            @pl.when(tile_cond & dest_cond)
            def _(out_ref=out_ref, with_sample=with_sample):
                run(out_ref, with_sample)


def _qkv(x, xs, g_all, g_layer, w_all, layer, gain):
    def out_map(i, j):
        return i, jnp.minimum(j, QKV_LAST_KV - 1)

    def out_last_map(i, j):
        return i, jnp.maximum(j - QKV_LAST_KV, 0)

    return pl.pallas_call(
        _qkv_body,
        grid=(MP // TM, QKV_NJ),
        in_specs=[
            pl.BlockSpec(memory_space=pl.ANY),
            pl.BlockSpec((MS, D_MODEL), lambda i, j: (0, 0)),
            pl.BlockSpec((None, 1, D_MODEL), lambda i, j: (g_layer, 0, 0)),
            pl.BlockSpec((None, D_MODEL, TN_QKV), lambda i, j: (layer, 0, j)),
            pl.BlockSpec((1, TN_QKV), lambda i, j: (0, j)),
        ],
        out_specs=[pl.BlockSpec((TM, TN_QKV), out_map),
                   pl.BlockSpec((TM, TN_QKV), out_last_map),
                   pl.BlockSpec((MS, TN_QKV), _hold_after_first_tile(lambda j: j, QKV_NJ - 1))],
        out_shape=[jax.ShapeDtypeStruct((MP, QKV_HEAD_COLS), F32),
                   jax.ShapeDtypeStruct((MP, KV_W), F32),
                   jax.ShapeDtypeStruct((MS, QKV_COLS), F32)],
        scratch_shapes=[pltpu.VMEM((TM + MS, D_MODEL), BF16),
                        pltpu.VMEM((TM, D_MODEL), F32),
                        pltpu.SemaphoreType.DMA(())],
        compiler_params=_params(("arbitrary", "arbitrary")),
        name="qkv",
    )(x, xs, _layer_vec(g_all), w_all, gain)


def _mixing_weights(wsm_ref, ws_ref, t_len, block):
    r = lax.broadcasted_iota(jnp.int32, (t_len, t_len), 0)
    c = lax.broadcasted_iota(jnp.int32, (t_len, t_len), 1)
    keep = c <= r
    if block < t_len:
        keep = keep & ((r // block) == (c // block)) & ((c % block) < DEC_SEQ)
    for h in range(A_HEADS):
        wsm_ref[h] = jnp.where(keep, ws_ref[h], 0.0).astype(BF16)


TN_A_IN = 1024
TN_A_OUT = 256
A_UV = A_WIDTH // TN_A_IN
A_P1 = 2 * A_UV
A_P2 = D_MODEL // TN_A_OUT


def _amix_prompt_body(x_ref, xs_ref, g_ref, win_ref, lng_ref, lnb_ref, ws_ref, bias_ref, wout_ref,
                      o_ref, zs_ref, h_ref, u_ref, v_ref, wsm_ref):
    i = pl.program_id(0)
    j = pl.program_id(1)
    tm = x_ref.shape[0]
    _fill_lhs(h_ref, x_ref, xs_ref, g_ref, i, j)

    def phase1(with_sample, dst_ref, slot):
        h = h_ref[...] if with_sample else h_ref[0:tm, :]
        for c0 in range(0, TN_A_IN, MXU_COLS):
            cs = slice(c0, c0 + MXU_COLS)
            w = win_ref[:, cs].astype(BF16)
            if with_sample:
                z = jax.nn.gelu(_dot(h, w), approximate=True)
                dst_ref[slot, :, cs] = z[:tm].astype(dst_ref.dtype)
                zs_ref[:, cs] = z[tm:]
            elif c0 + MXU_COLS < TN_A_IN:
                dst_ref[slot, :, cs] = jax.nn.gelu(_dot(h, w), approximate=True).astype(dst_ref.dtype)
            else:
                for r0 in range(0, tm, TAIL_ROWS):
                    rows = slice(r0, r0 + TAIL_ROWS)
                    z = jax.nn.gelu(_dot(h_ref[rows, :], w), approximate=True)
                    dst_ref[slot, rows, cs] = z.astype(dst_ref.dtype)

    for with_sample, tile_cond in ((True, i == 0), (False, i != 0)):
        for dst_ref, half_cond, slot in ((u_ref, j < A_UV, j),
                                         (v_ref, (j >= A_UV) & (j < A_P1), j - A_UV)):
            @pl.when(tile_cond & half_cond)
            def _(with_sample=with_sample, dst_ref=dst_ref, slot=slot):
                phase1(with_sample, dst_ref, slot)

    @pl.when(j == A_P1)
    def _():
        _mixing_weights(wsm_ref, ws_ref, A_CHUNK, A_CHUNK)

        def chunk(rows):
            vs = [v_ref[k, rows, :] for k in range(A_UV)]
            mu = sum(jnp.sum(vk, axis=-1, keepdims=True) for vk in vs) / A_WIDTH
            ds = [vk - mu for vk in vs]
            var = sum(jnp.sum(dk * dk, axis=-1, keepdims=True) for dk in ds) / A_WIDTH
            r = lax.rsqrt(var + EPS)
            for h in range(A_HEADS):
                k, c = divmod(h * A_HD, TN_A_IN)
                cs = slice(h * A_HD, (h + 1) * A_HD)
                vln = (ds[k][:, c:c + A_HD] * r) * lng_ref[:, cs] + lnb_ref[:, cs]
                mixed = _dot(wsm_ref[h], vln.astype(BF16)) + bias_ref[:, cs]
                h_ref[rows, cs] = (u_ref[k, rows, c:c + A_HD] * mixed).astype(BF16)
        _for_row_chunks(tm, A_CHUNK, chunk)

    @pl.when(j >= A_P1)
    def _():
        cols = pl.ds(pl.multiple_of((j - A_P1) * TN_A_OUT, TN_A_OUT), TN_A_OUT)
        o_ref[...] = x_ref[:, cols] + _dot(h_ref[0:tm, :], wout_ref[...].astype(BF16))


def _amix_prompt(x, xs, g_all, g_layer, win_all, lng_all, lnb_all, ws_all, bias, wout_all, layer):
    def out_col(j):
        return jnp.maximum(j - A_P1, 0)

    def z_col(j):
        return jnp.minimum(j, A_P1 - 1)

    return pl.pallas_call(
        _amix_prompt_body,
        grid=(MP // TM, A_P1 + A_P2),
        in_specs=[
            pl.BlockSpec((TM, D_MODEL), lambda i, j: (i, 0), pipeline_mode=pl.Buffered(1)),
            pl.BlockSpec((MS, D_MODEL), lambda i, j: (0, 0)),
            pl.BlockSpec((None, 1, D_MODEL), lambda i, j: (g_layer, 0, 0)),
            pl.BlockSpec((None, D_MODEL, TN_A_IN), lambda i, j: (layer, 0, z_col(j))),
            pl.BlockSpec((None, 1, A_WIDTH), lambda i, j: (layer, 0, 0)),
            pl.BlockSpec((None, 1, A_WIDTH), lambda i, j: (layer, 0, 0)),
            pl.BlockSpec((None, A_HEADS, A_CHUNK, A_CHUNK), lambda i, j: (layer, 0, 0, 0)),
            pl.BlockSpec((A_CHUNK, A_WIDTH), lambda i, j: (0, 0)),
            pl.BlockSpec((None, A_WIDTH, TN_A_OUT), lambda i, j: (layer, 0, out_col(j))),
        ],
        out_specs=[pl.BlockSpec((TM, TN_A_OUT), lambda i, j: (i, out_col(j))),
                   pl.BlockSpec((MS, TN_A_IN), _hold_after_first_tile(z_col, A_P1 - 1))],
        out_shape=[jax.ShapeDtypeStruct((MP, D_MODEL), F32),
                   jax.ShapeDtypeStruct((MS, 2 * A_WIDTH), F32)],
        scratch_shapes=[pltpu.VMEM((TM + MS, D_MODEL), BF16),
                        pltpu.VMEM((A_UV, TM, TN_A_IN), BF16),
                        pltpu.VMEM((A_UV, TM, TN_A_IN), F32),
                        pltpu.VMEM((A_HEADS, A_CHUNK, A_CHUNK), BF16)],
        compiler_params=_params(("arbitrary", "arbitrary")),
        name="amix_prompt",
    )(x, xs, _layer_vec(g_all), win_all, _layer_vec(lng_all), _layer_vec(lnb_all), ws_all, bias,
      wout_all)


def _amix_sample_body(x_ref, u_ref, v_ref, lng_ref, lnb_ref, ws_ref, bias_ref, wout_ref,
                      o_ref, vout_ref, gated_ref, wsm_ref):
    j = pl.program_id(0)

    @pl.when(j == 0)
    def _():
        _mixing_weights(wsm_ref, ws_ref, MS, SAMPLE_PAD)
        v = v_ref[...]
        mu = jnp.mean(v, axis=-1, keepdims=True)
        d = v - mu
        var = jnp.mean(d * d, axis=-1, keepdims=True)
        vln = (d * lax.rsqrt(var + EPS)) * lng_ref[...] + lnb_ref[...]
        vout_ref[...] = vln
        vb = vln.astype(BF16)
        for h in range(A_HEADS):
            cs = slice(h * A_HD, (h + 1) * A_HD)
            mixed = _dot(wsm_ref[h], vb[:, cs]) + bias_ref[:, cs]
            gated_ref[:, cs] = (u_ref[:, cs] * mixed).astype(BF16)

    o_ref[...] = x_ref[...] + _dot(gated_ref[...], wout_ref[...].astype(BF16))


def _amix_sample(xs, zs, lng_all, lnb_all, ws_s, bias_s, wout_all, layer):
    return pl.pallas_call(
        _amix_sample_body,
        grid=(D_MODEL // TN,),
        in_specs=[
            pl.BlockSpec((MS, TN), lambda j: (0, j)),
            pl.BlockSpec((MS, A_WIDTH), lambda j: (0, 0)),
            pl.BlockSpec((MS, A_WIDTH), lambda j: (0, 1)),
            pl.BlockSpec((None, 1, A_WIDTH), lambda j: (layer, 0, 0)),
            pl.BlockSpec((None, 1, A_WIDTH), lambda j: (layer, 0, 0)),
            pl.BlockSpec((A_HEADS, MS, MS), lambda j: (0, 0, 0)),
            pl.BlockSpec((MS, A_WIDTH), lambda j: (0, 0)),
            pl.BlockSpec((None, A_WIDTH, TN), lambda j: (layer, 0, j)),
        ],
        out_specs=[pl.BlockSpec((MS, TN), lambda j: (0, j)),
                   pl.BlockSpec((MS, A_WIDTH), lambda j: (0, 0))],
        out_shape=[jax.ShapeDtypeStruct((MS, D_MODEL), F32),
                   jax.ShapeDtypeStruct((MS, A_WIDTH), F32)],
        scratch_shapes=[pltpu.VMEM((MS, A_WIDTH), BF16),
                        pltpu.VMEM((A_HEADS, MS, MS), BF16)],
        compiler_params=_params(("arbitrary",)),
        name="amix_sample",
    )(xs, zs, zs, _layer_vec(lng_all), _layer_vec(lnb_all), ws_s, bias_s, wout_all)


ATTN_HEADS = 1


ATTN_STEP = B_DILATIONS[1]
assert B_DILATIONS == (1, ATTN_STEP, ATTN_STEP * ATTN_STEP)


def _attn_prompt_body(q0, k0, v0, q1, k1, v1, q2, k2, v2, o_ref,
                      acc_ref, m_ref, l_ref, acc4_ref, m4_ref, l4_ref, q4_ref, k4_ref, v4_ref):
    n = B_BAND
    step_d = ATTN_STEP
    seq4 = SEQ // step_d
    qi = lax.broadcasted_iota(jnp.int32, (n, 2 * n), 0)
    kj = lax.broadcasted_iota(jnp.int32, (n, 2 * n), 1)
    dist = n + qi - kj
    band_mask = (dist >= 0) & (dist <= n)
    qi1 = lax.broadcasted_iota(jnp.int32, (n, n), 0)
    kj1 = lax.broadcasted_iota(jnp.int32, (n, n), 1)
    causal_mask = kj1 <= qi1

    def attend(q, k, v, mask):
        s = jnp.where(mask, _dot_nt((q * B_SCALE).astype(BF16), k.astype(BF16)), NEG_BIG)
        m = jnp.max(s, axis=-1, keepdims=True)
        p = jnp.exp(s - m)
        vb = v.astype(BF16)
        pv = _dot(p.astype(BF16), jnp.concatenate([vb, jnp.ones_like(vb)], axis=1))
        return pv[:, :B_HD], m, pv[:, B_HD:]

    def merge(old, new):
        acc_old, m_old, l_old = old
        acc, m, l = new
        m_new = jnp.maximum(m_old, m)
        a_old = jnp.exp(m_old - m_new)
        a_new = jnp.exp(m - m_new)
        return acc_old * a_old + acc * a_new, m_new, l_old * a_old + l * a_new

    for hh in range(ATTN_HEADS):
        cs = slice(hh * B_HD, (hh + 1) * B_HD)

        for c in range(SEQ // n):
            rows = pl.ds(c * n, n)
            rows_k, mask = (rows, causal_mask) if c == 0 else (pl.ds((c - 1) * n, 2 * n), band_mask)
            acc, m, l = attend(q0[0, rows, cs], k0[0, rows_k, cs], v0[0, rows_k, cs], mask)
            acc_ref[rows, :] = acc
            m_ref[rows, :] = jnp.broadcast_to(m, (n, LANES))
            l_ref[rows, :] = l

        for r in range(step_d):
            for c in range(seq4 // n):
                rows = pl.ds(r + c * n * step_d, n, stride=step_d)
                if c == 0:
                    rows_k, mask = rows, causal_mask
                else:
                    rows_k, mask = pl.ds(r + (c - 1) * n * step_d, 2 * n, stride=step_d), band_mask
                new = attend(q1[0, rows, cs], k1[0, rows_k, cs], v1[0, rows_k, cs], mask)
                acc, m, l = merge((acc_ref[rows, :], m_ref[rows, :], l_ref[rows, :]), new)
                dst = pl.ds(c * n, n)
                acc4_ref[r, dst, :] = acc
                m4_ref[r, dst, :] = m
                l4_ref[r, dst, :] = l

        assert seq4 // step_d == n
        for r in range(step_d):
            rows_r = pl.ds(r, seq4, stride=step_d)
            q4_ref[r] = q2[0, rows_r, cs]
            k4_ref[r] = k2[0, rows_r, cs]
            v4_ref[r] = v2[0, rows_r, cs]
        for r in range(step_d):
            for r2 in range(step_d):
                rows = pl.ds(r2, n, stride=step_d)
                new = attend(q4_ref[r, rows, :], k4_ref[r, rows, :], v4_ref[r, rows, :], causal_mask)
                acc, _, l = merge((acc4_ref[r, rows, :], m4_ref[r, rows, :], l4_ref[r, rows, :]), new)
                acc4_ref[r, rows, :] = acc / l

        for r in range(step_d):
            acc_ref[pl.ds(r, seq4, stride=step_d), :] = acc4_ref[r]
        o_ref[0, :, cs] = acc_ref[...].astype(o_ref.dtype)


def _attn_prompt(qkv_head, kv_last):
    hw = ATTN_HEADS * B_HD

    def spec(first_col):
        return pl.BlockSpec((1, SEQ, hw), lambda b, h: (b, 0, first_col // hw + h))

    in_specs, args = [], []
    for g in range(B_GROUPS):
        c0 = g * 3 * B_WIDTH
        if g < B_GROUPS - 1:
            in_specs += [spec(c0), spec(c0 + B_WIDTH), spec(c0 + 2 * B_WIDTH)]
            args += [qkv_head] * 3
        else:
            in_specs += [spec(c0), spec(0), spec(B_WIDTH)]
            args += [qkv_head, kv_last, kv_last]
    return pl.pallas_call(
        _attn_prompt_body,
        grid=(BATCH, B_HEADS // ATTN_HEADS),
        in_specs=in_specs,
        out_specs=pl.BlockSpec((1, SEQ, hw), lambda b, h: (b, 0, h)),
        out_shape=jax.ShapeDtypeStruct((BATCH, SEQ, B_WIDTH), BF16),
        scratch_shapes=[pltpu.VMEM((SEQ, B_HD), F32)] * 3
                       + [pltpu.VMEM((ATTN_STEP, SEQ // ATTN_STEP, B_HD), F32)] * 6,
        compiler_params=_params(("parallel", "parallel")),
        name="attn_prompt",
    )(*args)


assert all(B_WINDOWS[g] == B_BAND * B_DILATIONS[g] for g in range(B_GROUPS))
assert PAST_LEN >= max(B_WINDOWS) and B_DILATIONS[0] == 1 and DEC_SEQ <= min(B_DILATIONS[1:])


def _attn_sample_body(q_ref, kvn_ref, c0_ref, c1_ref, c2_ref, o_ref):
    k_heads = slice(0, B_HEADS)
    v_heads = slice(B_HEADS, 2 * B_HEADS)

    def piece(q, k, v, valid=None):
        s = jnp.sum(k * q[None], axis=-1, keepdims=True) * B_SCALE
        if valid is not None:
            s = jnp.where(valid, s, NEG_BIG)
        m = jnp.max(s, axis=0)
        p = jnp.exp(s - m[None])
        return m, jnp.sum(p, axis=0), jnp.sum(p * v, axis=0)

    row0 = lax.broadcasted_iota(jnp.int32, (B_WINDOWS[0], B_HEADS, 1), 0)
    strided = (None, c1_ref, c2_ref)
    for t in range(DEC_SEQ):
        terms = []
        for g in range(B_GROUPS):
            q = q_ref[g, 0, t]
            if g == 0:
                valid = B_WINDOWS[0] + t - row0 <= B_BAND
                terms.append(piece(q, c0_ref[0, :, k_heads, :], c0_ref[0, :, v_heads, :], valid))
                new = slice(0, t + 1)
            else:
                c_ref = strided[g]
                terms.append(piece(q, c_ref[0, :, t, k_heads, :], c_ref[0, :, t, v_heads, :]))
                new = slice(t, t + 1)
            terms.append(piece(q, kvn_ref[g, 0, new, k_heads, :], kvn_ref[g, 0, new, v_heads, :]))
        m_all = terms[0][0]
        for m, _, _ in terms[1:]:
            m_all = jnp.maximum(m_all, m)
        l_all = jnp.zeros((B_HEADS, 1), F32)
        acc_all = jnp.zeros((B_HEADS, B_HD), F32)
        for m, l, acc in terms:
            w = jnp.exp(m - m_all)
            l_all = l_all + l * w
            acc_all = acc_all + acc * w
        o_ref[0, t] = acc_all / l_all
    o_ref[0, DEC_SEQ:] = jnp.zeros((SAMPLE_PAD - DEC_SEQ, B_HEADS, B_HD), F32)


def _attn_sample(q_s, kv_new, caches):
    kv_rows = 2 * B_HEADS

    def by_residue(g):
        dil = B_DILATIONS[g]
        view = caches[g].reshape(DEC_BATCH, B_BAND, dil, kv_rows, B_HD)
        return view, pl.BlockSpec((1, B_BAND, DEC_SEQ, kv_rows, B_HD), lambda b: (b, 0, 0, 0, 0))

    c1, c1_spec = by_residue(1)
    c2, c2_spec = by_residue(2)
    return pl.pallas_call(
        _attn_sample_body,
        grid=(DEC_BATCH,),
        in_specs=[
            pl.BlockSpec((B_GROUPS, 1, SAMPLE_PAD, B_HEADS, B_HD), lambda b: (0, b, 0, 0, 0)),
            pl.BlockSpec((B_GROUPS, 1, SAMPLE_PAD, kv_rows, B_HD), lambda b: (0, b, 0, 0, 0)),
            pl.BlockSpec((1, B_WINDOWS[0], kv_rows, B_HD), lambda b: (b, 0, 0, 0)),
            c1_spec, c2_spec,
        ],
        out_specs=pl.BlockSpec((1, SAMPLE_PAD, B_HEADS, B_HD), lambda b: (b, 0, 0, 0)),
        out_shape=jax.ShapeDtypeStruct((DEC_BATCH, SAMPLE_PAD, B_HEADS, B_HD), F32),
        compiler_params=_params(("parallel",)),
        name="attn_sample",
    )(q_s, kv_new, caches[0], c1, c2)


POOL_ROWS = 256
assert all(w & (w - 1) == 0 for w in POOL_WINDOWS)


def _pool_prompt_body(x_ref, g_ref, w_ref, scale_ref, o_ref, tail_ref, r_ref, sum_ref, z_ref):
    grp = pl.program_id(1)
    n_chunks = SEQ // POOL_ROWS

    @pl.when(grp == 0)
    def _():
        def chunk(rows):
            x = x_ref[0, rows, :]
            r_ref[rows, :] = lax.rsqrt(jnp.mean(x * x, axis=-1, keepdims=True) + EPS)
        _for_row_chunks(SEQ, NORM_ROWS, chunk)

    for gi, w in enumerate(POOL_WINDOWS):
        @pl.when(grp == gi)
        def _(gi=gi, w=w):
            cs = slice(gi * C_GW, (gi + 1) * C_GW)

            def normed(c):
                rows = slice(c * POOL_ROWS, (c + 1) * POOL_ROWS)
                return (x_ref[0, rows, cs] * r_ref[rows, :]) * g_ref[:, cs]

            sum_ref[0:POOL_PAD, :] = jnp.zeros((POOL_PAD, C_GW), F32)
            for c in range(n_chunks):
                h = normed(c)
                sum_ref[POOL_PAD + c * POOL_ROWS:POOL_PAD + (c + 1) * POOL_ROWS, :] = h
                if c == n_chunks - 1:
                    tail_ref[0, :, cs] = h[POOL_ROWS - POOL_PAD:]
            k = 1
            while k < w:
                for c in reversed(range(n_chunks)):
                    r0 = POOL_PAD + c * POOL_ROWS
                    sum_ref[r0:r0 + POOL_ROWS, :] = (sum_ref[r0:r0 + POOL_ROWS, :]
                                                    + sum_ref[r0 - k:r0 - k + POOL_ROWS, :])
                k *= 2
            for c in range(n_chunks):
                r0 = POOL_PAD + c * POOL_ROWS
                pos = c * POOL_ROWS + lax.broadcasted_iota(jnp.int32, (POOL_ROWS, 1), 0)
                cnt = jnp.minimum(w, pos + 1).astype(F32)
                z_ref[c * POOL_ROWS:(c + 1) * POOL_ROWS, :] = (
                    sum_ref[r0:r0 + POOL_ROWS, :] / cnt - normed(c)).astype(BF16)
            y = _dot(z_ref[...], w_ref[...].astype(BF16))
            o_ref[0] = x_ref[0, :, cs] + y * scale_ref[...]


def _pool_prompt(x, g_all, g_layer, w_all, scale_all, layer):
    return pl.pallas_call(
        _pool_prompt_body,
        grid=(BATCH, C_GROUPS),
        in_specs=[pl.BlockSpec((1, SEQ, D_MODEL), lambda b, g: (b, 0, 0)),
                  pl.BlockSpec((None, 1, D_MODEL), lambda b, g: (g_layer, 0, 0)),
                  pl.BlockSpec((None, None, C_GW, C_GW), lambda b, g: (layer, g, 0, 0)),
                  pl.BlockSpec((None, 1, C_GW), lambda b, g: (layer, 0, g))],
        out_specs=[pl.BlockSpec((1, SEQ, C_GW), lambda b, g: (b, 0, g)),
                   pl.BlockSpec((1, POOL_PAD, D_MODEL), lambda b, g: (b, 0, 0))],
        out_shape=[jax.ShapeDtypeStruct((BATCH, SEQ, D_MODEL), F32),
                   jax.ShapeDtypeStruct((BATCH, POOL_PAD, D_MODEL), F32)],
        scratch_shapes=[pltpu.VMEM((SEQ, 1), F32),
                        pltpu.VMEM((POOL_PAD + SEQ, C_GW), F32),
                        pltpu.VMEM((SEQ, C_GW), BF16)],
        compiler_params=_params(("arbitrary", "arbitrary")),
        name="pool_prompt",
    )(x, _layer_vec(g_all), w_all, _layer_vec(scale_all))


def _pool_sample_body(x_ref, g_ref, state_ref, w_ref, scale_ref, o_ref, seq_ref, z_ref):
    for b in range(DEC_BATCH):
        rows = slice(b * SAMPLE_PAD, (b + 1) * SAMPLE_PAD)
        seq_ref[b, 0:POOL_PAD, :] = state_ref[b]
        seq_ref[b, POOL_PAD:, :] = _rms_rows(x_ref[rows, :], g_ref[...])
    for b in range(DEC_BATCH):
        rows = slice(b * SAMPLE_PAD, (b + 1) * SAMPLE_PAD)
        for gi, w in enumerate(POOL_WINDOWS):
            cs = slice(gi * C_GW, (gi + 1) * C_GW)
            cur = seq_ref[b, POOL_PAD:POOL_PAD + SAMPLE_PAD, cs]
            tot = cur
            for k in range(1, w):
                tot = tot + seq_ref[b, POOL_PAD - k:POOL_PAD - k + SAMPLE_PAD, cs]
            pos = PAST_LEN + lax.broadcasted_iota(jnp.int32, (SAMPLE_PAD, 1), 0)
            cnt = jnp.minimum(w, pos + 1).astype(F32)
            z_ref[rows, cs] = (tot / cnt - cur).astype(BF16)
    for gi in range(C_GROUPS):
        cs = slice(gi * C_GW, (gi + 1) * C_GW)
        y = _dot(z_ref[:, cs], w_ref[gi].astype(BF16))
        o_ref[:, cs] = x_ref[:, cs] + y * scale_ref[:, cs]


def _pool_sample(xs, g_all, g_layer, state_pad, w_all, scale_all, layer):
    seq_rows = POOL_PAD + SAMPLE_PAD
    return pl.pallas_call(
        _pool_sample_body,
        grid=(1,),
        in_specs=[
            pl.BlockSpec((MS, D_MODEL), lambda i: (0, 0)),
            pl.BlockSpec((None, 1, D_MODEL), lambda i: (g_layer, 0, 0)),
            pl.BlockSpec((DEC_BATCH, POOL_PAD, D_MODEL), lambda i: (0, 0, 0)),
            pl.BlockSpec((None, C_GROUPS, C_GW, C_GW), lambda i: (layer, 0, 0, 0)),
            pl.BlockSpec((None, 1, D_MODEL), lambda i: (layer, 0, 0)),
        ],
        out_specs=[pl.BlockSpec((MS, D_MODEL), lambda i: (0, 0)),
                   pl.BlockSpec((DEC_BATCH, seq_rows, D_MODEL), lambda i: (0, 0, 0))],
        out_shape=[jax.ShapeDtypeStruct((MS, D_MODEL), F32),
                   jax.ShapeDtypeStruct((DEC_BATCH, seq_rows, D_MODEL), F32)],
        scratch_shapes=[pltpu.VMEM((MS, D_MODEL), BF16)],
        compiler_params=_params(("arbitrary",)),
        name="pool_sample",
    )(xs, _layer_vec(g_all), state_pad, w_all, _layer_vec(scale_all))


def _mixer_a(xp, xs, norm_g, layer, ia, a_w_in, a_ln_g, a_ln_b, a_w_s, a_b_s, a_w_out):
    b_s = a_b_s[ia]
    bias_p = jnp.repeat(jnp.transpose(b_s), A_HD, axis=1)
    yp, zs = _amix_prompt(xp, xs, norm_g, layer, a_w_in, a_ln_g, a_ln_b, a_w_s, bias_p, a_w_out, ia)
    ws_s = jnp.tile(a_w_s[ia][:, :SAMPLE_PAD, :SAMPLE_PAD], (1, DEC_BATCH, DEC_BATCH))
    bias_s = jnp.tile(jnp.repeat(jnp.transpose(b_s[:, :SAMPLE_PAD]), A_HD, axis=1), (DEC_BATCH, 1))
    ys, v_s = _amix_sample(xs, zs, a_ln_g, a_ln_b, ws_s, bias_s, a_w_out, ia)
    return yp, ys, v_s


def _mixer_b(xp, xs, norm_g, layer, ib, caches, b_w_qkv, b_q_g, b_k_g, b_w_out):
    ones = jnp.ones((B_GROUPS, B_WIDTH), F32)
    gain = jnp.stack([jnp.tile(b_q_g[ib], (1, B_HEADS)), jnp.tile(b_k_g[ib], (1, B_HEADS)), ones], axis=1)
    gain = gain.reshape(1, QKV_COLS)
    qkv_head, kv_last, qkv_s = _qkv(xp, xs, norm_g, layer, b_w_qkv, ib, gain)
    qkv_head = qkv_head.reshape(BATCH, SEQ, QKV_HEAD_COLS)
    kv_last = kv_last.reshape(BATCH, SEQ, KV_W)
    op = _attn_prompt(qkv_head, kv_last)
    qkv_s = jnp.transpose(qkv_s.reshape(DEC_BATCH, SAMPLE_PAD, B_GROUPS, 3 * B_HEADS, B_HD), (2, 0, 1, 3, 4))
    q_s = qkv_s[:, :, :, :B_HEADS]
    kv_new = qkv_s[:, :, :, B_HEADS:]
    c = [cc[ib].reshape(DEC_BATCH, cc.shape[2], 2 * B_HEADS, B_HD) for cc in caches]
    os_ = _attn_sample(q_s, kv_new, c)
    yp, ys = _proj_residual(op.reshape(MP, B_WIDTH), os_.reshape(MS, B_WIDTH), b_w_out, ib, xp, xs,
                            name="b_out")
    new_p = []
    for g in range(B_GROUPS):
        keep = min(B_WINDOWS[g], SEQ)
        if g == B_GROUPS - 1:
            kv = kv_last[:, SEQ - keep:]
        else:
            kv = qkv_head[:, SEQ - keep:, (3 * g + 1) * B_WIDTH:(3 * g + 3) * B_WIDTH]
        new_p.append(kv.reshape(BATCH, keep, 2, B_HEADS, B_HD))
    new_s = [kv_new[g, :, :DEC_SEQ].reshape(DEC_BATCH, DEC_SEQ, 2, B_HEADS, B_HD)
             for g in range(B_GROUPS)]
    return yp, ys, new_p, new_s


def _mixer_c(xp, xs, norm_g, layer, ic, state, c_w, c_scale):
    yp, tail = _pool_prompt(xp.reshape(BATCH, SEQ, D_MODEL), norm_g, layer, c_w, c_scale, ic)
    yp = yp.reshape(MP, D_MODEL)
    state_pad = jnp.pad(state[ic], ((0, 0), (POOL_PAD - POOL_STATE, 0), (0, 0)))
    ys, seq = _pool_sample(xs, norm_g, layer, state_pad, c_w, c_scale, ic)
    pool_p = tail[:, POOL_PAD - POOL_STATE:]
    first = POOL_PAD + DEC_SEQ - POOL_STATE
    pool_s = seq[:, first:first + POOL_STATE]
    return yp, ys, pool_p, pool_s


def kernel(x_prompt, x_sample, cache_b_kv0, cache_b_kv1, cache_b_kv2, state_c_pool, norm_mix_g, norm_ffn_g, a_w_in, a_ln_g, a_ln_b, a_w_s, a_b_s, a_w_out, b_w_qkv, b_q_g, b_k_g, b_w_out, c_w, c_scale, ffn_w1, ffn_w3, ffn_w2):
    xp = x_prompt.reshape(MP, D_MODEL)
    xs = jnp.pad(x_sample, ((0, 0), (0, SAMPLE_PAD - DEC_SEQ), (0, 0))).reshape(MS, D_MODEL)
    a_v_s, pool_p, pool_s = [], [], []
    kv_p = [[] for _ in range(B_GROUPS)]
    kv_s = [[] for _ in range(B_GROUPS)]
    ia = ib = ic = 0
    for layer in range(DEPTH):
        kind = layer % N_MIXERS
        if kind == 0:
            xp, xs, v_s = _mixer_a(xp, xs, norm_mix_g, layer, ia, a_w_in, a_ln_g, a_ln_b, a_w_s, a_b_s, a_w_out)
            a_v_s.append(v_s.reshape(DEC_BATCH, SAMPLE_PAD, A_WIDTH)[:, :DEC_SEQ])
            ia += 1
        elif kind == 1:
            caches = (cache_b_kv0, cache_b_kv1, cache_b_kv2)
            xp, xs, kvp, kvs = _mixer_b(xp, xs, norm_mix_g, layer, ib, caches, b_w_qkv, b_q_g, b_k_g, b_w_out)
            for g in range(B_GROUPS):
                kv_p[g].append(kvp[g])
                kv_s[g].append(kvs[g])
            ib += 1
        else:
            xp, xs, pp, ps = _mixer_c(xp, xs, norm_mix_g, layer, ic, state_c_pool, c_w, c_scale)
            pool_p.append(pp)
            pool_s.append(ps)
            ic += 1
        xp, xs = _ffn(xp, xs, norm_ffn_g, ffn_w1, ffn_w3, ffn_w2, layer)
    y_prompt = xp.reshape(BATCH, SEQ, D_MODEL)
    y_sample = xs.reshape(DEC_BATCH, SAMPLE_PAD, D_MODEL)[:, :DEC_SEQ]
    return (y_prompt, y_sample, jnp.stack(a_v_s),
            jnp.stack(kv_p[0]), jnp.stack(kv_p[1]), jnp.stack(kv_p[2]),
            jnp.stack(kv_s[0]), jnp.stack(kv_s[1]), jnp.stack(kv_s[2]),
            jnp.stack(pool_p), jnp.stack(pool_s))
```

```python
import jax
import jax.numpy as jnp
from jax import lax
from jax.experimental import pallas as pl
from jax.experimental.pallas import tpu as pltpu

F32 = jnp.float32
BF16 = jnp.bfloat16

D_MODEL = 2048
BATCH = 4
SEQ = 2048
DEPTH = 4
DEC_BATCH = 8
DEC_SEQ = 4
PAST_LEN = 16384
N_MIXERS = 3
A_CHUNK = 128
A_WIDTH = D_MODEL
A_HEADS = 16
A_HD = A_WIDTH // A_HEADS
B_WINDOWS = (128, 512, 2048)
B_DILATIONS = (1, 4, 16)
B_GROUPS = 3
B_HD = 128
B_HEADS = D_MODEL // B_HD
B_WIDTH = B_HEADS * B_HD
B_SCALE = B_HD ** -0.5
B_BAND = 128
POOL_WINDOWS = (2, 4, 8, 16)
C_GROUPS = 4
C_GW = D_MODEL // C_GROUPS
POOL_STATE = max(POOL_WINDOWS) - 1
POOL_PAD = POOL_STATE + 1
D_FF = ((8 * D_MODEL + 3 * 256 - 1) // (3 * 256)) * 256
EPS = 1e-6

SUBLANES = 8
LANES = 128
VMEM_LIMIT_BYTES = 56 * 1024 * 1024

SAMPLE_PAD = SUBLANES
MP = BATCH * SEQ
MS = DEC_BATCH * SAMPLE_PAD
NEG_BIG = -1e30

TM = 1024
TN = 512
TF_FFN = 256
NORM_ROWS = 256
TAIL_ROWS = 256


def _params(semantics):
    return pltpu.CompilerParams(dimension_semantics=semantics,
                                vmem_limit_bytes=VMEM_LIMIT_BYTES)


def _dot(a, b):
    return jnp.dot(a, b, preferred_element_type=F32)


def _dot_nt(a, b):
    return lax.dot_general(a, b, (((1,), (1,)), ((), ())), preferred_element_type=F32)


def _rms_rows(x, g):
    r = lax.rsqrt(jnp.mean(x * x, axis=-1, keepdims=True) + EPS)
    return (x * r) * g


def _for_row_chunks(rows, chunk, fn):
    chunk = min(chunk, rows)
    n = rows // chunk
    if n == 1:
        fn(pl.ds(0, chunk))
        return

    def body(c, carry):
        fn(pl.ds(pl.multiple_of(c * chunk, chunk), chunk))
        return carry

    lax.fori_loop(0, n, body, 0)


def _layer_vec(stacked):
    return stacked.reshape(stacked.shape[0], 1, stacked.shape[1])


def _hold_after_first_tile(col_fn, last):
    return lambda i, j: (0, jnp.where(i == 0, col_fn(j), last))


def _ffn_body(x_ref, xs_ref, g_ref, w1_ref, w3_ref, w2_ref, o_ref, os_ref, hn_ref):
    i = pl.program_id(0)
    j = pl.program_id(1)
    tm = x_ref.shape[0]

    def gate(h, w1, w3):
        return (jax.nn.silu(_dot(h, w1)) * _dot(h, w3)).astype(BF16)

    def first_step(with_sample):
        w1 = w1_ref[...].astype(BF16)
        w3 = w3_ref[...].astype(BF16)
        w2 = w2_ref[...].astype(BF16)
        for c in range(tm // NORM_ROWS):
            rows = slice(c * NORM_ROWS, (c + 1) * NORM_ROWS)
            x = x_ref[rows, :]
            h = _rms_rows(x, g_ref[...]).astype(BF16)
            hn_ref[rows, :] = h
            o_ref[rows, :] = x + _dot(gate(h, w1, w3), w2)
        if with_sample:
            xs = xs_ref[...]
            h = _rms_rows(xs, g_ref[...]).astype(BF16)
            hn_ref[tm:, :] = h
            os_ref[...] = xs + _dot(gate(h, w1, w3), w2)

    def later_step(with_sample):
        w1 = w1_ref[...].astype(BF16)
        w3 = w3_ref[...].astype(BF16)
        w2 = w2_ref[...].astype(BF16)
        if with_sample:
            gt = gate(hn_ref[...], w1, w3)
            o_ref[...] += _dot(gt[:tm], w2)
            os_ref[...] += _dot(gt[tm:], w2)
        else:
            o_ref[...] += _dot(gate(hn_ref[0:tm, :], w1, w3), w2)

    for with_sample, tile_cond in ((True, i == 0), (False, i != 0)):
        for step_fn, col_cond in ((first_step, j == 0), (later_step, j != 0)):
            @pl.when(tile_cond & col_cond)
            def _(step_fn=step_fn, with_sample=with_sample):
                step_fn(with_sample)


def _ffn(x, xs, g_all, w1_all, w3_all, w2_all, layer):
    return pl.pallas_call(
        _ffn_body,
        grid=(MP // TM, D_FF // TF_FFN),
        in_specs=[
            pl.BlockSpec((TM, D_MODEL), lambda i, j: (i, 0)),
            pl.BlockSpec((MS, D_MODEL), lambda i, j: (0, 0)),
            pl.BlockSpec((None, 1, D_MODEL), lambda i, j: (layer, 0, 0)),
            pl.BlockSpec((None, D_MODEL, TF_FFN), lambda i, j: (layer, 0, j)),
            pl.BlockSpec((None, D_MODEL, TF_FFN), lambda i, j: (layer, 0, j)),
            pl.BlockSpec((None, TF_FFN, D_MODEL), lambda i, j: (layer, j, 0)),
        ],
        out_specs=[pl.BlockSpec((TM, D_MODEL), lambda i, j: (i, 0)),
                   pl.BlockSpec((MS, D_MODEL), lambda i, j: (0, 0))],
        out_shape=[jax.ShapeDtypeStruct((MP, D_MODEL), F32),
                   jax.ShapeDtypeStruct((MS, D_MODEL), F32)],
        scratch_shapes=[pltpu.VMEM((TM + MS, D_MODEL), BF16)],
        compiler_params=_params(("arbitrary", "arbitrary")),
        name="ffn",
    )(x, xs, _layer_vec(g_all), w1_all, w3_all, w2_all)


def _fill_lhs(h_ref, x_ref, xs_ref, g_ref, i, j):
    tm = x_ref.shape[0]

    def prep(x):
        if g_ref is not None:
            x = _rms_rows(x, g_ref[...])
        return x.astype(BF16)

    @pl.when(j == 0)
    def _():
        def chunk(rows):
            h_ref[rows, :] = prep(x_ref[rows, :])
        _for_row_chunks(tm, NORM_ROWS, chunk)

    @pl.when((i == 0) & (j == 0))
    def _():
        h_ref[tm:, :] = prep(xs_ref[...])


def _proj_residual_body(x_ref, xs_ref, w_ref, res_ref, res_s_ref, o_ref, os_ref, h_ref):
    i = pl.program_id(0)
    j = pl.program_id(1)
    tm = x_ref.shape[0]
    _fill_lhs(h_ref, x_ref, xs_ref, None, i, j)
    cols = pl.ds(pl.multiple_of(j * TN, TN), TN)

    @pl.when(i == 0)
    def _():
        acc = _dot(h_ref[...], w_ref[:, cols].astype(BF16))
        o_ref[...] = res_ref[...] + acc[:tm]
        os_ref[...] = res_s_ref[...] + acc[tm:]

    @pl.when(i != 0)
    def _():
        o_ref[...] = res_ref[...] + _dot(h_ref[0:tm, :], w_ref[:, cols].astype(BF16))


def _proj_residual(x, xs, w_all, layer, res, res_s, *, name):
    k = x.shape[1]
    n = w_all.shape[2]
    nj = n // TN
    return pl.pallas_call(
        _proj_residual_body,
        grid=(MP // TM, nj),
        in_specs=[pl.BlockSpec((TM, k), lambda i, j: (i, 0)),
                  pl.BlockSpec((MS, k), lambda i, j: (0, 0)),
                  pl.BlockSpec((None, k, n), lambda i, j: (layer, 0, 0), pipeline_mode=pl.Buffered(1)),
                  pl.BlockSpec((TM, TN), lambda i, j: (i, j)),
                  pl.BlockSpec((MS, TN), lambda i, j: (0, j))],
        out_specs=[pl.BlockSpec((TM, TN), lambda i, j: (i, j)),
                   pl.BlockSpec((MS, TN), _hold_after_first_tile(lambda j: j, nj - 1))],
        out_shape=[jax.ShapeDtypeStruct((MP, n), F32),
                   jax.ShapeDtypeStruct((MS, n), F32)],
        scratch_shapes=[pltpu.VMEM((TM + MS, k), BF16)],
        compiler_params=_params(("arbitrary", "arbitrary")),
        name=name,
    )(x, xs, w_all, res, res_s)


TN_QKV = 1024
QKV_TILES = B_WIDTH // TN_QKV
QKV_COLS = B_GROUPS * 3 * B_WIDTH
KV_W = 2 * B_WIDTH
QKV_HEAD_COLS = QKV_COLS - KV_W
MXU_COLS = 256
QKV_NJ = QKV_COLS // TN_QKV
QKV_LAST_KV = QKV_HEAD_COLS // TN_QKV


def _qkv_body(x_hbm, xs_ref, g_ref, w_ref, gain_ref, o_ref, olast_ref, os_ref, h_ref, x_ref, x_sem):
    i = pl.program_id(0)
    j = pl.program_id(1)
    tm = x_ref.shape[0]

    def x_copy(tile):
        return pltpu.make_async_copy(x_hbm.at[pl.ds(tile * tm, tm), :], x_ref, x_sem)

    @pl.when((i == 0) & (j == 0))
    def _():
        x_copy(0).start()

    @pl.when(j == 0)
    def _():
        x_copy(i).wait()

    _fill_lhs(h_ref, x_ref, xs_ref, g_ref, i, j)

    @pl.when((j == 1) & (i + 1 < pl.num_programs(0)))
    def _():
        x_copy(i + 1).start()

    is_v = (j // QKV_TILES) % 3 == 2
    in_last = j >= QKV_LAST_KV

    def store_heads(acc, out_ref, c0, rows=slice(None)):
        for hh in range(MXU_COLS // B_HD):
            a = acc[:, hh * B_HD:(hh + 1) * B_HD]
            cs = slice(c0 + hh * B_HD, c0 + (hh + 1) * B_HD)
            r = lax.rsqrt(jnp.mean(a * a, axis=-1, keepdims=True) + EPS)
            r = jnp.where(is_v, 1.0, r)
            out_ref[rows, cs] = (a * r) * gain_ref[:, cs]

    def run(out_ref, with_sample):
        h = h_ref[...] if with_sample else h_ref[0:tm, :]
        for c0 in range(0, TN_QKV, MXU_COLS):
            w = w_ref[:, c0:c0 + MXU_COLS].astype(BF16)
            if with_sample:
                acc = _dot(h, w)
                store_heads(acc[:tm], out_ref, c0)
                store_heads(acc[tm:], os_ref, c0)
            elif c0 + MXU_COLS < TN_QKV:
                store_heads(_dot(h, w), out_ref, c0)
            else:
                for r0 in range(0, tm, TAIL_ROWS):
                    rows = slice(r0, r0 + TAIL_ROWS)
                    store_heads(_dot(h_ref[rows, :], w), out_ref, c0, rows)

    for with_sample, tile_cond in ((True, i == 0), (False, i != 0)):
        for out_ref, dest_cond in ((o_ref, jnp.logical_not(in_last)), (olast_ref, in_last)):
            @pl.when(tile_cond & dest_cond)
            def _(out_ref=out_ref, with_sample=with_sample):
                run(out_ref, with_sample)


def _qkv(x, xs, g_all, g_layer, w_all, layer, gain):
    def out_map(i, j):
        return i, jnp.minimum(j, QKV_LAST_KV - 1)

    def out_last_map(i, j):
        return i, jnp.maximum(j - QKV_LAST_KV, 0)

    return pl.pallas_call(
        _qkv_body,
        grid=(MP // TM, QKV_NJ),
        in_specs=[
            pl.BlockSpec(memory_space=pl.ANY),
            pl.BlockSpec((MS, D_MODEL), lambda i, j: (0, 0)),
            pl.BlockSpec((None, 1, D_MODEL), lambda i, j: (g_layer, 0, 0)),
            pl.BlockSpec((None, D_MODEL, TN_QKV), lambda i, j: (layer, 0, j)),
            pl.BlockSpec((1, TN_QKV), lambda i, j: (0, j)),
        ],
        out_specs=[pl.BlockSpec((TM, TN_QKV), out_map),
                   pl.BlockSpec((TM, TN_QKV), out_last_map),
                   pl.BlockSpec((MS, TN_QKV), _hold_after_first_tile(lambda j: j, QKV_NJ - 1))],
        out_shape=[jax.ShapeDtypeStruct((MP, QKV_HEAD_COLS), F32),
                   jax.ShapeDtypeStruct((MP, KV_W), F32),
                   jax.ShapeDtypeStruct((MS, QKV_COLS), F32)],
        scratch_shapes=[pltpu.VMEM((TM + MS, D_MODEL), BF16),
                        pltpu.VMEM((TM, D_MODEL), F32),
                        pltpu.SemaphoreType.DMA(())],
        compiler_params=_params(("arbitrary", "arbitrary")),
        name="qkv",
    )(x, xs, _layer_vec(g_all), w_all, gain)


def _mixing_weights(wsm_ref, ws_ref, t_len, block):
    r = lax.broadcasted_iota(jnp.int32, (t_len, t_len), 0)
    c = lax.broadcasted_iota(jnp.int32, (t_len, t_len), 1)
    keep = c <= r
    if block < t_len:
        keep = keep & ((r // block) == (c // block)) & ((c % block) < DEC_SEQ)
    for h in range(A_HEADS):
        wsm_ref[h] = jnp.where(keep, ws_ref[h], 0.0).astype(BF16)


TN_A_IN = 1024
TN_A_OUT = 256
A_UV = A_WIDTH // TN_A_IN
A_P1 = 2 * A_UV
A_P2 = D_MODEL // TN_A_OUT


assert A_UV * TN_A_IN == D_MODEL


def _amix_prompt_body(x_hbm, xs_ref, g_ref, win_ref, lng_ref, lnb_ref, ws_ref, bias_ref, wout_ref,
                      o_ref, zs_ref, h_ref, u_ref, v_ref, x_ref, wsm_ref, x_sem):
    i = pl.program_id(0)
    j = pl.program_id(1)
    tm = x_ref.shape[0]

    def x_copies(tile):
        rows = pl.ds(tile * tm, tm)
        return [pltpu.make_async_copy(x_hbm.at[rows, pl.ds(k * TN_A_IN, TN_A_IN)], v_ref.at[k], x_sem.at[k])
                for k in range(A_UV)]

    @pl.when((i == 0) & (j == 0))
    def _():
        for copy in x_copies(0):
            copy.start()
        h_ref[tm:, :] = _rms_rows(xs_ref[...], g_ref[...]).astype(BF16)

    @pl.when(j == 0)
    def _():
        for copy in x_copies(i):
            copy.wait()

        def chunk(rows):
            halves = [v_ref[k, rows, :] for k in range(A_UV)]
            ms = sum(jnp.sum(xk * xk, axis=-1, keepdims=True) for xk in halves) / D_MODEL
            r = lax.rsqrt(ms + EPS)
            for k, xk in enumerate(halves):
                cols = slice(k * TN_A_IN, (k + 1) * TN_A_IN)
                h_ref[rows, cols] = ((xk * r) * g_ref[:, cols]).astype(BF16)
                x_ref[rows, cols] = xk
        _for_row_chunks(tm, NORM_ROWS, chunk)

    @pl.when((j == A_P1 + 1) & (i + 1 < pl.num_programs(0)))
    def _():
        for copy in x_copies(i + 1):
            copy.start()

    def phase1(with_sample, dst_ref, slot):
        h = h_ref[...] if with_sample else h_ref[0:tm, :]
        for c0 in range(0, TN_A_IN, MXU_COLS):
            cs = slice(c0, c0 + MXU_COLS)
            w = win_ref[:, cs].astype(BF16)
            if with_sample:
                z = jax.nn.gelu(_dot(h, w), approximate=True)
                dst_ref[slot, :, cs] = z[:tm].astype(dst_ref.dtype)
                zs_ref[:, cs] = z[tm:]
            elif c0 + MXU_COLS < TN_A_IN:
                dst_ref[slot, :, cs] = jax.nn.gelu(_dot(h, w), approximate=True).astype(dst_ref.dtype)
            else:
                for r0 in range(0, tm, TAIL_ROWS):
                    rows = slice(r0, r0 + TAIL_ROWS)
                    z = jax.nn.gelu(_dot(h_ref[rows, :], w), approximate=True)
                    dst_ref[slot, rows, cs] = z.astype(dst_ref.dtype)

    for with_sample, tile_cond in ((True, i == 0), (False, i != 0)):
        for dst_ref, half_cond, slot in ((u_ref, j < A_UV, j),
                                         (v_ref, (j >= A_UV) & (j < A_P1), j - A_UV)):
            @pl.when(tile_cond & half_cond)
            def _(with_sample=with_sample, dst_ref=dst_ref, slot=slot):
                phase1(with_sample, dst_ref, slot)

    @pl.when(j == A_P1)
    def _():
        _mixing_weights(wsm_ref, ws_ref, A_CHUNK, A_CHUNK)

        def chunk(rows):
            vs = [v_ref[k, rows, :] for k in range(A_UV)]
            mu = sum(jnp.sum(vk, axis=-1, keepdims=True) for vk in vs) / A_WIDTH
            ds = [vk - mu for vk in vs]
            var = sum(jnp.sum(dk * dk, axis=-1, keepdims=True) for dk in ds) / A_WIDTH
            r = lax.rsqrt(var + EPS)
            for h in range(A_HEADS):
                k, c = divmod(h * A_HD, TN_A_IN)
                cs = slice(h * A_HD, (h + 1) * A_HD)
                vln = (ds[k][:, c:c + A_HD] * r) * lng_ref[:, cs] + lnb_ref[:, cs]
                mixed = _dot(wsm_ref[h], vln.astype(BF16)) + bias_ref[:, cs]
                h_ref[rows, cs] = (u_ref[k, rows, c:c + A_HD] * mixed).astype(BF16)
        _for_row_chunks(tm, A_CHUNK, chunk)

    @pl.when(j >= A_P1)
    def _():
        cols = pl.ds(pl.multiple_of((j - A_P1) * TN_A_OUT, TN_A_OUT), TN_A_OUT)
        o_ref[...] = x_ref[:, cols] + _dot(h_ref[0:tm, :], wout_ref[...].astype(BF16))


def _amix_prompt(x, xs, g_all, g_layer, win_all, lng_all, lnb_all, ws_all, bias, wout_all, layer):
    def out_col(j):
        return jnp.maximum(j - A_P1, 0)

    def z_col(j):
        return jnp.minimum(j, A_P1 - 1)

    def win_col(j):
        return jnp.where(j < A_P1, j, 0)

    return pl.pallas_call(
        _amix_prompt_body,
        grid=(MP // TM, A_P1 + A_P2),
        in_specs=[
            pl.BlockSpec(memory_space=pl.ANY),
            pl.BlockSpec((MS, D_MODEL), lambda i, j: (0, 0)),
            pl.BlockSpec((None, 1, D_MODEL), lambda i, j: (g_layer, 0, 0)),
            pl.BlockSpec((None, D_MODEL, TN_A_IN), lambda i, j: (layer, 0, win_col(j))),
            pl.BlockSpec((None, 1, A_WIDTH), lambda i, j: (layer, 0, 0)),
            pl.BlockSpec((None, 1, A_WIDTH), lambda i, j: (layer, 0, 0)),
            pl.BlockSpec((None, A_HEADS, A_CHUNK, A_CHUNK), lambda i, j: (layer, 0, 0, 0)),
            pl.BlockSpec((A_CHUNK, A_WIDTH), lambda i, j: (0, 0)),
            pl.BlockSpec((None, A_WIDTH, TN_A_OUT), lambda i, j: (layer, 0, out_col(j))),
        ],
        out_specs=[pl.BlockSpec((TM, TN_A_OUT), lambda i, j: (i, out_col(j))),
                   pl.BlockSpec((MS, TN_A_IN), _hold_after_first_tile(z_col, A_P1 - 1))],
        out_shape=[jax.ShapeDtypeStruct((MP, D_MODEL), F32),
                   jax.ShapeDtypeStruct((MS, 2 * A_WIDTH), F32)],
        scratch_shapes=[pltpu.VMEM((TM + MS, D_MODEL), BF16),
                        pltpu.VMEM((A_UV, TM, TN_A_IN), BF16),
                        pltpu.VMEM((A_UV, TM, TN_A_IN), F32),
                        pltpu.VMEM((TM, D_MODEL), F32),
                        pltpu.VMEM((A_HEADS, A_CHUNK, A_CHUNK), BF16),
                        pltpu.SemaphoreType.DMA((A_UV,))],
        compiler_params=_params(("arbitrary", "arbitrary")),
        name="amix_prompt",
    )(x, xs, _layer_vec(g_all), win_all, _layer_vec(lng_all), _layer_vec(lnb_all), ws_all, bias,
      wout_all)


def _amix_sample_body(x_ref, u_ref, v_ref, lng_ref, lnb_ref, ws_ref, bias_ref, wout_ref,
                      o_ref, vout_ref, gated_ref, wsm_ref):
    j = pl.program_id(0)

    @pl.when(j == 0)
    def _():
        _mixing_weights(wsm_ref, ws_ref, MS, SAMPLE_PAD)
        v = v_ref[...]
        mu = jnp.mean(v, axis=-1, keepdims=True)
        d = v - mu
        var = jnp.mean(d * d, axis=-1, keepdims=True)
        vln = (d * lax.rsqrt(var + EPS)) * lng_ref[...] + lnb_ref[...]
        vout_ref[...] = vln
        vb = vln.astype(BF16)
        for h in range(A_HEADS):
            cs = slice(h * A_HD, (h + 1) * A_HD)
            mixed = _dot(wsm_ref[h], vb[:, cs]) + bias_ref[:, cs]
            gated_ref[:, cs] = (u_ref[:, cs] * mixed).astype(BF16)

    o_ref[...] = x_ref[...] + _dot(gated_ref[...], wout_ref[...].astype(BF16))


def _amix_sample(xs, zs, lng_all, lnb_all, ws_s, bias_s, wout_all, layer):
    return pl.pallas_call(
        _amix_sample_body,
        grid=(D_MODEL // TN,),
        in_specs=[
            pl.BlockSpec((MS, TN), lambda j: (0, j)),
            pl.BlockSpec((MS, A_WIDTH), lambda j: (0, 0)),
            pl.BlockSpec((MS, A_WIDTH), lambda j: (0, 1)),
            pl.BlockSpec((None, 1, A_WIDTH), lambda j: (layer, 0, 0)),
            pl.BlockSpec((None, 1, A_WIDTH), lambda j: (layer, 0, 0)),
            pl.BlockSpec((A_HEADS, MS, MS), lambda j: (0, 0, 0)),
            pl.BlockSpec((MS, A_WIDTH), lambda j: (0, 0)),
            pl.BlockSpec((None, A_WIDTH, TN), lambda j: (layer, 0, j)),
        ],
        out_specs=[pl.BlockSpec((MS, TN), lambda j: (0, j)),
                   pl.BlockSpec((MS, A_WIDTH), lambda j: (0, 0))],
        out_shape=[jax.ShapeDtypeStruct((MS, D_MODEL), F32),
                   jax.ShapeDtypeStruct((MS, A_WIDTH), F32)],
        scratch_shapes=[pltpu.VMEM((MS, A_WIDTH), BF16),
                        pltpu.VMEM((A_HEADS, MS, MS), BF16)],
        compiler_params=_params(("arbitrary",)),
        name="amix_sample",
    )(xs, zs, zs, _layer_vec(lng_all), _layer_vec(lnb_all), ws_s, bias_s, wout_all)


ATTN_HEADS = 1


ATTN_STEP = B_DILATIONS[1]
assert B_DILATIONS == (1, ATTN_STEP, ATTN_STEP * ATTN_STEP)


def _attn_prompt_body(q0, k0, v0, q1, k1, v1, q2, k2, v2, o_ref,
                      acc_ref, m_ref, l_ref, acc4_ref, m4_ref, l4_ref, q4_ref, k4_ref, v4_ref):
    n = B_BAND
    step_d = ATTN_STEP
    seq4 = SEQ // step_d
    qi = lax.broadcasted_iota(jnp.int32, (n, 2 * n), 0)
    kj = lax.broadcasted_iota(jnp.int32, (n, 2 * n), 1)
    dist = n + qi - kj
    band_mask = (dist >= 0) & (dist <= n)
    qi1 = lax.broadcasted_iota(jnp.int32, (n, n), 0)
    kj1 = lax.broadcasted_iota(jnp.int32, (n, n), 1)
    causal_mask = kj1 <= qi1

    def attend(q, k, v, mask):
        s = jnp.where(mask, _dot_nt((q * B_SCALE).astype(BF16), k.astype(BF16)), NEG_BIG)
        m = jnp.max(s, axis=-1, keepdims=True)
        p = jnp.exp(s - m)
        vb = v.astype(BF16)
        pv = _dot(p.astype(BF16), jnp.concatenate([vb, jnp.ones_like(vb)], axis=1))
        return pv[:, :B_HD], m, pv[:, B_HD:]

    def merge(old, new):
        acc_old, m_old, l_old = old
        acc, m, l = new
        m_new = jnp.maximum(m_old, m)
        a_old = jnp.exp(m_old - m_new)
        a_new = jnp.exp(m - m_new)
        return acc_old * a_old + acc * a_new, m_new, l_old * a_old + l * a_new

    for hh in range(ATTN_HEADS):
        cs = slice(hh * B_HD, (hh + 1) * B_HD)

        for c in range(SEQ // n):
            rows = pl.ds(c * n, n)
            rows_k, mask = (rows, causal_mask) if c == 0 else (pl.ds((c - 1) * n, 2 * n), band_mask)
            acc, m, l = attend(q0[0, rows, cs], k0[0, rows_k, cs], v0[0, rows_k, cs], mask)
            acc_ref[rows, :] = acc
            m_ref[rows, :] = jnp.broadcast_to(m, (n, LANES))
            l_ref[rows, :] = l

        for r in range(step_d):
            for c in range(seq4 // n):
                rows = pl.ds(r + c * n * step_d, n, stride=step_d)
                if c == 0:
                    rows_k, mask = rows, causal_mask
                else:
                    rows_k, mask = pl.ds(r + (c - 1) * n * step_d, 2 * n, stride=step_d), band_mask
                new = attend(q1[0, rows, cs], k1[0, rows_k, cs], v1[0, rows_k, cs], mask)
                acc, m, l = merge((acc_ref[rows, :], m_ref[rows, :], l_ref[rows, :]), new)
                dst = pl.ds(c * n, n)
                acc4_ref[r, dst, :] = acc
                m4_ref[r, dst, :] = m
                l4_ref[r, dst, :] = l

        assert seq4 // step_d == n
        for r in range(step_d):
            rows_r = pl.ds(r, seq4, stride=step_d)
            q4_ref[r] = q2[0, rows_r, cs]
            k4_ref[r] = k2[0, rows_r, cs]
            v4_ref[r] = v2[0, rows_r, cs]
        for r in range(step_d):
            for r2 in range(step_d):
                rows = pl.ds(r2, n, stride=step_d)
                new = attend(q4_ref[r, rows, :], k4_ref[r, rows, :], v4_ref[r, rows, :], causal_mask)
                acc, _, l = merge((acc4_ref[r, rows, :], m4_ref[r, rows, :], l4_ref[r, rows, :]), new)
                acc4_ref[r, rows, :] = acc / l

        for r in range(step_d):
            acc_ref[pl.ds(r, seq4, stride=step_d), :] = acc4_ref[r]
        o_ref[0, :, cs] = acc_ref[...].astype(o_ref.dtype)


def _attn_prompt(qkv_head, kv_last):
    hw = ATTN_HEADS * B_HD

    def spec(first_col):
        return pl.BlockSpec((1, SEQ, hw), lambda b, h: (b, 0, first_col // hw + h))

    in_specs, args = [], []
    for g in range(B_GROUPS):
        c0 = g * 3 * B_WIDTH
        if g < B_GROUPS - 1:
            in_specs += [spec(c0), spec(c0 + B_WIDTH), spec(c0 + 2 * B_WIDTH)]
            args += [qkv_head] * 3
        else:
            in_specs += [spec(c0), spec(0), spec(B_WIDTH)]
            args += [qkv_head, kv_last, kv_last]
    return pl.pallas_call(
        _attn_prompt_body,
        grid=(BATCH, B_HEADS // ATTN_HEADS),
        in_specs=in_specs,
        out_specs=pl.BlockSpec((1, SEQ, hw), lambda b, h: (b, 0, h)),
        out_shape=jax.ShapeDtypeStruct((BATCH, SEQ, B_WIDTH), BF16),
        scratch_shapes=[pltpu.VMEM((SEQ, B_HD), F32)] * 3
                       + [pltpu.VMEM((ATTN_STEP, SEQ // ATTN_STEP, B_HD), F32)] * 6,
        compiler_params=_params(("parallel", "parallel")),
        name="attn_prompt",
    )(*args)


assert all(B_WINDOWS[g] == B_BAND * B_DILATIONS[g] for g in range(B_GROUPS))
assert PAST_LEN >= max(B_WINDOWS) and B_DILATIONS[0] == 1 and DEC_SEQ <= min(B_DILATIONS[1:])


def _attn_sample_body(q_ref, kvn_ref, c0_ref, c1_ref, c2_ref, o_ref):
    k_heads = slice(0, B_HEADS)
    v_heads = slice(B_HEADS, 2 * B_HEADS)

    def piece(q, k, v, valid=None):
        s = jnp.sum(k * q[None], axis=-1, keepdims=True) * B_SCALE
        if valid is not None:
            s = jnp.where(valid, s, NEG_BIG)
        m = jnp.max(s, axis=0)
        p = jnp.exp(s - m[None])
        return m, jnp.sum(p, axis=0), jnp.sum(p * v, axis=0)

    row0 = lax.broadcasted_iota(jnp.int32, (B_WINDOWS[0], B_HEADS, 1), 0)
    strided = (None, c1_ref, c2_ref)
    for t in range(DEC_SEQ):
        terms = []
        for g in range(B_GROUPS):
            q = q_ref[g, 0, t]
            if g == 0:
                valid = B_WINDOWS[0] + t - row0 <= B_BAND
                terms.append(piece(q, c0_ref[0, :, k_heads, :], c0_ref[0, :, v_heads, :], valid))
                new = slice(0, t + 1)
            else:
                c_ref = strided[g]
                terms.append(piece(q, c_ref[0, :, t, k_heads, :], c_ref[0, :, t, v_heads, :]))
                new = slice(t, t + 1)
            terms.append(piece(q, kvn_ref[g, 0, new, k_heads, :], kvn_ref[g, 0, new, v_heads, :]))
        m_all = terms[0][0]
        for m, _, _ in terms[1:]:
            m_all = jnp.maximum(m_all, m)
        l_all = jnp.zeros((B_HEADS, 1), F32)
        acc_all = jnp.zeros((B_HEADS, B_HD), F32)
        for m, l, acc in terms:
            w = jnp.exp(m - m_all)
            l_all = l_all + l * w
            acc_all = acc_all + acc * w
        o_ref[0, t] = acc_all / l_all
    o_ref[0, DEC_SEQ:] = jnp.zeros((SAMPLE_PAD - DEC_SEQ, B_HEADS, B_HD), F32)


def _attn_sample(q_s, kv_new, caches):
    kv_rows = 2 * B_HEADS

    def by_residue(g):
        dil = B_DILATIONS[g]
        view = caches[g].reshape(DEC_BATCH, B_BAND, dil, kv_rows, B_HD)
        return view, pl.BlockSpec((1, B_BAND, DEC_SEQ, kv_rows, B_HD), lambda b: (b, 0, 0, 0, 0))

    c1, c1_spec = by_residue(1)
    c2, c2_spec = by_residue(2)
    return pl.pallas_call(
        _attn_sample_body,
        grid=(DEC_BATCH,),
        in_specs=[
            pl.BlockSpec((B_GROUPS, 1, SAMPLE_PAD, B_HEADS, B_HD), lambda b: (0, b, 0, 0, 0)),
            pl.BlockSpec((B_GROUPS, 1, SAMPLE_PAD, kv_rows, B_HD), lambda b: (0, b, 0, 0, 0)),
            pl.BlockSpec((1, B_WINDOWS[0], kv_rows, B_HD), lambda b: (b, 0, 0, 0)),
            c1_spec, c2_spec,
        ],
        out_specs=pl.BlockSpec((1, SAMPLE_PAD, B_HEADS, B_HD), lambda b: (b, 0, 0, 0)),
        out_shape=jax.ShapeDtypeStruct((DEC_BATCH, SAMPLE_PAD, B_HEADS, B_HD), F32),
        compiler_params=_params(("parallel",)),
        name="attn_sample",
    )(q_s, kv_new, caches[0], c1, c2)


POOL_ROWS = 256
assert all(w & (w - 1) == 0 for w in POOL_WINDOWS)


def _pool_prompt_body(x_ref, g_ref, w_ref, scale_ref, o_ref, tail_ref, r_ref, sum_ref, z_ref):
    grp = pl.program_id(1)
    n_chunks = SEQ // POOL_ROWS

    @pl.when(grp == 0)
    def _():
        def chunk(rows):
            x = x_ref[0, rows, :]
            r_ref[rows, :] = lax.rsqrt(jnp.mean(x * x, axis=-1, keepdims=True) + EPS)
        _for_row_chunks(SEQ, NORM_ROWS, chunk)

    for gi, w in enumerate(POOL_WINDOWS):
        @pl.when(grp == gi)
        def _(gi=gi, w=w):
            cs = slice(gi * C_GW, (gi + 1) * C_GW)

            def normed(c):
                rows = slice(c * POOL_ROWS, (c + 1) * POOL_ROWS)
                return (x_ref[0, rows, cs] * r_ref[rows, :]) * g_ref[:, cs]

            sum_ref[0:POOL_PAD, :] = jnp.zeros((POOL_PAD, C_GW), F32)
            for c in range(n_chunks):
                h = normed(c)
                sum_ref[POOL_PAD + c * POOL_ROWS:POOL_PAD + (c + 1) * POOL_ROWS, :] = h
                if c == n_chunks - 1:
                    tail_ref[0, :, cs] = h[POOL_ROWS - POOL_PAD:]
            k = 1
            while k < w:
                for c in reversed(range(n_chunks)):
                    r0 = POOL_PAD + c * POOL_ROWS
                    sum_ref[r0:r0 + POOL_ROWS, :] = (sum_ref[r0:r0 + POOL_ROWS, :]
                                                    + sum_ref[r0 - k:r0 - k + POOL_ROWS, :])
                k *= 2
            for c in range(n_chunks):
                r0 = POOL_PAD + c * POOL_ROWS
                pos = c * POOL_ROWS + lax.broadcasted_iota(jnp.int32, (POOL_ROWS, 1), 0)
                cnt = jnp.minimum(w, pos + 1).astype(F32)
                z_ref[c * POOL_ROWS:(c + 1) * POOL_ROWS, :] = (
                    sum_ref[r0:r0 + POOL_ROWS, :] / cnt - normed(c)).astype(BF16)
            y = _dot(z_ref[...], w_ref[...].astype(BF16))
            o_ref[0] = x_ref[0, :, cs] + y * scale_ref[...]


def _pool_prompt(x, g_all, g_layer, w_all, scale_all, layer):
    return pl.pallas_call(
        _pool_prompt_body,
        grid=(BATCH, C_GROUPS),
        in_specs=[pl.BlockSpec((1, SEQ, D_MODEL), lambda b, g: (b, 0, 0)),
                  pl.BlockSpec((None, 1, D_MODEL), lambda b, g: (g_layer, 0, 0)),
                  pl.BlockSpec((None, None, C_GW, C_GW), lambda b, g: (layer, g, 0, 0)),
                  pl.BlockSpec((None, 1, C_GW), lambda b, g: (layer, 0, g))],
        out_specs=[pl.BlockSpec((1, SEQ, C_GW), lambda b, g: (b, 0, g)),
                   pl.BlockSpec((1, POOL_PAD, D_MODEL), lambda b, g: (b, 0, 0))],
        out_shape=[jax.ShapeDtypeStruct((BATCH, SEQ, D_MODEL), F32),
                   jax.ShapeDtypeStruct((BATCH, POOL_PAD, D_MODEL), F32)],
        scratch_shapes=[pltpu.VMEM((SEQ, 1), F32),
                        pltpu.VMEM((POOL_PAD + SEQ, C_GW), F32),
                        pltpu.VMEM((SEQ, C_GW), BF16)],
        compiler_params=_params(("arbitrary", "arbitrary")),
        name="pool_prompt",
    )(x, _layer_vec(g_all), w_all, _layer_vec(scale_all))


def _pool_sample_body(x_ref, g_ref, state_ref, w_ref, scale_ref, o_ref, seq_ref, z_ref):
    for b in range(DEC_BATCH):
        rows = slice(b * SAMPLE_PAD, (b + 1) * SAMPLE_PAD)
        seq_ref[b, 0:POOL_PAD, :] = state_ref[b]
        seq_ref[b, POOL_PAD:, :] = _rms_rows(x_ref[rows, :], g_ref[...])
    for b in range(DEC_BATCH):
        rows = slice(b * SAMPLE_PAD, (b + 1) * SAMPLE_PAD)
        for gi, w in enumerate(POOL_WINDOWS):
            cs = slice(gi * C_GW, (gi + 1) * C_GW)
            cur = seq_ref[b, POOL_PAD:POOL_PAD + SAMPLE_PAD, cs]
            tot = cur
            for k in range(1, w):
                tot = tot + seq_ref[b, POOL_PAD - k:POOL_PAD - k + SAMPLE_PAD, cs]
            pos = PAST_LEN + lax.broadcasted_iota(jnp.int32, (SAMPLE_PAD, 1), 0)
            cnt = jnp.minimum(w, pos + 1).astype(F32)
            z_ref[rows, cs] = (tot / cnt - cur).astype(BF16)
    for gi in range(C_GROUPS):
        cs = slice(gi * C_GW, (gi + 1) * C_GW)
        y = _dot(z_ref[:, cs], w_ref[gi].astype(BF16))
        o_ref[:, cs] = x_ref[:, cs] + y * scale_ref[:, cs]


def _pool_sample(xs, g_all, g_layer, state_pad, w_all, scale_all, layer):
    seq_rows = POOL_PAD + SAMPLE_PAD
    return pl.pallas_call(
        _pool_sample_body,
        grid=(1,),
        in_specs=[
            pl.BlockSpec((MS, D_MODEL), lambda i: (0, 0)),
            pl.BlockSpec((None, 1, D_MODEL), lambda i: (g_layer, 0, 0)),
            pl.BlockSpec((DEC_BATCH, POOL_PAD, D_MODEL), lambda i: (0, 0, 0)),
            pl.BlockSpec((None, C_GROUPS, C_GW, C_GW), lambda i: (layer, 0, 0, 0)),
            pl.BlockSpec((None, 1, D_MODEL), lambda i: (layer, 0, 0)),
        ],
        out_specs=[pl.BlockSpec((MS, D_MODEL), lambda i: (0, 0)),
                   pl.BlockSpec((DEC_BATCH, seq_rows, D_MODEL), lambda i: (0, 0, 0))],
        out_shape=[jax.ShapeDtypeStruct((MS, D_MODEL), F32),
                   jax.ShapeDtypeStruct((DEC_BATCH, seq_rows, D_MODEL), F32)],
        scratch_shapes=[pltpu.VMEM((MS, D_MODEL), BF16)],
        compiler_params=_params(("arbitrary",)),
        name="pool_sample",
    )(xs, _layer_vec(g_all), state_pad, w_all, _layer_vec(scale_all))


def _mixer_a(xp, xs, norm_g, layer, ia, a_w_in, a_ln_g, a_ln_b, a_w_s, a_b_s, a_w_out):
    b_s = a_b_s[ia]
    bias_p = jnp.repeat(jnp.transpose(b_s), A_HD, axis=1)
    yp, zs = _amix_prompt(xp, xs, norm_g, layer, a_w_in, a_ln_g, a_ln_b, a_w_s, bias_p, a_w_out, ia)
    ws_s = jnp.tile(a_w_s[ia][:, :SAMPLE_PAD, :SAMPLE_PAD], (1, DEC_BATCH, DEC_BATCH))
    bias_s = jnp.tile(jnp.repeat(jnp.transpose(b_s[:, :SAMPLE_PAD]), A_HD, axis=1), (DEC_BATCH, 1))
    ys, v_s = _amix_sample(xs, zs, a_ln_g, a_ln_b, ws_s, bias_s, a_w_out, ia)
    return yp, ys, v_s


def _mixer_b(xp, xs, norm_g, layer, ib, caches, b_w_qkv, b_q_g, b_k_g, b_w_out):
    ones = jnp.ones((B_GROUPS, B_WIDTH), F32)
    gain = jnp.stack([jnp.tile(b_q_g[ib], (1, B_HEADS)), jnp.tile(b_k_g[ib], (1, B_HEADS)), ones], axis=1)
    gain = gain.reshape(1, QKV_COLS)
    qkv_head, kv_last, qkv_s = _qkv(xp, xs, norm_g, layer, b_w_qkv, ib, gain)
    qkv_head = qkv_head.reshape(BATCH, SEQ, QKV_HEAD_COLS)
    kv_last = kv_last.reshape(BATCH, SEQ, KV_W)
    op = _attn_prompt(qkv_head, kv_last)
    qkv_s = jnp.transpose(qkv_s.reshape(DEC_BATCH, SAMPLE_PAD, B_GROUPS, 3 * B_HEADS, B_HD), (2, 0, 1, 3, 4))
    q_s = qkv_s[:, :, :, :B_HEADS]
    kv_new = qkv_s[:, :, :, B_HEADS:]
    c = [cc[ib].reshape(DEC_BATCH, cc.shape[2], 2 * B_HEADS, B_HD) for cc in caches]
    os_ = _attn_sample(q_s, kv_new, c)
    yp, ys = _proj_residual(op.reshape(MP, B_WIDTH), os_.reshape(MS, B_WIDTH), b_w_out, ib, xp, xs,
                            name="b_out")
    new_p = []
    for g in range(B_GROUPS):
        keep = min(B_WINDOWS[g], SEQ)
        if g == B_GROUPS - 1:
            kv = kv_last[:, SEQ - keep:]
        else:
            kv = qkv_head[:, SEQ - keep:, (3 * g + 1) * B_WIDTH:(3 * g + 3) * B_WIDTH]
        new_p.append(kv.reshape(BATCH, keep, 2, B_HEADS, B_HD))
    new_s = [kv_new[g, :, :DEC_SEQ].reshape(DEC_BATCH, DEC_SEQ, 2, B_HEADS, B_HD)
             for g in range(B_GROUPS)]
    return yp, ys, new_p, new_s


def _mixer_c(xp, xs, norm_g, layer, ic, state, c_w, c_scale):
    yp, tail = _pool_prompt(xp.reshape(BATCH, SEQ, D_MODEL), norm_g, layer, c_w, c_scale, ic)
    yp = yp.reshape(MP, D_MODEL)
    state_pad = jnp.pad(state[ic], ((0, 0), (POOL_PAD - POOL_STATE, 0), (0, 0)))
    ys, seq = _pool_sample(xs, norm_g, layer, state_pad, c_w, c_scale, ic)
    pool_p = tail[:, POOL_PAD - POOL_STATE:]
    first = POOL_PAD + DEC_SEQ - POOL_STATE
    pool_s = seq[:, first:first + POOL_STATE]
    return yp, ys, pool_p, pool_s


def kernel(x_prompt, x_sample, cache_b_kv0, cache_b_kv1, cache_b_kv2, state_c_pool, norm_mix_g, norm_ffn_g, a_w_in, a_ln_g, a_ln_b, a_w_s, a_b_s, a_w_out, b_w_qkv, b_q_g, b_k_g, b_w_out, c_w, c_scale, ffn_w1, ffn_w3, ffn_w2):
    xp = x_prompt.reshape(MP, D_MODEL)
    xs = jnp.pad(x_sample, ((0, 0), (0, SAMPLE_PAD - DEC_SEQ), (0, 0))).reshape(MS, D_MODEL)
    a_v_s, pool_p, pool_s = [], [], []
    kv_p = [[] for _ in range(B_GROUPS)]
    kv_s = [[] for _ in range(B_GROUPS)]
    ia = ib = ic = 0
    for layer in range(DEPTH):
        kind = layer % N_MIXERS
        if kind == 0:
            xp, xs, v_s = _mixer_a(xp, xs, norm_mix_g, layer, ia, a_w_in, a_ln_g, a_ln_b, a_w_s, a_b_s, a_w_out)
            a_v_s.append(v_s.reshape(DEC_BATCH, SAMPLE_PAD, A_WIDTH)[:, :DEC_SEQ])
            ia += 1
        elif kind == 1:
            caches = (cache_b_kv0, cache_b_kv1, cache_b_kv2)
            xp, xs, kvp, kvs = _mixer_b(xp, xs, norm_mix_g, layer, ib, caches, b_w_qkv, b_q_g, b_k_g, b_w_out)
            for g in range(B_GROUPS):
                kv_p[g].append(kvp[g])
                kv_s[g].append(kvs[g])
            ib += 1
        else:
            xp, xs, pp, ps = _mixer_c(xp, xs, norm_mix_g, layer, ic, state_c_pool, c_w, c_scale)
            pool_p.append(pp)
            pool_s.append(ps)
            ic += 1
        xp, xs = _ffn(xp, xs, norm_ffn_g, ffn_w1, ffn_w3, ffn_w2, layer)
    y_prompt = xp.reshape(BATCH, SEQ, D_MODEL)
    y_sample = xs.reshape(DEC_BATCH, SAMPLE_PAD, D_MODEL)[:, :DEC_SEQ]
    return (y_prompt, y_sample, jnp.stack(a_v_s),
            jnp.stack(kv_p[0]), jnp.stack(kv_p[1]), jnp.stack(kv_p[2]),
            jnp.stack(kv_s[0]), jnp.stack(kv_s[1]), jnp.stack(kv_s[2]),
            jnp.stack(pool_p), jnp.stack(pool_s))
```

```python
import jax
import jax.numpy as jnp
from jax import lax
from jax.experimental import pallas as pl
from jax.experimental.pallas import tpu as pltpu

F32 = jnp.float32
BF16 = jnp.bfloat16

D_MODEL = 2048
BATCH = 4
SEQ = 2048
DEPTH = 4
DEC_BATCH = 8
DEC_SEQ = 4
PAST_LEN = 16384
N_MIXERS = 3
A_CHUNK = 128
A_WIDTH = D_MODEL
A_HEADS = 16
A_HD = A_WIDTH // A_HEADS
B_WINDOWS = (128, 512, 2048)
B_DILATIONS = (1, 4, 16)
B_GROUPS = 3
B_HD = 128
B_HEADS = D_MODEL // B_HD
B_WIDTH = B_HEADS * B_HD
B_SCALE = B_HD ** -0.5
B_BAND = 128
POOL_WINDOWS = (2, 4, 8, 16)
C_GROUPS = 4
C_GW = D_MODEL // C_GROUPS
POOL_STATE = max(POOL_WINDOWS) - 1
POOL_PAD = POOL_STATE + 1
D_FF = ((8 * D_MODEL + 3 * 256 - 1) // (3 * 256)) * 256
EPS = 1e-6

SUBLANES = 8
LANES = 128
VMEM_LIMIT_BYTES = 56 * 1024 * 1024

SAMPLE_PAD = SUBLANES
MP = BATCH * SEQ
MS = DEC_BATCH * SAMPLE_PAD
NEG_BIG = -1e30

TM = 1024
TN = 512
TF_FFN = 256
NORM_ROWS = 256
TAIL_ROWS = 256


def _params(semantics):
    return pltpu.CompilerParams(dimension_semantics=semantics,
                                vmem_limit_bytes=VMEM_LIMIT_BYTES)


def _dot(a, b):
    return jnp.dot(a, b, preferred_element_type=F32)


def _dot_nt(a, b):
    return lax.dot_general(a, b, (((1,), (1,)), ((), ())), preferred_element_type=F32)


def _rms_rows(x, g):
    r = lax.rsqrt(jnp.mean(x * x, axis=-1, keepdims=True) + EPS)
    return (x * r) * g


def _for_row_chunks(rows, chunk, fn):
    chunk = min(chunk, rows)
    n = rows // chunk
    if n == 1:
        fn(pl.ds(0, chunk))
        return

    def body(c, carry):
        fn(pl.ds(pl.multiple_of(c * chunk, chunk), chunk))
        return carry

    lax.fori_loop(0, n, body, 0)


def _layer_vec(stacked):
    return stacked.reshape(stacked.shape[0], 1, stacked.shape[1])


def _hold_after_first_tile(col_fn, last):
    return lambda i, j: (0, jnp.where(i == 0, col_fn(j), last))


def _ffn_body(x_ref, xs_ref, g_ref, w1_ref, w3_ref, w2_ref, o_ref, os_ref, hn_ref):
    i = pl.program_id(0)
    j = pl.program_id(1)
    tm = x_ref.shape[0]

    def gate(h, w1, w3):
        return (jax.nn.silu(_dot(h, w1)) * _dot(h, w3)).astype(BF16)

    def first_step(with_sample):
        w1 = w1_ref[...].astype(BF16)
        w3 = w3_ref[...].astype(BF16)
        w2 = w2_ref[...].astype(BF16)
        for c in range(tm // NORM_ROWS):
            rows = slice(c * NORM_ROWS, (c + 1) * NORM_ROWS)
            x = x_ref[rows, :]
            h = _rms_rows(x, g_ref[...]).astype(BF16)
            hn_ref[rows, :] = h
            o_ref[rows, :] = x + _dot(gate(h, w1, w3), w2)
        if with_sample:
            xs = xs_ref[...]
            h = _rms_rows(xs, g_ref[...]).astype(BF16)
            hn_ref[tm:, :] = h
            os_ref[...] = xs + _dot(gate(h, w1, w3), w2)

    def later_step(with_sample):
        w1 = w1_ref[...].astype(BF16)
        w3 = w3_ref[...].astype(BF16)
        w2 = w2_ref[...].astype(BF16)
        if with_sample:
            gt = gate(hn_ref[...], w1, w3)
            o_ref[...] += _dot(gt[:tm], w2)
            os_ref[...] += _dot(gt[tm:], w2)
        else:
            o_ref[...] += _dot(gate(hn_ref[0:tm, :], w1, w3), w2)

    for with_sample, tile_cond in ((True, i == 0), (False, i != 0)):
        for step_fn, col_cond in ((first_step, j == 0), (later_step, j != 0)):
            @pl.when(tile_cond & col_cond)
            def _(step_fn=step_fn, with_sample=with_sample):
                step_fn(with_sample)


def _ffn(x, xs, g_all, w1_all, w3_all, w2_all, layer):
    return pl.pallas_call(
        _ffn_body,
        grid=(MP // TM, D_FF // TF_FFN),
        in_specs=[
            pl.BlockSpec((TM, D_MODEL), lambda i, j: (i, 0)),
            pl.BlockSpec((MS, D_MODEL), lambda i, j: (0, 0)),
            pl.BlockSpec((None, 1, D_MODEL), lambda i, j: (layer, 0, 0)),
            pl.BlockSpec((None, D_MODEL, TF_FFN), lambda i, j: (layer, 0, j)),
            pl.BlockSpec((None, D_MODEL, TF_FFN), lambda i, j: (layer, 0, j)),
            pl.BlockSpec((None, TF_FFN, D_MODEL), lambda i, j: (layer, j, 0)),
        ],
        out_specs=[pl.BlockSpec((TM, D_MODEL), lambda i, j: (i, 0)),
                   pl.BlockSpec((MS, D_MODEL), lambda i, j: (0, 0))],
        out_shape=[jax.ShapeDtypeStruct((MP, D_MODEL), F32),
                   jax.ShapeDtypeStruct((MS, D_MODEL), F32)],
        scratch_shapes=[pltpu.VMEM((TM + MS, D_MODEL), BF16)],
        compiler_params=_params(("arbitrary", "arbitrary")),
        name="ffn",
    )(x, xs, _layer_vec(g_all), w1_all, w3_all, w2_all)


def _fill_lhs(h_ref, x_ref, xs_ref, g_ref, i, j):
    tm = x_ref.shape[0]

    def prep(x):
        if g_ref is not None:
            x = _rms_rows(x, g_ref[...])
        return x.astype(BF16)

    @pl.when(j == 0)
    def _():
        def chunk(rows):
            h_ref[rows, :] = prep(x_ref[rows, :])
        _for_row_chunks(tm, NORM_ROWS, chunk)

    @pl.when((i == 0) & (j == 0))
    def _():
        h_ref[tm:, :] = prep(xs_ref[...])


def _proj_residual_body(x_ref, xs_ref, w_ref, res_ref, res_s_ref, o_ref, os_ref, h_ref):
    i = pl.program_id(0)
    j = pl.program_id(1)
    tm = x_ref.shape[0]
    _fill_lhs(h_ref, x_ref, xs_ref, None, i, j)
    cols = pl.ds(pl.multiple_of(j * TN, TN), TN)

    @pl.when(i == 0)
    def _():
        acc = _dot(h_ref[...], w_ref[:, cols].astype(BF16))
        o_ref[...] = res_ref[...] + acc[:tm]
        os_ref[...] = res_s_ref[...] + acc[tm:]

    @pl.when(i != 0)
    def _():
        o_ref[...] = res_ref[...] + _dot(h_ref[0:tm, :], w_ref[:, cols].astype(BF16))


def _proj_residual(x, xs, w_all, layer, res, res_s, *, name):
    k = x.shape[1]
    n = w_all.shape[2]
    nj = n // TN
    return pl.pallas_call(
        _proj_residual_body,
        grid=(MP // TM, nj),
        in_specs=[pl.BlockSpec((TM, k), lambda i, j: (i, 0)),
                  pl.BlockSpec((MS, k), lambda i, j: (0, 0)),
                  pl.BlockSpec((None, k, n), lambda i, j: (layer, 0, 0), pipeline_mode=pl.Buffered(1)),
                  pl.BlockSpec((TM, TN), lambda i, j: (i, j)),
                  pl.BlockSpec((MS, TN), lambda i, j: (0, j))],
        out_specs=[pl.BlockSpec((TM, TN), lambda i, j: (i, j)),
                   pl.BlockSpec((MS, TN), _hold_after_first_tile(lambda j: j, nj - 1))],
        out_shape=[jax.ShapeDtypeStruct((MP, n), F32),
                   jax.ShapeDtypeStruct((MS, n), F32)],
        scratch_shapes=[pltpu.VMEM((TM + MS, k), BF16)],
        compiler_params=_params(("arbitrary", "arbitrary")),
        name=name,
    )(x, xs, w_all, res, res_s)


TN_QKV = 1024
QKV_TILES = B_WIDTH // TN_QKV
QKV_COLS = B_GROUPS * 3 * B_WIDTH
KV_W = 2 * B_WIDTH
QKV_HEAD_COLS = QKV_COLS - KV_W
MXU_COLS = 256
QKV_NJ = QKV_COLS // TN_QKV
QKV_LAST_KV = QKV_HEAD_COLS // TN_QKV


def _qkv_body(x_hbm, xs_ref, g_ref, w_ref, gain_ref, o_ref, olast_ref, os_ref, h_ref, x_ref, x_sem):
    i = pl.program_id(0)
    j = pl.program_id(1)
    tm = x_ref.shape[0]

    def x_copy(tile):
        return pltpu.make_async_copy(x_hbm.at[pl.ds(tile * tm, tm), :], x_ref, x_sem)

    @pl.when((i == 0) & (j == 0))
    def _():
        x_copy(0).start()

    @pl.when(j == 0)
    def _():
        x_copy(i).wait()

    _fill_lhs(h_ref, x_ref, xs_ref, g_ref, i, j)

    @pl.when((j == 1) & (i + 1 < pl.num_programs(0)))
    def _():
        x_copy(i + 1).start()

    is_v = (j // QKV_TILES) % 3 == 2
    in_last = j >= QKV_LAST_KV

    def store_heads(acc, out_ref, c0, rows=slice(None)):
        for hh in range(MXU_COLS // B_HD):
            a = acc[:, hh * B_HD:(hh + 1) * B_HD]
            cs = slice(c0 + hh * B_HD, c0 + (hh + 1) * B_HD)
            r = lax.rsqrt(jnp.mean(a * a, axis=-1, keepdims=True) + EPS)
            r = jnp.where(is_v, 1.0, r)
            out_ref[rows, cs] = (a * r) * gain_ref[:, cs]

    def run(out_ref, with_sample):
        h = h_ref[...] if with_sample else h_ref[0:tm, :]
        for c0 in range(0, TN_QKV, MXU_COLS):
            w = w_ref[:, c0:c0 + MXU_COLS].astype(BF16)
            if with_sample:
                acc = _dot(h, w)
                store_heads(acc[:tm], out_ref, c0)
                store_heads(acc[tm:], os_ref, c0)
            elif c0 + MXU_COLS < TN_QKV:
                store_heads(_dot(h, w), out_ref, c0)
            else:
                for r0 in range(0, tm, TAIL_ROWS):
                    rows = slice(r0, r0 + TAIL_ROWS)
                    store_heads(_dot(h_ref[rows, :], w), out_ref, c0, rows)

    for with_sample, tile_cond in ((True, i == 0), (False, i != 0)):
        for out_ref, dest_cond in ((o_ref, jnp.logical_not(in_last)), (olast_ref, in_last)):
            @pl.when(tile_cond & dest_cond)
            def _(out_ref=out_ref, with_sample=with_sample):
                run(out_ref, with_sample)


def _qkv(x, xs, g_all, g_layer, w_all, layer, gain):
    def out_map(i, j):
        return i, jnp.minimum(j, QKV_LAST_KV - 1)

    def out_last_map(i, j):
        return i, jnp.maximum(j - QKV_LAST_KV, 0)

    return pl.pallas_call(
        _qkv_body,
        grid=(MP // TM, QKV_NJ),
        in_specs=[
            pl.BlockSpec(memory_space=pl.ANY),
            pl.BlockSpec((MS, D_MODEL), lambda i, j: (0, 0)),
            pl.BlockSpec((None, 1, D_MODEL), lambda i, j: (g_layer, 0, 0)),
            pl.BlockSpec((None, D_MODEL, TN_QKV), lambda i, j: (layer, 0, j)),
            pl.BlockSpec((1, TN_QKV), lambda i, j: (0, j)),
        ],
        out_specs=[pl.BlockSpec((TM, TN_QKV), out_map),
                   pl.BlockSpec((TM, TN_QKV), out_last_map),
                   pl.BlockSpec((MS, TN_QKV), _hold_after_first_tile(lambda j: j, QKV_NJ - 1))],
        out_shape=[jax.ShapeDtypeStruct((MP, QKV_HEAD_COLS), F32),
                   jax.ShapeDtypeStruct((MP, KV_W), F32),
                   jax.ShapeDtypeStruct((MS, QKV_COLS), F32)],
        scratch_shapes=[pltpu.VMEM((TM + MS, D_MODEL), BF16),
                        pltpu.VMEM((TM, D_MODEL), F32),
                        pltpu.SemaphoreType.DMA(())],
        compiler_params=_params(("arbitrary", "arbitrary")),
        name="qkv",
    )(x, xs, _layer_vec(g_all), w_all, gain)


def _mixing_weights(wsm_ref, ws_ref, t_len, block):
    r = lax.broadcasted_iota(jnp.int32, (t_len, t_len), 0)
    c = lax.broadcasted_iota(jnp.int32, (t_len, t_len), 1)
    keep = c <= r
    if block < t_len:
        keep = keep & ((r // block) == (c // block)) & ((c % block) < DEC_SEQ)
    for h in range(A_HEADS):
        wsm_ref[h] = jnp.where(keep, ws_ref[h], 0.0).astype(BF16)


TN_A_IN = 1024
TN_A_OUT = 256
A_UV = A_WIDTH // TN_A_IN
A_P1 = 2 * A_UV
A_P2 = D_MODEL // TN_A_OUT


assert A_UV * TN_A_IN == D_MODEL


def _amix_prompt_body(x_hbm, xs_ref, g_ref, win_ref, lng_ref, lnb_ref, ws_ref, bias_ref, wout_ref,
                      o_ref, zs_ref, h_ref, u_ref, v_ref, x_ref, wsm_ref, x_sem):
    i = pl.program_id(0)
    j = pl.program_id(1)
    tm = x_ref.shape[0]

    def x_copies(tile):
        rows = pl.ds(tile * tm, tm)
        return [pltpu.make_async_copy(x_hbm.at[rows, pl.ds(k * TN_A_IN, TN_A_IN)], v_ref.at[k], x_sem.at[k])
                for k in range(A_UV)]

    @pl.when((i == 0) & (j == 0))
    def _():
        for copy in x_copies(0):
            copy.start()
        h_ref[tm:, :] = _rms_rows(xs_ref[...], g_ref[...]).astype(BF16)

    @pl.when(j == 0)
    def _():
        for copy in x_copies(i):
            copy.wait()

        def chunk(rows):
            halves = [v_ref[k, rows, :] for k in range(A_UV)]
            ms = sum(jnp.sum(xk * xk, axis=-1, keepdims=True) for xk in halves) / D_MODEL
            r = lax.rsqrt(ms + EPS)
            for k, xk in enumerate(halves):
                cols = slice(k * TN_A_IN, (k + 1) * TN_A_IN)
                h_ref[rows, cols] = ((xk * r) * g_ref[:, cols]).astype(BF16)
                x_ref[rows, cols] = xk
        _for_row_chunks(tm, NORM_ROWS, chunk)

    @pl.when((j == A_P1 + 1) & (i + 1 < pl.num_programs(0)))
    def _():
        for copy in x_copies(i + 1):
            copy.start()

    def phase1(with_sample, dst_ref, slot):
        h = h_ref[...] if with_sample else h_ref[0:tm, :]
        for c0 in range(0, TN_A_IN, MXU_COLS):
            cs = slice(c0, c0 + MXU_COLS)
            w = win_ref[:, cs].astype(BF16)
            if with_sample:
                z = jax.nn.gelu(_dot(h, w), approximate=True)
                dst_ref[slot, :, cs] = z[:tm].astype(dst_ref.dtype)
                zs_ref[:, cs] = z[tm:]
            elif c0 + MXU_COLS < TN_A_IN:
                dst_ref[slot, :, cs] = jax.nn.gelu(_dot(h, w), approximate=True).astype(dst_ref.dtype)
            else:
                for r0 in range(0, tm, TAIL_ROWS):
                    rows = slice(r0, r0 + TAIL_ROWS)
                    z = jax.nn.gelu(_dot(h_ref[rows, :], w), approximate=True)
                    dst_ref[slot, rows, cs] = z.astype(dst_ref.dtype)

    for with_sample, tile_cond in ((True, i == 0), (False, i != 0)):
        for dst_ref, half_cond, slot in ((u_ref, j < A_UV, j),
                                         (v_ref, (j >= A_UV) & (j < A_P1), j - A_UV)):
            @pl.when(tile_cond & half_cond)
            def _(with_sample=with_sample, dst_ref=dst_ref, slot=slot):
                phase1(with_sample, dst_ref, slot)

    @pl.when(j == A_P1)
    def _():
        _mixing_weights(wsm_ref, ws_ref, A_CHUNK, A_CHUNK)

        def chunk(rows):
            vs = [v_ref[k, rows, :] for k in range(A_UV)]
            mu = sum(jnp.sum(vk, axis=-1, keepdims=True) for vk in vs) / A_WIDTH
            ds = [vk - mu for vk in vs]
            var = sum(jnp.sum(dk * dk, axis=-1, keepdims=True) for dk in ds) / A_WIDTH
            r = lax.rsqrt(var + EPS)
            for h in range(A_HEADS):
                k, c = divmod(h * A_HD, TN_A_IN)
                cs = slice(h * A_HD, (h + 1) * A_HD)
                vln = (ds[k][:, c:c + A_HD] * r) * lng_ref[:, cs] + lnb_ref[:, cs]
                mixed = _dot(wsm_ref[h], vln.astype(BF16)) + bias_ref[:, cs]
                h_ref[rows, cs] = (u_ref[k, rows, c:c + A_HD] * mixed).astype(BF16)
        _for_row_chunks(tm, A_CHUNK, chunk)

    @pl.when(j >= A_P1)
    def _():
        cols = pl.ds(pl.multiple_of((j - A_P1) * TN_A_OUT, TN_A_OUT), TN_A_OUT)
        o_ref[...] = x_ref[:, cols] + _dot(h_ref[0:tm, :], wout_ref[...].astype(BF16))


def _amix_prompt(x, xs, g_all, g_layer, win_all, lng_all, lnb_all, ws_all, bias, wout_all, layer):
    def out_col(j):
        return jnp.maximum(j - A_P1, 0)

    def z_col(j):
        return jnp.minimum(j, A_P1 - 1)

    def win_col(j):
        return jnp.where(j < A_P1, j, 0)

    return pl.pallas_call(
        _amix_prompt_body,
        grid=(MP // TM, A_P1 + A_P2),
        in_specs=[
            pl.BlockSpec(memory_space=pl.ANY),
            pl.BlockSpec((MS, D_MODEL), lambda i, j: (0, 0)),
            pl.BlockSpec((None, 1, D_MODEL), lambda i, j: (g_layer, 0, 0)),
            pl.BlockSpec((None, D_MODEL, TN_A_IN), lambda i, j: (layer, 0, win_col(j))),
            pl.BlockSpec((None, 1, A_WIDTH), lambda i, j: (layer, 0, 0)),
            pl.BlockSpec((None, 1, A_WIDTH), lambda i, j: (layer, 0, 0)),
            pl.BlockSpec((None, A_HEADS, A_CHUNK, A_CHUNK), lambda i, j: (layer, 0, 0, 0)),
            pl.BlockSpec((A_CHUNK, A_WIDTH), lambda i, j: (0, 0)),
            pl.BlockSpec((None, A_WIDTH, TN_A_OUT), lambda i, j: (layer, 0, out_col(j))),
        ],
        out_specs=[pl.BlockSpec((TM, TN_A_OUT), lambda i, j: (i, out_col(j))),
                   pl.BlockSpec((MS, TN_A_IN), _hold_after_first_tile(z_col, A_P1 - 1))],
        out_shape=[jax.ShapeDtypeStruct((MP, D_MODEL), F32),
                   jax.ShapeDtypeStruct((MS, 2 * A_WIDTH), F32)],
        scratch_shapes=[pltpu.VMEM((TM + MS, D_MODEL), BF16),
                        pltpu.VMEM((A_UV, TM, TN_A_IN), BF16),
                        pltpu.VMEM((A_UV, TM, TN_A_IN), F32),
                        pltpu.VMEM((TM, D_MODEL), F32),
                        pltpu.VMEM((A_HEADS, A_CHUNK, A_CHUNK), BF16),
                        pltpu.SemaphoreType.DMA((A_UV,))],
        compiler_params=_params(("arbitrary", "arbitrary")),
        name="amix_prompt",
    )(x, xs, _layer_vec(g_all), win_all, _layer_vec(lng_all), _layer_vec(lnb_all), ws_all, bias,
      wout_all)


def _amix_sample_body(x_ref, u_ref, v_ref, lng_ref, lnb_ref, ws_ref, bias_ref, wout_ref,
                      o_ref, vout_ref, gated_ref, wsm_ref):
    j = pl.program_id(0)

    @pl.when(j == 0)
    def _():
        _mixing_weights(wsm_ref, ws_ref, MS, SAMPLE_PAD)
        v = v_ref[...]
        mu = jnp.mean(v, axis=-1, keepdims=True)
        d = v - mu
        var = jnp.mean(d * d, axis=-1, keepdims=True)
        vln = (d * lax.rsqrt(var + EPS)) * lng_ref[...] + lnb_ref[...]
        vout_ref[...] = vln
        vb = vln.astype(BF16)
        for h in range(A_HEADS):
            cs = slice(h * A_HD, (h + 1) * A_HD)
            mixed = _dot(wsm_ref[h], vb[:, cs]) + bias_ref[:, cs]
            gated_ref[:, cs] = (u_ref[:, cs] * mixed).astype(BF16)

    o_ref[...] = x_ref[...] + _dot(gated_ref[...], wout_ref[...].astype(BF16))


def _amix_sample(xs, zs, lng_all, lnb_all, ws_s, bias_s, wout_all, layer):
    return pl.pallas_call(
        _amix_sample_body,
        grid=(D_MODEL // TN,),
        in_specs=[
            pl.BlockSpec((MS, TN), lambda j: (0, j)),
            pl.BlockSpec((MS, A_WIDTH), lambda j: (0, 0)),
            pl.BlockSpec((MS, A_WIDTH), lambda j: (0, 1)),
            pl.BlockSpec((None, 1, A_WIDTH), lambda j: (layer, 0, 0)),
            pl.BlockSpec((None, 1, A_WIDTH), lambda j: (layer, 0, 0)),
            pl.BlockSpec((A_HEADS, MS, MS), lambda j: (0, 0, 0)),
            pl.BlockSpec((MS, A_WIDTH), lambda j: (0, 0)),
            pl.BlockSpec((None, A_WIDTH, TN), lambda j: (layer, 0, j)),
        ],
        out_specs=[pl.BlockSpec((MS, TN), lambda j: (0, j)),
                   pl.BlockSpec((MS, A_WIDTH), lambda j: (0, 0))],
        out_shape=[jax.ShapeDtypeStruct((MS, D_MODEL), F32),
                   jax.ShapeDtypeStruct((MS, A_WIDTH), F32)],
        scratch_shapes=[pltpu.VMEM((MS, A_WIDTH), BF16),
                        pltpu.VMEM((A_HEADS, MS, MS), BF16)],
        compiler_params=_params(("arbitrary",)),
        name="amix_sample",
    )(xs, zs, zs, _layer_vec(lng_all), _layer_vec(lnb_all), ws_s, bias_s, wout_all)


ATTN_HEADS = 1


ATTN_STEP = B_DILATIONS[1]
assert B_DILATIONS == (1, ATTN_STEP, ATTN_STEP * ATTN_STEP)


ATTN_RING = 3
ATTN_BLOCKS = tuple((0, c * B_WIDTH) for c in range(3 * (B_GROUPS - 1) + 1)) + ((1, 0), (1, B_WIDTH))


def _attn_prompt_body(head_hbm, last_hbm, o_ref, buf_ref, sems,
                      acc_ref, m_ref, l_ref, acc4_ref, m4_ref, l4_ref, q4_ref, k4_ref, v4_ref):
    n_heads = pl.num_programs(1)
    step = pl.program_id(0) * n_heads + pl.program_id(1)
    n_steps = pl.num_programs(0) * n_heads

    def copies(s, slot):
        bb = s // n_heads
        col_h = pl.multiple_of((s % n_heads) * B_HD, B_HD)
        return [pltpu.make_async_copy(
            (head_hbm, last_hbm)[src].at[pl.ds(bb, 1), :, pl.ds(col0 + col_h, B_HD)],
            buf_ref.at[slot, idx], sems.at[slot, idx]) for idx, (src, col0) in enumerate(ATTN_BLOCKS)]

    @pl.when(step == 0)
    def _():
        for s in range(ATTN_RING - 1):
            for copy in copies(s, s):
                copy.start()

    ahead = step + (ATTN_RING - 1)

    @pl.when(ahead < n_steps)
    def _():
        for copy in copies(ahead, ahead % ATTN_RING):
            copy.start()

    slot = step % ATTN_RING
    for copy in copies(step, slot):
        copy.wait()
    q0, k0, v0, q1, k1, v1, q2, k2, v2 = [buf_ref.at[slot, idx] for idx in range(len(ATTN_BLOCKS))]

    n = B_BAND
    step_d = ATTN_STEP
    seq4 = SEQ // step_d
    qi = lax.broadcasted_iota(jnp.int32, (n, 2 * n), 0)
    kj = lax.broadcasted_iota(jnp.int32, (n, 2 * n), 1)
    dist = n + qi - kj
    band_mask = (dist >= 0) & (dist <= n)
    qi1 = lax.broadcasted_iota(jnp.int32, (n, n), 0)
    kj1 = lax.broadcasted_iota(jnp.int32, (n, n), 1)
    causal_mask = kj1 <= qi1

    def attend(q, k, v, mask):
        s = jnp.where(mask, _dot_nt((q * B_SCALE).astype(BF16), k.astype(BF16)), NEG_BIG)
        m = jnp.max(s, axis=-1, keepdims=True)
        p = jnp.exp(s - m)
        vb = v.astype(BF16)
        pv = _dot(p.astype(BF16), jnp.concatenate([vb, jnp.ones_like(vb)], axis=1))
        return pv[:, :B_HD], m, pv[:, B_HD:]

    def merge(old, new):
        acc_old, m_old, l_old = old
        acc, m, l = new
        m_new = jnp.maximum(m_old, m)
        a_old = jnp.exp(m_old - m_new)
        a_new = jnp.exp(m - m_new)
        return acc_old * a_old + acc * a_new, m_new, l_old * a_old + l * a_new

    for hh in range(ATTN_HEADS):
        cs = slice(hh * B_HD, (hh + 1) * B_HD)

        for c in range(SEQ // n):
            rows = pl.ds(c * n, n)
            rows_k, mask = (rows, causal_mask) if c == 0 else (pl.ds((c - 1) * n, 2 * n), band_mask)
            acc, m, l = attend(q0[0, rows, cs], k0[0, rows_k, cs], v0[0, rows_k, cs], mask)
            acc_ref[rows, :] = acc
            m_ref[rows, :] = jnp.broadcast_to(m, (n, LANES))
            l_ref[rows, :] = l

        for r in range(step_d):
            for c in range(seq4 // n):
                rows = pl.ds(r + c * n * step_d, n, stride=step_d)
                if c == 0:
                    rows_k, mask = rows, causal_mask
                else:
                    rows_k, mask = pl.ds(r + (c - 1) * n * step_d, 2 * n, stride=step_d), band_mask
                new = attend(q1[0, rows, cs], k1[0, rows_k, cs], v1[0, rows_k, cs], mask)
                acc, m, l = merge((acc_ref[rows, :], m_ref[rows, :], l_ref[rows, :]), new)
                dst = pl.ds(c * n, n)
                acc4_ref[r, dst, :] = acc
                m4_ref[r, dst, :] = m
                l4_ref[r, dst, :] = l

        assert seq4 // step_d == n
        for r in range(step_d):
            rows_r = pl.ds(r, seq4, stride=step_d)
            q4_ref[r] = q2[0, rows_r, cs]
            k4_ref[r] = k2[0, rows_r, cs]
            v4_ref[r] = v2[0, rows_r, cs]
        for r in range(step_d):
            for r2 in range(step_d):
                rows = pl.ds(r2, n, stride=step_d)
                new = attend(q4_ref[r, rows, :], k4_ref[r, rows, :], v4_ref[r, rows, :], causal_mask)
                acc, _, l = merge((acc4_ref[r, rows, :], m4_ref[r, rows, :], l4_ref[r, rows, :]), new)
                acc4_ref[r, rows, :] = acc / l

        for r in range(step_d):
            acc_ref[pl.ds(r, seq4, stride=step_d), :] = acc4_ref[r]
        o_ref[0, :, cs] = acc_ref[...].astype(o_ref.dtype)


def _attn_prompt(qkv_head, kv_last):
    hw = ATTN_HEADS * B_HD
    return pl.pallas_call(
        _attn_prompt_body,
        grid=(BATCH, B_HEADS // ATTN_HEADS),
        in_specs=[pl.BlockSpec(memory_space=pl.ANY), pl.BlockSpec(memory_space=pl.ANY)],
        out_specs=pl.BlockSpec((1, SEQ, hw), lambda b, h: (b, 0, h)),
        out_shape=jax.ShapeDtypeStruct((BATCH, SEQ, B_WIDTH), BF16),
        scratch_shapes=[pltpu.VMEM((ATTN_RING, len(ATTN_BLOCKS), 1, SEQ, B_HD), F32),
                        pltpu.SemaphoreType.DMA((ATTN_RING, len(ATTN_BLOCKS)))]
                       + [pltpu.VMEM((SEQ, B_HD), F32)] * 3
                       + [pltpu.VMEM((ATTN_STEP, SEQ // ATTN_STEP, B_HD), F32)] * 6,
        compiler_params=_params(("arbitrary", "arbitrary")),
        name="attn_prompt",
    )(qkv_head, kv_last)


assert all(B_WINDOWS[g] == B_BAND * B_DILATIONS[g] for g in range(B_GROUPS))
assert PAST_LEN >= max(B_WINDOWS) and B_DILATIONS[0] == 1 and DEC_SEQ <= min(B_DILATIONS[1:])


def _attn_sample_body(q_ref, kvn_ref, c0_ref, c1_ref, c2_ref, o_ref):
    k_heads = slice(0, B_HEADS)
    v_heads = slice(B_HEADS, 2 * B_HEADS)

    def piece(q, k, v, valid=None):
        s = jnp.sum(k * q[None], axis=-1, keepdims=True) * B_SCALE
        if valid is not None:
            s = jnp.where(valid, s, NEG_BIG)
        m = jnp.max(s, axis=0)
        p = jnp.exp(s - m[None])
        return m, jnp.sum(p, axis=0), jnp.sum(p * v, axis=0)

    row0 = lax.broadcasted_iota(jnp.int32, (B_WINDOWS[0], B_HEADS, 1), 0)
    strided = (None, c1_ref, c2_ref)
    for t in range(DEC_SEQ):
        terms = []
        for g in range(B_GROUPS):
            q = q_ref[g, 0, t]
            if g == 0:
                valid = B_WINDOWS[0] + t - row0 <= B_BAND
                terms.append(piece(q, c0_ref[0, :, k_heads, :], c0_ref[0, :, v_heads, :], valid))
                new = slice(0, t + 1)
            else:
                c_ref = strided[g]
                terms.append(piece(q, c_ref[0, :, t, k_heads, :], c_ref[0, :, t, v_heads, :]))
                new = slice(t, t + 1)
            terms.append(piece(q, kvn_ref[g, 0, new, k_heads, :], kvn_ref[g, 0, new, v_heads, :]))
        m_all = terms[0][0]
        for m, _, _ in terms[1:]:
            m_all = jnp.maximum(m_all, m)
        l_all = jnp.zeros((B_HEADS, 1), F32)
        acc_all = jnp.zeros((B_HEADS, B_HD), F32)
        for m, l, acc in terms:
            w = jnp.exp(m - m_all)
            l_all = l_all + l * w
            acc_all = acc_all + acc * w
        o_ref[0, t] = acc_all / l_all
    o_ref[0, DEC_SEQ:] = jnp.zeros((SAMPLE_PAD - DEC_SEQ, B_HEADS, B_HD), F32)


def _attn_sample(q_s, kv_new, caches):
    kv_rows = 2 * B_HEADS

    def by_residue(g):
        dil = B_DILATIONS[g]
        view = caches[g].reshape(DEC_BATCH, B_BAND, dil, kv_rows, B_HD)
        return view, pl.BlockSpec((1, B_BAND, DEC_SEQ, kv_rows, B_HD), lambda b: (b, 0, 0, 0, 0))

    c1, c1_spec = by_residue(1)
    c2, c2_spec = by_residue(2)
    return pl.pallas_call(
        _attn_sample_body,
        grid=(DEC_BATCH,),
        in_specs=[
            pl.BlockSpec((B_GROUPS, 1, SAMPLE_PAD, B_HEADS, B_HD), lambda b: (0, b, 0, 0, 0)),
            pl.BlockSpec((B_GROUPS, 1, SAMPLE_PAD, kv_rows, B_HD), lambda b: (0, b, 0, 0, 0)),
            pl.BlockSpec((1, B_WINDOWS[0], kv_rows, B_HD), lambda b: (b, 0, 0, 0)),
            c1_spec, c2_spec,
        ],
        out_specs=pl.BlockSpec((1, SAMPLE_PAD, B_HEADS, B_HD), lambda b: (b, 0, 0, 0)),
        out_shape=jax.ShapeDtypeStruct((DEC_BATCH, SAMPLE_PAD, B_HEADS, B_HD), F32),
        compiler_params=_params(("parallel",)),
        name="attn_sample",
    )(q_s, kv_new, caches[0], c1, c2)


POOL_ROWS = 256
assert all(w & (w - 1) == 0 for w in POOL_WINDOWS)


def _pool_prompt_body(x_ref, g_ref, w_ref, scale_ref, o_ref, tail_ref, r_ref, sum_ref, z_ref):
    grp = pl.program_id(1)
    n_chunks = SEQ // POOL_ROWS

    @pl.when(grp == 0)
    def _():
        def chunk(rows):
            x = x_ref[0, rows, :]
            r_ref[rows, :] = lax.rsqrt(jnp.mean(x * x, axis=-1, keepdims=True) + EPS)
        _for_row_chunks(SEQ, NORM_ROWS, chunk)

    for gi, w in enumerate(POOL_WINDOWS):
        @pl.when(grp == gi)
        def _(gi=gi, w=w):
            cs = slice(gi * C_GW, (gi + 1) * C_GW)

            def normed(c):
                rows = slice(c * POOL_ROWS, (c + 1) * POOL_ROWS)
                return (x_ref[0, rows, cs] * r_ref[rows, :]) * g_ref[:, cs]

            sum_ref[0:POOL_PAD, :] = jnp.zeros((POOL_PAD, C_GW), F32)
            for c in range(n_chunks):
                h = normed(c)
                sum_ref[POOL_PAD + c * POOL_ROWS:POOL_PAD + (c + 1) * POOL_ROWS, :] = h
                if c == n_chunks - 1:
                    tail_ref[0, :, cs] = h[POOL_ROWS - POOL_PAD:]
            k = 1
            while k < w:
                for c in reversed(range(n_chunks)):
                    r0 = POOL_PAD + c * POOL_ROWS
                    sum_ref[r0:r0 + POOL_ROWS, :] = (sum_ref[r0:r0 + POOL_ROWS, :]
                                                    + sum_ref[r0 - k:r0 - k + POOL_ROWS, :])
                k *= 2
            for c in range(n_chunks):
                r0 = POOL_PAD + c * POOL_ROWS
                pos = c * POOL_ROWS + lax.broadcasted_iota(jnp.int32, (POOL_ROWS, 1), 0)
                cnt = jnp.minimum(w, pos + 1).astype(F32)
                z_ref[c * POOL_ROWS:(c + 1) * POOL_ROWS, :] = (
                    sum_ref[r0:r0 + POOL_ROWS, :] / cnt - normed(c)).astype(BF16)
            y = _dot(z_ref[...], w_ref[...].astype(BF16))
            o_ref[0] = x_ref[0, :, cs] + y * scale_ref[...]


def _pool_prompt(x, g_all, g_layer, w_all, scale_all, layer):
    return pl.pallas_call(
        _pool_prompt_body,
        grid=(BATCH, C_GROUPS),
        in_specs=[pl.BlockSpec((1, SEQ, D_MODEL), lambda b, g: (b, 0, 0)),
                  pl.BlockSpec((None, 1, D_MODEL), lambda b, g: (g_layer, 0, 0)),
                  pl.BlockSpec((None, None, C_GW, C_GW), lambda b, g: (layer, g, 0, 0)),
                  pl.BlockSpec((None, 1, C_GW), lambda b, g: (layer, 0, g))],
        out_specs=[pl.BlockSpec((1, SEQ, C_GW), lambda b, g: (b, 0, g)),
                   pl.BlockSpec((1, POOL_PAD, D_MODEL), lambda b, g: (b, 0, 0))],
        out_shape=[jax.ShapeDtypeStruct((BATCH, SEQ, D_MODEL), F32),
                   jax.ShapeDtypeStruct((BATCH, POOL_PAD, D_MODEL), F32)],
        scratch_shapes=[pltpu.VMEM((SEQ, 1), F32),
                        pltpu.VMEM((POOL_PAD + SEQ, C_GW), F32),
                        pltpu.VMEM((SEQ, C_GW), BF16)],
        compiler_params=_params(("arbitrary", "arbitrary")),
        name="pool_prompt",
    )(x, _layer_vec(g_all), w_all, _layer_vec(scale_all))


def _pool_sample_body(x_ref, g_ref, state_ref, w_ref, scale_ref, o_ref, seq_ref, z_ref):
    for b in range(DEC_BATCH):
        rows = slice(b * SAMPLE_PAD, (b + 1) * SAMPLE_PAD)
        seq_ref[b, 0:POOL_PAD, :] = state_ref[b]
        seq_ref[b, POOL_PAD:, :] = _rms_rows(x_ref[rows, :], g_ref[...])
    for b in range(DEC_BATCH):
        rows = slice(b * SAMPLE_PAD, (b + 1) * SAMPLE_PAD)
        for gi, w in enumerate(POOL_WINDOWS):
            cs = slice(gi * C_GW, (gi + 1) * C_GW)
            cur = seq_ref[b, POOL_PAD:POOL_PAD + SAMPLE_PAD, cs]
            tot = cur
            for k in range(1, w):
                tot = tot + seq_ref[b, POOL_PAD - k:POOL_PAD - k + SAMPLE_PAD, cs]
            pos = PAST_LEN + lax.broadcasted_iota(jnp.int32, (SAMPLE_PAD, 1), 0)
            cnt = jnp.minimum(w, pos + 1).astype(F32)
            z_ref[rows, cs] = (tot / cnt - cur).astype(BF16)
    for gi in range(C_GROUPS):
        cs = slice(gi * C_GW, (gi + 1) * C_GW)
        y = _dot(z_ref[:, cs], w_ref[gi].astype(BF16))
        o_ref[:, cs] = x_ref[:, cs] + y * scale_ref[:, cs]


def _pool_sample(xs, g_all, g_layer, state_pad, w_all, scale_all, layer):
    seq_rows = POOL_PAD + SAMPLE_PAD
    return pl.pallas_call(
        _pool_sample_body,
        grid=(1,),
        in_specs=[
            pl.BlockSpec((MS, D_MODEL), lambda i: (0, 0)),
            pl.BlockSpec((None, 1, D_MODEL), lambda i: (g_layer, 0, 0)),
            pl.BlockSpec((DEC_BATCH, POOL_PAD, D_MODEL), lambda i: (0, 0, 0)),
            pl.BlockSpec((None, C_GROUPS, C_GW, C_GW), lambda i: (layer, 0, 0, 0)),
            pl.BlockSpec((None, 1, D_MODEL), lambda i: (layer, 0, 0)),
        ],
        out_specs=[pl.BlockSpec((MS, D_MODEL), lambda i: (0, 0)),
                   pl.BlockSpec((DEC_BATCH, seq_rows, D_MODEL), lambda i: (0, 0, 0))],
        out_shape=[jax.ShapeDtypeStruct((MS, D_MODEL), F32),
                   jax.ShapeDtypeStruct((DEC_BATCH, seq_rows, D_MODEL), F32)],
        scratch_shapes=[pltpu.VMEM((MS, D_MODEL), BF16)],
        compiler_params=_params(("arbitrary",)),
        name="pool_sample",
    )(xs, _layer_vec(g_all), state_pad, w_all, _layer_vec(scale_all))


def _mixer_a(xp, xs, norm_g, layer, ia, a_w_in, a_ln_g, a_ln_b, a_w_s, a_b_s, a_w_out):
    b_s = a_b_s[ia]
    bias_p = jnp.repeat(jnp.transpose(b_s), A_HD, axis=1)
    yp, zs = _amix_prompt(xp, xs, norm_g, layer, a_w_in, a_ln_g, a_ln_b, a_w_s, bias_p, a_w_out, ia)
    ws_s = jnp.tile(a_w_s[ia][:, :SAMPLE_PAD, :SAMPLE_PAD], (1, DEC_BATCH, DEC_BATCH))
    bias_s = jnp.tile(jnp.repeat(jnp.transpose(b_s[:, :SAMPLE_PAD]), A_HD, axis=1), (DEC_BATCH, 1))
    ys, v_s = _amix_sample(xs, zs, a_ln_g, a_ln_b, ws_s, bias_s, a_w_out, ia)
    return yp, ys, v_s


def _mixer_b(xp, xs, norm_g, layer, ib, caches, b_w_qkv, b_q_g, b_k_g, b_w_out):
    ones = jnp.ones((B_GROUPS, B_WIDTH), F32)
    gain = jnp.stack([jnp.tile(b_q_g[ib], (1, B_HEADS)), jnp.tile(b_k_g[ib], (1, B_HEADS)), ones], axis=1)
    gain = gain.reshape(1, QKV_COLS)
    qkv_head, kv_last, qkv_s = _qkv(xp, xs, norm_g, layer, b_w_qkv, ib, gain)
    qkv_head = qkv_head.reshape(BATCH, SEQ, QKV_HEAD_COLS)
    kv_last = kv_last.reshape(BATCH, SEQ, KV_W)
    op = _attn_prompt(qkv_head, kv_last)
    qkv_s = jnp.transpose(qkv_s.reshape(DEC_BATCH, SAMPLE_PAD, B_GROUPS, 3 * B_HEADS, B_HD), (2, 0, 1, 3, 4))
    q_s = qkv_s[:, :, :, :B_HEADS]
    kv_new = qkv_s[:, :, :, B_HEADS:]
    c = [cc[ib].reshape(DEC_BATCH, cc.shape[2], 2 * B_HEADS, B_HD) for cc in caches]
    os_ = _attn_sample(q_s, kv_new, c)
    yp, ys = _proj_residual(op.reshape(MP, B_WIDTH), os_.reshape(MS, B_WIDTH), b_w_out, ib, xp, xs,
                            name="b_out")
    new_p = []
    for g in range(B_GROUPS):
        keep = min(B_WINDOWS[g], SEQ)
        if g == B_GROUPS - 1:
            kv = kv_last[:, SEQ - keep:]
        else:
            kv = qkv_head[:, SEQ - keep:, (3 * g + 1) * B_WIDTH:(3 * g + 3) * B_WIDTH]
        new_p.append(kv.reshape(BATCH, keep, 2, B_HEADS, B_HD))
    new_s = [kv_new[g, :, :DEC_SEQ].reshape(DEC_BATCH, DEC_SEQ, 2, B_HEADS, B_HD)
             for g in range(B_GROUPS)]
    return yp, ys, new_p, new_s


def _mixer_c(xp, xs, norm_g, layer, ic, state, c_w, c_scale):
    yp, tail = _pool_prompt(xp.reshape(BATCH, SEQ, D_MODEL), norm_g, layer, c_w, c_scale, ic)
    yp = yp.reshape(MP, D_MODEL)
    state_pad = jnp.pad(state[ic], ((0, 0), (POOL_PAD - POOL_STATE, 0), (0, 0)))
    ys, seq = _pool_sample(xs, norm_g, layer, state_pad, c_w, c_scale, ic)
    pool_p = tail[:, POOL_PAD - POOL_STATE:]
    first = POOL_PAD + DEC_SEQ - POOL_STATE
    pool_s = seq[:, first:first + POOL_STATE]
    return yp, ys, pool_p, pool_s


def kernel(x_prompt, x_sample, cache_b_kv0, cache_b_kv1, cache_b_kv2, state_c_pool, norm_mix_g, norm_ffn_g, a_w_in, a_ln_g, a_ln_b, a_w_s, a_b_s, a_w_out, b_w_qkv, b_q_g, b_k_g, b_w_out, c_w, c_scale, ffn_w1, ffn_w3, ffn_w2):
    xp = x_prompt.reshape(MP, D_MODEL)
    xs = jnp.pad(x_sample, ((0, 0), (0, SAMPLE_PAD - DEC_SEQ), (0, 0))).reshape(MS, D_MODEL)
    a_v_s, pool_p, pool_s = [], [], []
    kv_p = [[] for _ in range(B_GROUPS)]
    kv_s = [[] for _ in range(B_GROUPS)]
    ia = ib = ic = 0
    for layer in range(DEPTH):
        kind = layer % N_MIXERS
        if kind == 0:
            xp, xs, v_s = _mixer_a(xp, xs, norm_mix_g, layer, ia, a_w_in, a_ln_g, a_ln_b, a_w_s, a_b_s, a_w_out)
            a_v_s.append(v_s.reshape(DEC_BATCH, SAMPLE_PAD, A_WIDTH)[:, :DEC_SEQ])
            ia += 1
        elif kind == 1:
            caches = (cache_b_kv0, cache_b_kv1, cache_b_kv2)
            xp, xs, kvp, kvs = _mixer_b(xp, xs, norm_mix_g, layer, ib, caches, b_w_qkv, b_q_g, b_k_g, b_w_out)
            for g in range(B_GROUPS):
                kv_p[g].append(kvp[g])
                kv_s[g].append(kvs[g])
            ib += 1
        else:
            xp, xs, pp, ps = _mixer_c(xp, xs, norm_mix_g, layer, ic, state_c_pool, c_w, c_scale)
            pool_p.append(pp)
            pool_s.append(ps)
            ic += 1
        xp, xs = _ffn(xp, xs, norm_ffn_g, ffn_w1, ffn_w3, ffn_w2, layer)
    y_prompt = xp.reshape(BATCH, SEQ, D_MODEL)
    y_sample = xs.reshape(DEC_BATCH, SAMPLE_PAD, D_MODEL)[:, :DEC_SEQ]
    return (y_prompt, y_sample, jnp.stack(a_v_s),
            jnp.stack(kv_p[0]), jnp.stack(kv_p[1]), jnp.stack(kv_p[2]),
            jnp.stack(kv_s[0]), jnp.stack(kv_s[1]), jnp.stack(kv_s[2]),
            jnp.stack(pool_p), jnp.stack(pool_s))
```
